```python
import jax, jax.numpy as jnp
from jax import lax
import numpy as np

D_MODEL = 1024
BATCH = 16
SEQ = 256
DEPTH = 1
DEC_BATCH = 8
DEC_SEQ = 2048
PAST_LEN = 512

GRID_W = 64
GLA_HEADS = 4
GLA_DV = (D_MODEL // 2) // GLA_HEADS
GLA_DK = GLA_DV // 2
GLA_GATE_RANK = 16
GLA_GATE_NORM = 16.0
GLA_CHUNK = 64
HEAD_DIM = 64
ATT_HEADS = (D_MODEL // 2) // HEAD_DIM
KV_HEADS = ATT_HEADS // 4
Q_BLOCK = 128
ROPE_THETA = 10000.0
ROPE_AXIS_PAIRS = HEAD_DIM // 4
N_EXPERTS = 32
TOP_K = 4
EXPERT_FF = D_MODEL
SWIGLU_LIMIT = 7.0
SWIGLU_ALPHA = 1.702
RMS_EPS = 1e-6

SPLIT_SIZES = (GLA_HEADS * GLA_DK, GLA_HEADS * GLA_DK, GLA_HEADS * GLA_DV, GLA_HEADS * GLA_DV,
               GLA_GATE_RANK, GLA_GATE_RANK,
               ATT_HEADS * HEAD_DIM, KV_HEADS * HEAD_DIM, KV_HEADS * HEAD_DIM)
D_IN = sum(SPLIT_SIZES)
D_MIX = GLA_HEADS * GLA_DV + ATT_HEADS * HEAD_DIM

kernel_name = "hybrid_gla_gqa_moe_diffusion_step"


def rms_norm(x, g):
    xf = x.astype(jnp.float32)
    y = xf * lax.rsqrt(jnp.mean(xf * xf, axis=-1, keepdims=True) + RMS_EPS)
    return (y * g.astype(jnp.float32)).astype(x.dtype)


def axial_rope(n_tok):
    rows = n_tok // GRID_W
    r, col = jnp.meshgrid(jnp.arange(rows), jnp.arange(GRID_W), indexing='ij')
    r = r.reshape(-1).astype(jnp.float32)
    col = col.reshape(-1).astype(jnp.float32)
    inv = 1.0 / (ROPE_THETA ** (jnp.arange(ROPE_AXIS_PAIRS, dtype=jnp.float32) / ROPE_AXIS_PAIRS))
    ang = jnp.concatenate([r[:, None] * inv, col[:, None] * inv], axis=-1)
    return jnp.cos(ang), jnp.sin(ang)


def apply_rope(x, cos, sin):
    xf = x.astype(jnp.float32).reshape(*x.shape[:-1], HEAD_DIM // 2, 2)
    x1, x2 = xf[..., 0], xf[..., 1]
    c = cos[None, :, None, :]
    s = sin[None, :, None, :]
    out = jnp.stack([x1 * c - x2 * s, x1 * s + x2 * c], axis=-1)
    return out.reshape(x.shape).astype(x.dtype)


def gla_chunked(q, k, v, log_a, s0):
    b_, h_, L, _ = q.shape
    n = L // GLA_CHUNK

    def chunks(t):
        return jnp.moveaxis(t.reshape(b_, h_, n, GLA_CHUNK, t.shape[-1]), 2, 0)

    lower = jnp.tril(jnp.ones((GLA_CHUNK, GLA_CHUNK), dtype=bool))[:, :, None]

    def step(S, xs):
        qc, kc, vc, gc = xs
        cum = jnp.cumsum(gc, axis=2)
        o_inter = jnp.einsum('bhik,bhkv->bhiv', qc * jnp.exp(cum), S)
        rel = jnp.exp(jnp.where(lower, cum[:, :, :, None, :] - cum[:, :, None, :, :], -jnp.inf))
        scores = jnp.einsum('bhik,bhjk,bhijk->bhij', qc, kc, rel)
        o_intra = jnp.einsum('bhij,bhjv->bhiv', scores, vc)
        last = cum[:, :, -1:, :]
        S = (jnp.exp(last[:, :, 0, :])[..., None] * S
             + jnp.einsum('bhjk,bhjv->bhkv', kc * jnp.exp(last - cum), vc))
        return S, o_inter + o_intra

    s_fin, o = lax.scan(step, s0, (chunks(q), chunks(k), chunks(v), chunks(log_a)))
    o = jnp.moveaxis(o, 0, 2).reshape(b_, h_, L, v.shape[-1])
    return o, s_fin


def block_attention(q, k, v):
    B, Sq, H, D = q.shape
    kvh = k.shape[2]
    g = H // kvh
    nq = Sq // Q_BLOCK
    qb = jnp.moveaxis(q.reshape(B, nq, Q_BLOCK, kvh, g, D), 1, 0)
    scale = HEAD_DIM ** -0.5

    def one(qblk):
        s = jnp.einsum('bqkgd,bskd->bkgqs', qblk, k, preferred_element_type=jnp.float32) * scale
        p = jax.nn.softmax(s, axis=-1).astype(v.dtype)
        return jnp.einsum('bkgqs,bskd->bqkgd', p, v)

    o = lax.map(one, qb)
    return jnp.moveaxis(o, 0, 1).reshape(B, Sq, H * D)


def token_mixer(h, w_in, w_gate_f, b_gate_f, w_gate_b, b_gate_b, gla_norm, q_norm, k_norm, w_out,
                rope, ctx_k, ctx_v, s0_f, s0_b):
    B, S, _ = h.shape
    proj = h @ w_in
    qa, ka, va, ga, lr_f, lr_b, qb, kb, vb = jnp.split(
        proj, np.cumsum(SPLIT_SIZES)[:-1].tolist(), axis=-1)

    def heads(t, n):
        return t.reshape(B, S, n, -1).transpose(0, 2, 1, 3).astype(jnp.float32)

    qa = heads(qa, GLA_HEADS) * (GLA_DK ** -0.5)
    ka = heads(ka, GLA_HEADS)
    va = heads(va, GLA_HEADS)
    la_f = heads(jax.nn.log_sigmoid((lr_f @ w_gate_f + b_gate_f).astype(jnp.float32)) / GLA_GATE_NORM, GLA_HEADS)
    la_b = heads(jax.nn.log_sigmoid((lr_b @ w_gate_b + b_gate_b).astype(jnp.float32)) / GLA_GATE_NORM, GLA_HEADS)
    o_f, s_f = gla_chunked(qa, ka, va, la_f, s0_f)
    o_r, s_b = gla_chunked(qa[:, :, ::-1], ka[:, :, ::-1], va[:, :, ::-1], la_b[:, :, ::-1], s0_b)
    o_gla = (o_f + o_r[:, :, ::-1]).transpose(0, 2, 1, 3)
    o_gla = rms_norm(o_gla, gla_norm) * jax.nn.silu(ga.reshape(B, S, GLA_HEADS, GLA_DV).astype(jnp.float32))
    o_gla = o_gla.reshape(B, S, GLA_HEADS * GLA_DV).astype(h.dtype)

    qh = rms_norm(qb.reshape(B, S, ATT_HEADS, HEAD_DIM), q_norm)
    kh = rms_norm(kb.reshape(B, S, KV_HEADS, HEAD_DIM), k_norm)
    vh = vb.reshape(B, S, KV_HEADS, HEAD_DIM)
    if rope is None:
        o_att = block_attention(qh, kh, vh)
    else:
        cos, sin = rope
        keys = jnp.concatenate([apply_rope(kh, cos, sin), ctx_k.astype(kh.dtype)], axis=1)
        vals = jnp.concatenate([vh, ctx_v.astype(vh.dtype)], axis=1)
        o_att = block_attention(apply_rope(qh, cos, sin), keys, vals)

    out = jnp.concatenate([o_gla, o_att.astype(h.dtype)], axis=-1) @ w_out
    return out, kh, vh, s_f, s_b


def moe_ffn(h, w_router, b_router, w_gu, b_gu, w_down, b_down):
    B, S, D = h.shape
    tok = h.reshape(-1, D)
    n_tok = tok.shape[0]
    logits = (tok @ w_router + b_router).astype(jnp.float32)
    top_logit, top_e = lax.top_k(logits, TOP_K)
    gate_w = jax.nn.softmax(top_logit, axis=-1)
    flat_e = top_e.reshape(-1)
    order = jnp.argsort(flat_e)
    e_sorted = flat_e[order]
    src = order // TOP_K
    group_sizes = jnp.bincount(flat_e, length=N_EXPERTS).astype(jnp.int32)
    xs = tok[src]
    gu = lax.ragged_dot(xs, w_gu, group_sizes) + b_gu[e_sorted]
    g, u = gu[:, :EXPERT_FF], gu[:, EXPERT_FF:]
    g = jnp.minimum(g, SWIGLU_LIMIT)
    u = jnp.clip(u, -SWIGLU_LIMIT, SWIGLU_LIMIT)
    act = (u + 1) * (g * jax.nn.sigmoid(SWIGLU_ALPHA * g))
    out = lax.ragged_dot(act, w_down, group_sizes) + b_down[e_sorted]
    out = out * gate_w.reshape(-1)[order][:, None].astype(out.dtype)
    y = jax.ops.segment_sum(out, src, num_segments=n_tok)
    return y.reshape(B, S, D)


def trunk_layer(x, cond, w_ada, b_ada, norm_mix, mix_p, norm_ffn, ffn_p, rope, ctx_k, ctx_v, s0_f, s0_b):
    m = (jax.nn.silu(cond) @ w_ada + b_ada)[:, None, :]
    sh1, sc1, g1, sh2, sc2, g2 = jnp.split(m, 6, axis=-1)
    h = rms_norm(x, norm_mix) * (1 + sc1) + sh1
    mo, k_c, v_c, s_f, s_b = token_mixer(h, *mix_p, rope, ctx_k, ctx_v, s0_f, s0_b)
    x = x + g1 * mo
    h = rms_norm(x, norm_ffn) * (1 + sc2) + sh2
    x = x + g2 * moe_ffn(h, *ffn_p)
    return x, k_c, v_c, s_f, s_b


def setup_inputs(seed: int = 0) -> dict:
    key = jax.random.key(seed)
    ks = jax.random.split(key, 32)

    def nrm(k, shape, s):
        return jax.random.normal(k, shape, jnp.float32) * s

    def gain(k, shape):
        return 1.0 + 0.02 * jax.random.normal(k, shape, jnp.float32)

    return {
        'x_prompt': nrm(ks[0], (BATCH, SEQ, D_MODEL), 1.0),
        'x_sample': nrm(ks[1], (DEC_BATCH, DEC_SEQ, D_MODEL), 1.0),
        'c': nrm(ks[2], (DEC_BATCH, D_MODEL), 1.0),
        'cache_k': nrm(ks[3], (DEC_BATCH, DEPTH, PAST_LEN, KV_HEADS, HEAD_DIM), 1.0),
        'cache_v': nrm(ks[4], (DEC_BATCH, DEPTH, PAST_LEN, KV_HEADS, HEAD_DIM), 1.0),
        'state_gla_fwd': nrm(ks[5], (DEC_BATCH, DEPTH, GLA_HEADS, GLA_DK, GLA_DV), 0.5),
        'state_gla_bwd': nrm(ks[6], (DEC_BATCH, DEPTH, GLA_HEADS, GLA_DK, GLA_DV), 0.5),
        'c_ctx': nrm(ks[7], (D_MODEL,), 1.0),
        'w_ada': nrm(ks[8], (DEPTH, D_MODEL, 6 * D_MODEL), D_MODEL ** -0.5),
        'b_ada': nrm(ks[9], (DEPTH, 6 * D_MODEL), 0.02),
        'norm_mix': gain(ks[10], (DEPTH, D_MODEL)),
        'w_in': nrm(ks[11], (DEPTH, D_MODEL, D_IN), D_MODEL ** -0.5),
        'w_gate_f': nrm(ks[12], (DEPTH, GLA_GATE_RANK, GLA_HEADS * GLA_DK), GLA_GATE_RANK ** -0.5),
        'b_gate_f': nrm(ks[13], (DEPTH, GLA_HEADS * GLA_DK), 0.1),
        'w_gate_b': nrm(ks[14], (DEPTH, GLA_GATE_RANK, GLA_HEADS * GLA_DK), GLA_GATE_RANK ** -0.5),
        'b_gate_b': nrm(ks[15], (DEPTH, GLA_HEADS * GLA_DK), 0.1),
        'gla_norm': gain(ks[16], (DEPTH, GLA_DV)),
        'q_norm': gain(ks[17], (DEPTH, HEAD_DIM)),
        'k_norm': gain(ks[18], (DEPTH, HEAD_DIM)),
        'w_out': nrm(ks[19], (DEPTH, D_MIX, D_MODEL), D_MIX ** -0.5),
        'norm_ffn': gain(ks[20], (DEPTH, D_MODEL)),
        'w_router': nrm(ks[21], (DEPTH, D_MODEL, N_EXPERTS), D_MODEL ** -0.5),
        'b_router': nrm(ks[22], (DEPTH, N_EXPERTS), 0.01),
        'w_gu': nrm(ks[23], (DEPTH, N_EXPERTS, D_MODEL, 2 * EXPERT_FF), D_MODEL ** -0.5),
        'b_gu': nrm(ks[24], (DEPTH, N_EXPERTS, 2 * EXPERT_FF), 0.01),
        'w_down': nrm(ks[25], (DEPTH, N_EXPERTS, EXPERT_FF, D_MODEL), EXPERT_FF ** -0.5),
        'b_down': nrm(ks[26], (DEPTH, N_EXPERTS, D_MODEL), 0.01),
        'final_norm': gain(ks[27], (D_MODEL,)),
    }


def reference(x_prompt, x_sample, c, cache_k, cache_v, state_gla_fwd, state_gla_bwd, c_ctx,
              w_ada, b_ada, norm_mix, w_in, w_gate_f, b_gate_f, w_gate_b, b_gate_b, gla_norm,
              q_norm, k_norm, w_out, norm_ffn, w_router, b_router, w_gu, b_gu, w_down, b_down,
              final_norm):
    bp = x_prompt.shape[0]
    zero_state = jnp.zeros((bp, GLA_HEADS, GLA_DK, GLA_DV), jnp.float32)
    cond_ctx = jnp.broadcast_to(c_ctx, (bp, D_MODEL))
    rope = axial_rope(x_sample.shape[1])

    xp, xs = x_prompt, x_sample
    new_k, new_v, new_sf, new_sb = [], [], [], []
    for l in range(DEPTH):
        mix_p = (w_in[l], w_gate_f[l], b_gate_f[l], w_gate_b[l], b_gate_b[l], gla_norm[l],
                 q_norm[l], k_norm[l], w_out[l])
        ffn_p = (w_router[l], b_router[l], w_gu[l], b_gu[l], w_down[l], b_down[l])
        xp, k_l, v_l, sf_l, sb_l = trunk_layer(xp, cond_ctx, w_ada[l], b_ada[l], norm_mix[l], mix_p,
                                               norm_ffn[l], ffn_p, None, None, None,
                                               zero_state, zero_state)
        new_k.append(k_l)
        new_v.append(v_l)
        new_sf.append(sf_l)
        new_sb.append(sb_l)
        xs, _, _, _, _ = trunk_layer(xs, c, w_ada[l], b_ada[l], norm_mix[l], mix_p, norm_ffn[l], ffn_p,
                                     rope, cache_k[:, l], cache_v[:, l],
                                     state_gla_fwd[:, l].astype(jnp.float32),
                                     state_gla_bwd[:, l].astype(jnp.float32))

    y_prompt = rms_norm(xp, final_norm)
    y_sample = rms_norm(xs, final_norm)
    new_cache_k = jnp.stack(new_k, axis=1).astype(x_prompt.dtype)
    new_cache_v = jnp.stack(new_v, axis=1).astype(x_prompt.dtype)
    new_state_gla_fwd = jnp.stack(new_sf, axis=1).astype(x_prompt.dtype)
    new_state_gla_bwd = jnp.stack(new_sb, axis=1).astype(x_prompt.dtype)
    return (y_prompt, y_sample, new_cache_k, new_cache_v, new_state_gla_fwd, new_state_gla_bwd)
```

```python
import functools

import numpy as np
import jax
import jax.numpy as jnp
from jax import lax
from jax.experimental import pallas as pl
from jax.experimental.pallas import tpu as pltpu

F32 = jnp.float32
BF16 = jnp.bfloat16

D_MODEL = 1024
GRID_W = 64
GLA_HEADS = 4
GLA_DV = 128
GLA_DK = 64
GLA_GATE_RANK = 16
GLA_GATE_NORM = 16.0
HEAD_DIM = 64
ATT_HEADS = 8
KV_HEADS = 2
ROPE_THETA = 10000.0
ROPE_AXIS_PAIRS = HEAD_DIM // 4
N_EXPERTS = 32
TOP_K = 4
EXPERT_FF = 1024
SWIGLU_LIMIT = 7.0
SWIGLU_ALPHA = 1.702
RMS_EPS = 1e-6

W_QK = GLA_HEADS * GLA_DK
W_V = GLA_HEADS * GLA_DV
W_ATT = ATT_HEADS * HEAD_DIM
W_KV = KV_HEADS * HEAD_DIM
LANES = 128
D_IN_PAD = 2 * W_QK + 2 * W_V + W_ATT + 2 * W_KV + LANES

TOKEN_TILE = 256
GLA_BLOCK = 256
GLA_LEVELS = ((256, 32), (32, 8), (8, 1))
ATT_Q_TILE = 256
EXPERT_TILE = 256
COMBINE_TILE = 128
VMEM_LIMIT = 56 * 1024 * 1024


def _split_bf16(x):
    hi = x.astype(BF16)
    lo = (x - hi.astype(F32)).astype(BF16)
    return hi, lo


def _dot(a, b):
    return jnp.dot(a, b, preferred_element_type=F32)


def _dot_nt(a, b):
    return lax.dot_general(a, b, (((1,), (1,)), ((), ())), preferred_element_type=F32)


def _dot_tn(a, b):
    return lax.dot_general(a, b, (((0,), (0,)), ((), ())), preferred_element_type=F32)


def _adaln_kernel(c_ref, w_ref, b_ref, o_ref):
    c = c_ref[...]
    a = c * jax.nn.sigmoid(c)
    ah, al = _split_bf16(a)
    wh, wl = _split_bf16(w_ref[...])
    o_ref[...] = _dot(ah, wh) + _dot(al, wh) + _dot(ah, wl) + b_ref[...]


def _adaln(conds, w_ada, b_ada):
    m, d = conds.shape
    n = w_ada.shape[1]
    tn = 512
    return pl.pallas_call(
        _adaln_kernel,
        grid=(n // tn,),
        in_specs=[pl.BlockSpec((m, d), lambda j: (0, 0)),
                  pl.BlockSpec((d, tn), lambda j: (0, j)),
                  pl.BlockSpec((1, tn), lambda j: (0, j))],
        out_specs=pl.BlockSpec((m, tn), lambda j: (0, j)),
        out_shape=jax.ShapeDtypeStruct((m, n), F32),
        name="adaln",
    )(conds, w_ada, b_ada.reshape(1, n))


def _rope_rotate(x, c, s):
    n = x.shape[-1]
    lane = lax.broadcasted_iota(jnp.int32, x.shape, 1)
    partner = jnp.where((lane & 1) == 0, pltpu.roll(x, n - 1, 1), pltpu.roll(x, 1, 1))
    return x * c + partner * s


def _premix_kernel(*refs, use_rope):
    (x_ref, mod_ref, nmix_ref, win_ref, wgh_ref, wgl_ref, bg_ref, bq_ref, bk_ref,
     qn_ref, kn_ref) = refs[:11]
    n_in = 11
    if use_rope:
        cq_ref, sq_ref, ck_ref, sk_ref = refs[11:15]
        n_in = 15
    (qa_o, ka_o, va_o, ga_o, laf_o, lab_o, q_o, k_o, v_o, kc_o, vc_o) = refs[n_in:]

    x = x_ref[0]
    mod = mod_ref[0]
    sh1 = mod[0:1]
    sc1 = mod[1:2]
    ms = jnp.mean(x * x, axis=-1, keepdims=True)
    h = x * lax.rsqrt(ms + RMS_EPS) * nmix_ref[...]
    h = h * (1.0 + sc1) + sh1
    proj = _dot(h.astype(BF16), win_ref[...])

    o = 0
    qa_o[0] = proj[:, o:o + W_QK] * (GLA_DK ** -0.5); o += W_QK
    ka_o[0] = proj[:, o:o + W_QK]; o += W_QK
    va_o[0] = proj[:, o:o + W_V]; o += W_V
    ga_o[0] = proj[:, o:o + W_V]; o += W_V
    qb = proj[:, o:o + W_ATT]; o += W_ATT
    kb = proj[:, o:o + W_KV]; o += W_KV
    vb = proj[:, o:o + W_KV]; o += W_KV
    lr = proj[:, o:o + LANES]

    lh, ll = _split_bf16(lr)
    xg = _dot(lh, wgh_ref[...]) + _dot(ll, wgh_ref[...]) + _dot(lh, wgl_ref[...]) + bg_ref[...]
    la = (jnp.minimum(xg, 0.0) - jnp.log1p(jnp.exp(-jnp.abs(xg)))) * (1.0 / GLA_GATE_NORM)
    laf_o[0] = la[:, :W_QK]
    lab_o[0] = la[:, W_QK:]

    def head_norm(t, ones_ref, g_ref):
        sh, sl = _split_bf16(t * t)
        msq = (_dot(sh, ones_ref[...]) + _dot(sl, ones_ref[...])) * (1.0 / HEAD_DIM)
        return t * lax.rsqrt(msq + RMS_EPS) * g_ref[...]

    qh = head_norm(qb, bq_ref, qn_ref)
    kh = head_norm(kb, bk_ref, kn_ref)
    kc_o[0] = kh
    vc_o[0] = vb
    if use_rope:
        qh = _rope_rotate(qh, cq_ref[...], sq_ref[...])
        kr = _rope_rotate(kh, ck_ref[...], sk_ref[...])
    else:
        kr = kh
    q_o[0] = (qh * (HEAD_DIM ** -0.5)).astype(BF16)
    k_o[0] = kr.astype(BF16)
    v_o[0] = vb.astype(BF16)


def _premix(x, mod, mod_shared, wp, rope):
    b, s, d = x.shape
    tm = min(TOKEN_TILE, s)
    use_rope = rope is not None
    const = lambda shape: pl.BlockSpec(shape, lambda i, j: (0,) * len(shape))
    tok = lambda w: pl.BlockSpec((1, tm, w), lambda i, j: (i, j, 0))
    mod_map = (lambda i, j: (0, 0, 0)) if mod_shared else (lambda i, j: (i, 0, 0))
    in_specs = [tok(d), pl.BlockSpec((1, 8, d), mod_map), const((1, d)),
                const((d, D_IN_PAD)), const((LANES, 2 * W_QK)), const((LANES, 2 * W_QK)),
                const((1, 2 * W_QK)), const((W_ATT, W_ATT)), const((W_KV, W_KV)),
                const((1, W_ATT)), const((1, W_KV))]
    args = [x, mod, wp["norm_mix"], wp["w_in"], wp["wg_hi"], wp["wg_lo"], wp["b_gate"],
            wp["ones_q"], wp["ones_k"], wp["q_norm"], wp["k_norm"]]
    if use_rope:
        seq = lambda w: pl.BlockSpec((tm, w), lambda i, j: (j, 0))
        in_specs += [seq(W_ATT), seq(W_ATT), seq(W_KV), seq(W_KV)]
        args += list(rope)
    widths = [(W_QK, F32), (W_QK, F32), (W_V, F32), (W_V, F32), (W_QK, F32), (W_QK, F32),
              (W_ATT, BF16), (W_KV, BF16), (W_KV, BF16), (W_KV, F32), (W_KV, F32)]
    return pl.pallas_call(
        functools.partial(_premix_kernel, use_rope=use_rope),
        grid=(b, s // tm),
        in_specs=in_specs,
        out_specs=[tok(w) for w, _ in widths],
        out_shape=[jax.ShapeDtypeStruct((b, s, w), dt) for w, dt in widths],
        compiler_params=pltpu.CompilerParams(
            dimension_semantics=("parallel", "parallel"), vmem_limit_bytes=VMEM_LIMIT),
        name="premix_rope" if use_rope else "premix",
    )(*args)


def _bcast_rows(x, n_par, p_rows, row):
    w = x.shape[-1]
    r = x.reshape(n_par, p_rows, w)[:, row:row + 1, :]
    return jnp.broadcast_to(r, (n_par, p_rows, w)).reshape(n_par * p_rows, w)


def _gla_kernel(q_ref, k_ref, v_ref, la_ref, s0_ref, o_ref, sf_ref, st_scr, *, rows, levels):
    blk = pl.program_id(1)

    @pl.when(blk == 0)
    def _():
        st_scr[...] = s0_ref[0]

    q = q_ref[0]
    k = k_ref[0]
    la = la_ref[0]
    vb = v_ref[0].astype(BF16)

    ri = lax.broadcasted_iota(jnp.int32, (rows, rows), 0)
    ci = lax.broadcasted_iota(jnp.int32, (rows, rows), 1)
    tril = jnp.where(ri >= ci, 1.0, 0.0).astype(BF16)
    hi = la.astype(BF16)
    r1 = la - hi.astype(F32)
    mid = r1.astype(BF16)
    lo = (r1 - mid.astype(F32)).astype(BF16)
    cum = _dot(tril, hi) + _dot(tril, mid) + _dot(tril, lo)

    ridx = lax.broadcasted_iota(jnp.int32, (rows, W_QK), 0)
    s_acc = [None] * GLA_HEADS
    for (par, sub) in levels:
        groups = par // sub
        n_par = rows // par
        pi = (ridx % par) // sub
        if sub > 1:
            own = _bcast_rows(cum, rows // sub, sub, 0)
            qt = q * jnp.exp(cum - own)
        else:
            qt = q
        qps, kps = [], []
        for p in range(1 if sub > 1 else 0, groups):
            cp = _bcast_rows(cum, n_par, par, p * sub)
            km = (pi < p) if sub > 1 else (pi <= p)
            ek = jnp.where(km, cp - cum, 0.0)
            kps.append(jnp.where(km, k * jnp.exp(ek), 0.0).astype(BF16))
            qps.append(jnp.where(pi == p, qt, 0.0).astype(BF16))
        ng = len(qps)
        lane_c = lax.broadcasted_iota(jnp.int32, (rows, LANES * ng), 1)
        same_parent = (ri // par) == (ci // par)
        for hp in range(GLA_HEADS // 2):
            qc = jnp.concatenate([a[:, LANES * hp:LANES * (hp + 1)] for a in qps], axis=1)
            kc = jnp.concatenate([a[:, LANES * hp:LANES * (hp + 1)] for a in kps], axis=1)
            for hh in range(2):
                qh = jnp.where(((lane_c % LANES) // GLA_DK) == hh, qc, jnp.zeros_like(qc))
                sl = _dot_nt(qh, kc)
                if par < rows:
                    sl = jnp.where(same_parent, sl, 0.0)
                h = 2 * hp + hh
                s_acc[h] = sl if s_acc[h] is None else s_acc[h] + sl

    q0 = (q * jnp.exp(cum)).astype(BF16)
    last = cum[rows - 1:rows, :]
    kd = (k * jnp.exp(last - cum)).astype(BF16)
    br = lax.broadcasted_iota(jnp.int32, (2 * GLA_DV, 2 * GLA_DK), 0)
    bc = lax.broadcasted_iota(jnp.int32, (2 * GLA_DV, 2 * GLA_DK), 1)
    blockdiag = (br // GLA_DV) == (bc // GLA_DK)
    for hp in range(GLA_HEADS // 2):
        st = st_scr[hp]
        o_inter = _dot_nt(q0[:, LANES * hp:LANES * (hp + 1)], st.astype(BF16))
        o_intra = jnp.concatenate(
            [_dot(s_acc[2 * hp + hh].astype(BF16),
                  vb[:, GLA_DV * (2 * hp + hh):GLA_DV * (2 * hp + hh + 1)]) for hh in range(2)],
            axis=1)
        o_ref[0, :, 2 * GLA_DV * hp:2 * GLA_DV * (hp + 1)] = o_inter + o_intra
        upd = _dot_tn(vb[:, 2 * GLA_DV * hp:2 * GLA_DV * (hp + 1)],
                      kd[:, LANES * hp:LANES * (hp + 1)])
        st_scr[hp] = (jnp.exp(last[:, LANES * hp:LANES * (hp + 1)]) * st
                      + jnp.where(blockdiag, upd, 0.0))

    @pl.when(blk == pl.num_programs(1) - 1)
    def _():
        sf_ref[0] = st_scr[...]


def _gla(q, k, v, la, s0t):
    n, l, _ = q.shape
    rows = min(GLA_BLOCK, l)
    levels = tuple((min(p, rows), s) for p, s in GLA_LEVELS)
    tok = lambda w: pl.BlockSpec((1, rows, w), lambda i, j: (i, j, 0))
    st_spec = pl.BlockSpec((1, 2, 2 * GLA_DV, 2 * GLA_DK), lambda i, j: (i, 0, 0, 0))
    return pl.pallas_call(
        functools.partial(_gla_kernel, rows=rows, levels=levels),
        grid=(n, l // rows),
        in_specs=[tok(W_QK), tok(W_QK), tok(W_V), tok(W_QK), st_spec],
        out_specs=[tok(W_V), st_spec],
        out_shape=[jax.ShapeDtypeStruct((n, l, W_V), F32),
                   jax.ShapeDtypeStruct(s0t.shape, F32)],
        scratch_shapes=[pltpu.VMEM((2, 2 * GLA_DV, 2 * GLA_DK), F32)],
        compiler_params=pltpu.CompilerParams(
            dimension_semantics=("parallel", "arbitrary"), vmem_limit_bytes=VMEM_LIMIT),
        name="gla",
    )(q, k, v, la, s0t)


def _state_to_blockdiag_t(s):
    n = s.shape[0]
    st = jnp.swapaxes(s, -1, -2).reshape(n, 2, 2, GLA_DV, GLA_DK)
    z = jnp.zeros_like(st[:, :, 0])
    top = jnp.concatenate([st[:, :, 0], z], axis=-1)
    bot = jnp.concatenate([z, st[:, :, 1]], axis=-1)
    return jnp.concatenate([top, bot], axis=-2)


def _blockdiag_t_to_state(sb):
    n = sb.shape[0]
    h0 = sb[:, :, :GLA_DV, :GLA_DK]
    h1 = sb[:, :, GLA_DV:, GLA_DK:]
    st = jnp.stack([h0, h1], axis=2).reshape(n, GLA_HEADS, GLA_DV, GLA_DK)
    return jnp.swapaxes(st, -1, -2)


def _attn_kernel(*refs, s_self, n_ctx):
    if n_ctx:
        q_ref, k_ref, v_ref, ck_ref, cv_ref, o_ref, kg_scr, v_scr = refs
    else:
        q_ref, k_ref, v_ref, o_ref, kg_scr, v_scr = refs

    @pl.when(pl.program_id(1) == 0)
    def _():
        def put(kk, vv, start, n):
            lane = lax.broadcasted_iota(jnp.int32, kk.shape, 1)
            kg_scr[0, start:start + n, :] = jnp.where(lane < HEAD_DIM, kk, jnp.zeros_like(kk))
            kg_scr[1, start:start + n, :] = jnp.where(lane >= HEAD_DIM, kk, jnp.zeros_like(kk))
            v_scr[start:start + n, :] = vv
        put(k_ref[0], v_ref[0], 0, s_self)
        if n_ctx:
            put(ck_ref[0].astype(BF16), cv_ref[0].astype(BF16), s_self, n_ctx)

    q = q_ref[0]
    vv = v_scr[...]
    for m in range(ATT_HEADS // KV_HEADS):
        qm = q[:, LANES * m:LANES * (m + 1)]
        og = []
        for g in range(KV_HEADS):
            s = _dot_nt(qm, kg_scr[g])
            mx = jnp.max(s, axis=-1, keepdims=True)
            p = jnp.exp(s - mx)
            l = jnp.sum(p, axis=-1, keepdims=True)
            og.append(_dot(p.astype(BF16), vv) / l)
        lane = lax.broadcasted_iota(jnp.int32, og[0].shape, 1)
        o_ref[0, :, LANES * m:LANES * (m + 1)] = jnp.where(lane < HEAD_DIM, og[0], og[1]).astype(BF16)


def _attention(q, k, v, cache_k=None, cache_v=None):
    b, s, _ = q.shape
    n_ctx = 0 if cache_k is None else cache_k.shape[1]
    tq = min(ATT_Q_TILE, s)
    sk = s + n_ctx
    full = lambda n, w: pl.BlockSpec((1, n, w), lambda i, j: (i, 0, 0))
    in_specs = [pl.BlockSpec((1, tq, W_ATT), lambda i, j: (i, j, 0)), full(s, W_KV), full(s, W_KV)]
    args = [q, k, v]
    if n_ctx:
        in_specs += [full(n_ctx, W_KV), full(n_ctx, W_KV)]
        args += [cache_k, cache_v]
    return pl.pallas_call(
        functools.partial(_attn_kernel, s_self=s, n_ctx=n_ctx),
        grid=(b, s // tq),
        in_specs=in_specs,
        out_specs=pl.BlockSpec((1, tq, W_ATT), lambda i, j: (i, j, 0)),
        out_shape=jax.ShapeDtypeStruct((b, s, W_ATT), BF16),
        scratch_shapes=[pltpu.VMEM((KV_HEADS, sk, W_KV), BF16), pltpu.VMEM((sk, W_KV), BF16)],
        compiler_params=pltpu.CompilerParams(
            dimension_semantics=("parallel", "arbitrary"), vmem_limit_bytes=VMEM_LIMIT),
        name="attention_ctx" if n_ctx else "attention",
    )(*args)


def _postmix_kernel(of_ref, or_ref, ga_ref, oa_ref, x_ref, mod_ref, gn_ref, wout_ref, nffn_ref,
                    wrh_ref, wrl_ref, br_ref, x1_o, h2_o, te_o, tg_o):
    mod = mod_ref[0]
    g1, sh2, sc2 = mod[2:3], mod[3:4], mod[4:5]
    og = of_ref[0] + or_ref[0]
    ga = ga_ref[0]
    parts = []
    for h in range(GLA_HEADS):
        blk = og[:, GLA_DV * h:GLA_DV * (h + 1)]
        ms = jnp.mean(blk * blk, axis=-1, keepdims=True)
        gh = ga[:, GLA_DV * h:GLA_DV * (h + 1)]
        parts.append((blk * lax.rsqrt(ms + RMS_EPS) * gn_ref[...] * (gh * jax.nn.sigmoid(gh))).astype(BF16))
    mix = jnp.concatenate(parts + [oa_ref[0]], axis=1)
    mo = _dot(mix, wout_ref[...])
    x1 = x_ref[0] + g1 * mo
    x1_o[0] = x1
    ms = jnp.mean(x1 * x1, axis=-1, keepdims=True)
    h2 = x1 * lax.rsqrt(ms + RMS_EPS) * nffn_ref[...]
    h2 = h2 * (1.0 + sc2) + sh2
    h2_o[0] = h2

    hh, hl = _split_bf16(h2)
    lt = (_dot_nt(wrh_ref[...], hh) + _dot_nt(wrh_ref[...], hl) + _dot_nt(wrl_ref[...], hh)
          + br_ref[...])
    eidx = lax.broadcasted_iota(jnp.int32, lt.shape, 0)
    vals, ids = [], []
    for _ in range(TOP_K):
        mx = jnp.max(lt, axis=0, keepdims=True)
        idx = jnp.min(jnp.where(lt == mx, eidx, N_EXPERTS), axis=0, keepdims=True)
        lt = jnp.where(eidx == idx, -jnp.inf, lt)
        vals.append(mx)
        ids.append(idx)
    ws = [jnp.exp(vv - vals[0]) for vv in vals]
    tot = ws[0] + ws[1] + ws[2] + ws[3]
    tm = lt.shape[1]
    te_o[0] = jnp.concatenate(ids + [jnp.zeros((8 - TOP_K, tm), jnp.int32)], axis=0)
    tg_o[0] = jnp.concatenate([w / tot for w in ws] + [jnp.zeros((8 - TOP_K, tm), F32)], axis=0)


def _postmix(o_f, o_r, ga, o_att, x, mod, mod_shared, wp):
    b, s, d = x.shape
    tm = min(TOKEN_TILE, s)
    nt = s // tm
    const = lambda shape: pl.BlockSpec(shape, lambda i, j: (0,) * len(shape))
    tok = lambda w: pl.BlockSpec((1, tm, w), lambda i, j: (i, j, 0))
    mod_map = (lambda i, j: (0, 0, 0)) if mod_shared else (lambda i, j: (i, 0, 0))
    lane_tok = pl.BlockSpec((1, 8, tm), lambda i, j: (i * nt + j, 0, 0))
    return pl.pallas_call(
        _postmix_kernel,
        grid=(b, nt),
        in_specs=[tok(W_V), tok(W_V), tok(W_V), tok(W_ATT), tok(d), pl.BlockSpec((1, 8, d), mod_map),
                  const((1, GLA_DV)), const((d, d)), const((1, d)),
                  const((N_EXPERTS, d)), const((N_EXPERTS, d)), const((N_EXPERTS, 1))],
        out_specs=[tok(d), tok(d), lane_tok, lane_tok],
        out_shape=[jax.ShapeDtypeStruct((b, s, d), F32), jax.ShapeDtypeStruct((b, s, d), F32),
                   jax.ShapeDtypeStruct((b * nt, 8, tm), jnp.int32),
                   jax.ShapeDtypeStruct((b * nt, 8, tm), F32)],
        compiler_params=pltpu.CompilerParams(
            dimension_semantics=("parallel", "parallel"), vmem_limit_bytes=VMEM_LIMIT),
        name="postmix",
    )(o_f, o_r, ga, o_att, x, mod, wp["gla_norm"], wp["w_out"], wp["norm_ffn"],
      wp["wr_hi"], wp["wr_lo"], wp["b_router"])


def _row_gather(idx_ref, n_rows, src_hbm, dst, sem, idx_of):
    def body(i, carry):
        pltpu.make_async_copy(src_hbm.at[pl.ds(idx_of(idx_ref, i), 1)], dst(i), sem).start()
        return carry
    lax.fori_loop(0, n_rows, body, 0)


def _dispatch_kernel(src_ref, h_hbm, xs_ref, sem):
    tg = xs_ref.shape[0]

    def start(i, carry):
        pltpu.make_async_copy(h_hbm.at[pl.ds(src_ref[0, 0, i], 1)], xs_ref.at[pl.ds(i, 1)], sem).start()
        return carry
    lax.fori_loop(0, tg, start, 0)
    pltpu.make_async_copy(h_hbm.at[pl.ds(0, tg)], xs_ref, sem).wait()


def _dispatch(h2, src_rows, n_rows):
    t, d = h2.shape
    tg = EXPERT_TILE
    nt = n_rows // tg
    return pl.pallas_call(
        _dispatch_kernel,
        grid=(nt,),
        in_specs=[pl.BlockSpec((1, 1, tg), lambda i: (i, 0, 0), memory_space=pltpu.SMEM),
                  pl.BlockSpec(memory_space=pl.ANY)],
        out_specs=pl.BlockSpec((tg, d), lambda i: (i, 0)),
        out_shape=jax.ShapeDtypeStruct((n_rows, d), F32),
        scratch_shapes=[pltpu.SemaphoreType.DMA(())],
        compiler_params=pltpu.CompilerParams(dimension_semantics=("arbitrary",)),
        name="dispatch",
    )(src_rows.reshape(nt, 1, tg), h2)


def _experts_kernel(te_ref, nv_ref, xs_ref, wgu_ref, bgu_ref, wd_ref, bd_ref, o_ref, wgu_bf, wd_bf):
    i = pl.program_id(0)
    valid = i < nv_ref[0]
    new_expert = jnp.logical_or(i == 0, te_ref[i] != te_ref[jnp.maximum(i - 1, 0)])

    @pl.when(jnp.logical_and(valid, new_expert))
    def _():
        wgu_bf[...] = wgu_ref[0].astype(BF16)
        wd_bf[...] = wd_ref[0].astype(BF16)

    @pl.when(valid)
    def _():
        x = xs_ref[...].astype(BF16)
        gu = _dot(x, wgu_bf[...]) + bgu_ref[0]
        g = jnp.minimum(gu[:, :EXPERT_FF], SWIGLU_LIMIT)
        u = jnp.clip(gu[:, EXPERT_FF:], -SWIGLU_LIMIT, SWIGLU_LIMIT)
        act = (u + 1.0) * (g * jax.nn.sigmoid(SWIGLU_ALPHA * g))
        o_ref[...] = _dot(act.astype(BF16), wd_bf[...]) + bd_ref[0]

    @pl.when(jnp.logical_not(valid))
    def _():
        o_ref[...] = jnp.zeros_like(o_ref)


def _experts(xs, tile_expert, n_valid, w_gu, b_gu, w_down, b_down):
    n_rows, d = xs.shape
    tm = EXPERT_TILE
    nt = n_rows // tm
    ne, _, ff2 = w_gu.shape
    ff = ff2 // 2
    grid_spec = pltpu.PrefetchScalarGridSpec(
        num_scalar_prefetch=2,
        grid=(nt,),
        in_specs=[pl.BlockSpec((tm, d), lambda i, te, nv: (i, 0)),
                  pl.BlockSpec((1, d, ff2), lambda i, te, nv: (te[i], 0, 0)),
                  pl.BlockSpec((1, 1, ff2), lambda i, te, nv: (te[i], 0, 0)),
                  pl.BlockSpec((1, ff, d), lambda i, te, nv: (te[i], 0, 0)),
                  pl.BlockSpec((1, 1, d), lambda i, te, nv: (te[i], 0, 0))],
        out_specs=pl.BlockSpec((tm, d), lambda i, te, nv: (i, 0)),
        scratch_shapes=[pltpu.VMEM((d, ff2), BF16), pltpu.VMEM((ff, d), BF16)],
    )
    return pl.pallas_call(
        _experts_kernel,
        grid_spec=grid_spec,
        out_shape=jax.ShapeDtypeStruct((n_rows, d), F32),
        compiler_params=pltpu.CompilerParams(
            dimension_semantics=("arbitrary",), vmem_limit_bytes=VMEM_LIMIT),
        name="experts",
    )(tile_expert, n_valid, xs, w_gu, b_gu.reshape(ne, 1, ff2), w_down, b_down.reshape(ne, 1, d))


def _combine_kernel(pos_ref, ys_hbm, tg_ref, x1_ref, mod_ref, fn_ref, y_ref, buf, sem):
    tm = x1_ref.shape[1]

    def start(t, carry):
        for r in range(TOP_K):
            pltpu.make_async_copy(ys_hbm.at[pl.ds(pos_ref[0, 0, TOP_K * t + r], 1)],
                                  buf.at[r, pl.ds(t, 1)], sem).start()
        return carry
    lax.fori_loop(0, tm, start, 0)
    for r in range(TOP_K):
        pltpu.make_async_copy(ys_hbm.at[pl.ds(0, tm)], buf.at[r], sem).wait()

    gates = jnp.transpose(tg_ref[0])
    y = gates[:, 0:1] * buf[0]
    for r in range(1, TOP_K):
        y = y + gates[:, r:r + 1] * buf[r]
    g2 = mod_ref[0][5:6]
    x2 = x1_ref[0] + g2 * y
    ms = jnp.mean(x2 * x2, axis=-1, keepdims=True)
    y_ref[0] = x2 * lax.rsqrt(ms + RMS_EPS) * fn_ref[...]


def _combine(ys, pos, gates_t, x1, mod, mod_shared, final_norm):
    b, s, d = x1.shape
    tm = min(COMBINE_TILE, s)
    nt = s // tm
    mod_map = (lambda i, j: (0, 0, 0)) if mod_shared else (lambda i, j: (i, 0, 0))
    return pl.pallas_call(
        _combine_kernel,
        grid=(b, nt),
        in_specs=[pl.BlockSpec((1, 1, TOP_K * tm), lambda i, j: (i * nt + j, 0, 0), memory_space=pltpu.SMEM),
                  pl.BlockSpec(memory_space=pl.ANY),
                  pl.BlockSpec((1, LANES, tm), lambda i, j: (i * nt + j, 0, 0)),
                  pl.BlockSpec((1, tm, d), lambda i, j: (i, j, 0)),
                  pl.BlockSpec((1, 8, d), mod_map),
                  pl.BlockSpec((1, d), lambda i, j: (0, 0))],
        out_specs=pl.BlockSpec((1, tm, d), lambda i, j: (i, j, 0)),
        out_shape=jax.ShapeDtypeStruct((b, s, d), F32),
        scratch_shapes=[pltpu.VMEM((TOP_K, tm, d), F32), pltpu.SemaphoreType.DMA(())],
        compiler_params=pltpu.CompilerParams(
            dimension_semantics=("arbitrary", "arbitrary"), vmem_limit_bytes=VMEM_LIMIT),
        name="combine",
    )(pos.reshape(b * nt, 1, TOP_K * tm), ys, gates_t, x1, mod, final_norm.reshape(1, d))


def _route(top_e, n_tok, n_rows):
    tm = EXPERT_TILE
    onehot = jnp.sum((top_e[:, :, None] == jnp.arange(N_EXPERTS)[None, None, :]).astype(jnp.int32), axis=1)
    excl = jnp.cumsum(onehot, axis=0) - onehot
    counts = jnp.sum(onehot, axis=0)
    tiles = (counts + tm - 1) // tm
    tile_end = jnp.cumsum(tiles)
    row_off = (tile_end - tiles) * tm
    rank = jnp.take_along_axis(excl, top_e, axis=1)
    pos = row_off[top_e] + rank
    src = jnp.zeros((n_rows,), jnp.int32).at[pos.reshape(-1)].set(
        jnp.repeat(jnp.arange(n_tok, dtype=jnp.int32), TOP_K))
    n_valid = tile_end[-1].astype(jnp.int32).reshape(1)
    tile_ids = jnp.minimum(jnp.arange(n_rows // tm), n_valid[0] - 1)
    tile_expert = jnp.searchsorted(tile_end, tile_ids, side="right").astype(jnp.int32)
    return pos.astype(jnp.int32), src, tile_expert, n_valid


def _rope_tables(n_tok):
    rows = n_tok // GRID_W
    r, col = jnp.meshgrid(jnp.arange(rows), jnp.arange(GRID_W), indexing="ij")
    r = r.reshape(-1).astype(F32)
    col = col.reshape(-1).astype(F32)
    inv = 1.0 / (ROPE_THETA ** (jnp.arange(ROPE_AXIS_PAIRS, dtype=F32) / ROPE_AXIS_PAIRS))
    ang = jnp.concatenate([r[:, None] * inv, col[:, None] * inv], axis=-1)
    c64 = jnp.repeat(jnp.cos(ang), 2, axis=-1)
    sign = jnp.tile(jnp.array([-1.0, 1.0], F32), HEAD_DIM // 2)
    s64 = jnp.repeat(jnp.sin(ang), 2, axis=-1) * sign
    return (jnp.tile(c64, (1, ATT_HEADS)), jnp.tile(s64, (1, ATT_HEADS)),
            jnp.tile(c64, (1, KV_HEADS)), jnp.tile(s64, (1, KV_HEADS)))


def _prep_weights(w_in, w_gate_f, b_gate_f, w_gate_b, b_gate_b, gla_norm, q_norm, k_norm, w_out,
                  norm_mix, norm_ffn, w_router, b_router):
    d = w_in.shape[0]
    o_lr = 2 * W_QK + 2 * W_V
    o_q = o_lr + 2 * GLA_GATE_RANK
    o_k = o_q + W_ATT
    head_order = np.array([0, 4, 1, 5, 2, 6, 3, 7])
    perm = (head_order[:, None] * HEAD_DIM + np.arange(HEAD_DIM)[None, :]).reshape(-1)
    w_q = w_in[:, o_q:o_k][:, perm]
    lr_pad = jnp.zeros((d, LANES - 2 * GLA_GATE_RANK), w_in.dtype)
    w_in_p = jnp.concatenate([w_in[:, :o_lr], w_q, w_in[:, o_k:], w_in[:, o_lr:o_q], lr_pad], axis=1)
    wg = jnp.zeros((LANES, 2 * W_QK), F32)
    wg = wg.at[:GLA_GATE_RANK, :W_QK].set(w_gate_f)
    wg = wg.at[GLA_GATE_RANK:2 * GLA_GATE_RANK, W_QK:].set(w_gate_b)
    wg_hi, wg_lo = _split_bf16(wg)
    ones_q = jnp.asarray(np.kron(np.eye(ATT_HEADS), np.ones((HEAD_DIM, HEAD_DIM))), BF16)
    ones_k = jnp.asarray(np.kron(np.eye(KV_HEADS), np.ones((HEAD_DIM, HEAD_DIM))), BF16)
    w_out_p = jnp.concatenate([w_out[:W_V], w_out[W_V:][perm]], axis=0).astype(BF16)
    wr_hi, wr_lo = _split_bf16(w_router.T)
    return {
        "norm_mix": norm_mix.reshape(1, d), "w_in": w_in_p.astype(BF16),
        "wg_hi": wg_hi, "wg_lo": wg_lo,
        "b_gate": jnp.concatenate([b_gate_f, b_gate_b]).reshape(1, 2 * W_QK),
        "ones_q": ones_q, "ones_k": ones_k,
        "q_norm": jnp.tile(q_norm, ATT_HEADS).reshape(1, W_ATT),
        "k_norm": jnp.tile(k_norm, KV_HEADS).reshape(1, W_KV),
        "gla_norm": gla_norm.reshape(1, GLA_DV), "w_out": w_out_p,
        "norm_ffn": norm_ffn.reshape(1, d), "wr_hi": wr_hi, "wr_lo": wr_lo,
        "b_router": b_router.reshape(N_EXPERTS, 1),
    }


def _mixer(x, mod, mod_shared, wp, rope, cache_k, cache_v, s0_f, s0_b):
    b, s, _ = x.shape
    qa, ka, va, ga, la_f, la_b, q, k, v, kc, vc = _premix(x, mod, mod_shared, wp, rope)
    rev = lambda t: t[:, ::-1]
    cat = lambda a, c: jnp.concatenate([a, c], axis=0)
    o2, sfin = _gla(cat(qa, rev(qa)), cat(ka, rev(ka)), cat(va, rev(va)), cat(la_f, rev(la_b)),
                    _state_to_blockdiag_t(cat(s0_f, s0_b)))
    o_att = _attention(q, k, v, cache_k, cache_v)
    x1, h2, te, tg = _postmix(o2[:b], rev(o2[b:]), ga, o_att, x, mod, mod_shared, wp)
    states = _blockdiag_t_to_state(sfin)
    return x1, h2, te, tg, kc, vc, states[:b], states[b:]


def kernel(x_prompt, x_sample, c, cache_k, cache_v, state_gla_fwd, state_gla_bwd, c_ctx, w_ada, b_ada,
           norm_mix, w_in, w_gate_f, b_gate_f, w_gate_b, b_gate_b, gla_norm, q_norm, k_norm, w_out,
           norm_ffn, w_router, b_router, w_gu, b_gu, w_down, b_down, final_norm):
    bp, sp, d = x_prompt.shape
    bs, ss, _ = x_sample.shape
    assert w_ada.shape[0] == 1, "single-layer trunk"
    wp = _prep_weights(w_in[0], w_gate_f[0], b_gate_f[0], w_gate_b[0], b_gate_b[0], gla_norm[0],
                       q_norm[0], k_norm[0], w_out[0], norm_mix[0], norm_ffn[0], w_router[0], b_router[0])

    n_cond = -(-(1 + bs) // 8) * 8
    conds = jnp.zeros((n_cond, d), F32).at[0].set(c_ctx).at[1:1 + bs].set(c)
    mod = _adaln(conds, w_ada[0], b_ada[0]).reshape(n_cond, 6, d)
    mod = jnp.concatenate([mod, jnp.zeros((n_cond, 2, d), F32)], axis=1)
    mod_p, mod_s = mod[0:1], mod[1:1 + bs]

    zero_state = jnp.zeros((bp, GLA_HEADS, GLA_DK, GLA_DV), F32)
    x1p, h2p, tep, tgp, kc, vc, sf, sb = _mixer(x_prompt, mod_p, True, wp, None, None, None,
                                                zero_state, zero_state)
    n_ctx = cache_k.shape[2]
    x1s, h2s, tes, tgs, _, _, _, _ = _mixer(
        x_sample, mod_s, False, wp, _rope_tables(ss),
        cache_k[:, 0].reshape(bs, n_ctx, W_KV), cache_v[:, 0].reshape(bs, n_ctx, W_KV),
        state_gla_fwd[:, 0].astype(F32), state_gla_bwd[:, 0].astype(F32))

    n_p, n_s = bp * sp, bs * ss
    n_tok = n_p + n_s
    h2 = jnp.concatenate([h2p.reshape(n_p, d), h2s.reshape(n_s, d)], axis=0)
    lane_major = lambda t: jnp.swapaxes(t, 0, 1).reshape(8, -1)
    te = jnp.concatenate([lane_major(tep), lane_major(tes)], axis=1)[:TOP_K].T
    n_rows = -(-(n_tok * TOP_K + N_EXPERTS * (EXPERT_TILE - 1)) // EXPERT_TILE) * EXPERT_TILE
    pos, src, tile_expert, n_valid = _route(te, n_tok, n_rows)
    xs = _dispatch(h2, src, n_rows)
    ys = _experts(xs, tile_expert, n_valid, w_gu[0], b_gu[0], w_down[0], b_down[0])

    def gates_for(tg, b, s):
        tm = min(COMBINE_TILE, s)
        g = lane_major(tg)
        g = jnp.concatenate([g, jnp.zeros((LANES - 8, b * s), F32)], axis=0)
        return jnp.swapaxes(g.reshape(LANES, b * s // tm, tm), 0, 1)

    y_prompt = _combine(ys, pos[:n_p], gates_for(tgp, bp, sp), x1p, mod_p, True, final_norm)
    y_sample = _combine(ys, pos[n_p:], gates_for(tgs, bs, ss), x1s, mod_s, False, final_norm)

    new_cache_k = kc.reshape(bp, 1, sp, KV_HEADS, HEAD_DIM)
    new_cache_v = vc.reshape(bp, 1, sp, KV_HEADS, HEAD_DIM)
    return (y_prompt, y_sample, new_cache_k, new_cache_v, sf[:, None], sb[:, None])
```

```python
import functools

import numpy as np
import jax
import jax.numpy as jnp
from jax import lax
from jax.experimental import pallas as pl
from jax.experimental.pallas import tpu as pltpu

F32 = jnp.float32
BF16 = jnp.bfloat16

D_MODEL = 1024
GRID_W = 64
GLA_HEADS = 4
GLA_DV = 128
GLA_DK = 64
GLA_GATE_RANK = 16
GLA_GATE_NORM = 16.0
HEAD_DIM = 64
ATT_HEADS = 8
KV_HEADS = 2
ROPE_THETA = 10000.0
ROPE_AXIS_PAIRS = HEAD_DIM // 4
N_EXPERTS = 32
TOP_K = 4
EXPERT_FF = 1024
SWIGLU_LIMIT = 7.0
SWIGLU_ALPHA = 1.702
RMS_EPS = 1e-6

W_QK = GLA_HEADS * GLA_DK
W_V = GLA_HEADS * GLA_DV
W_ATT = ATT_HEADS * HEAD_DIM
W_KV = KV_HEADS * HEAD_DIM
LANES = 128
D_IN_PAD = 2 * W_QK + 2 * W_V + W_ATT + 2 * W_KV + LANES

TOKEN_TILE = 256
GLA_BLOCK = 256
GLA_LEVELS = ((256, 32), (32, 8), (8, 1))
ATT_Q_TILE = 256
EXPERT_TILE = 256
COMBINE_TILE = 128
VMEM_LIMIT = 56 * 1024 * 1024


def _split_bf16(x):
    hi = x.astype(BF16)
    lo = (x - hi.astype(F32)).astype(BF16)
    return hi, lo


def _dot(a, b):
    return jnp.dot(a, b, preferred_element_type=F32)


def _dot_nt(a, b):
    return lax.dot_general(a, b, (((1,), (1,)), ((), ())), preferred_element_type=F32)


def _dot_tn(a, b):
    return lax.dot_general(a, b, (((0,), (0,)), ((), ())), preferred_element_type=F32)


def _adaln_kernel(c_ref, w_ref, b_ref, o_ref):
    c = c_ref[...]
    a = c * jax.nn.sigmoid(c)
    ah, al = _split_bf16(a)
    wh, wl = _split_bf16(w_ref[...])
    o_ref[...] = _dot(ah, wh) + _dot(al, wh) + _dot(ah, wl) + b_ref[...]


def _adaln(conds, w_ada, b_ada):
    m, d = conds.shape
    n = w_ada.shape[1]
    tn = 512
    return pl.pallas_call(
        _adaln_kernel,
        grid=(n // tn,),
        in_specs=[pl.BlockSpec((m, d), lambda j: (0, 0)),
                  pl.BlockSpec((d, tn), lambda j: (0, j)),
                  pl.BlockSpec((1, tn), lambda j: (0, j))],
        out_specs=pl.BlockSpec((m, tn), lambda j: (0, j)),
        out_shape=jax.ShapeDtypeStruct((m, n), F32),
        name="adaln",
    )(conds, w_ada, b_ada.reshape(1, n))


def _rope_rotate(x, c, s):
    n = x.shape[-1]
    lane = lax.broadcasted_iota(jnp.int32, x.shape, 1)
    partner = jnp.where((lane & 1) == 0, pltpu.roll(x, n - 1, 1), pltpu.roll(x, 1, 1))
    return x * c + partner * s


def _premix_kernel(*refs, use_rope):
    (x_ref, mod_ref, nmix_ref, win_ref, wgh_ref, wgl_ref, bg_ref, bq_ref, bk_ref,
     qn_ref, kn_ref) = refs[:11]
    n_in = 11
    if use_rope:
        cq_ref, sq_ref, ck_ref, sk_ref = refs[11:15]
        n_in = 15
    (qa_o, ka_o, va_o, ga_o, laf_o, lab_o, q_o, k_o, v_o, kc_o, vc_o) = refs[n_in:]

    x = x_ref[0]
    mod = mod_ref[0]
    sh1 = mod[0:1]
    sc1 = mod[1:2]
    ms = jnp.mean(x * x, axis=-1, keepdims=True)
    h = x * lax.rsqrt(ms + RMS_EPS) * nmix_ref[...]
    h = h * (1.0 + sc1) + sh1
    proj = _dot(h.astype(BF16), win_ref[...])

    o = 0
    qa_o[0] = proj[:, o:o + W_QK] * (GLA_DK ** -0.5); o += W_QK
    ka_o[0] = proj[:, o:o + W_QK]; o += W_QK
    va_o[0] = proj[:, o:o + W_V]; o += W_V
    ga_o[0] = proj[:, o:o + W_V]; o += W_V
    qb = proj[:, o:o + W_ATT]; o += W_ATT
    kb = proj[:, o:o + W_KV]; o += W_KV
    vb = proj[:, o:o + W_KV]; o += W_KV
    lr = proj[:, o:o + LANES]

    lh, ll = _split_bf16(lr)
    xg = _dot(lh, wgh_ref[...]) + _dot(ll, wgh_ref[...]) + _dot(lh, wgl_ref[...]) + bg_ref[...]
    la = (jnp.minimum(xg, 0.0) - jnp.log1p(jnp.exp(-jnp.abs(xg)))) * (1.0 / GLA_GATE_NORM)
    laf_o[0] = la[:, :W_QK]
    lab_o[0] = la[:, W_QK:]

    def head_norm(t, ones_ref, g_ref):
        sh, sl = _split_bf16(t * t)
        msq = (_dot(sh, ones_ref[...]) + _dot(sl, ones_ref[...])) * (1.0 / HEAD_DIM)
        return t * lax.rsqrt(msq + RMS_EPS) * g_ref[...]

    qh = head_norm(qb, bq_ref, qn_ref)
    kh = head_norm(kb, bk_ref, kn_ref)
    kc_o[0] = kh
    vc_o[0] = vb
    if use_rope:
        qh = _rope_rotate(qh, cq_ref[...], sq_ref[...])
        kr = _rope_rotate(kh, ck_ref[...], sk_ref[...])
    else:
        kr = kh
    q_o[0] = (qh * (HEAD_DIM ** -0.5)).astype(BF16)
    k_o[0] = kr.astype(BF16)
    v_o[0] = vb.astype(BF16)


def _premix(x, mod, mod_shared, wp, rope):
    b, s, d = x.shape
    tm = min(TOKEN_TILE, s)
    use_rope = rope is not None
    const = lambda shape: pl.BlockSpec(shape, lambda i, j: (0,) * len(shape))
    tok = lambda w: pl.BlockSpec((1, tm, w), lambda i, j: (i, j, 0))
    mod_map = (lambda i, j: (0, 0, 0)) if mod_shared else (lambda i, j: (i, 0, 0))
    in_specs = [tok(d), pl.BlockSpec((1, 8, d), mod_map), const((1, d)),
                const((d, D_IN_PAD)), const((LANES, 2 * W_QK)), const((LANES, 2 * W_QK)),
                const((1, 2 * W_QK)), const((W_ATT, W_ATT)), const((W_KV, W_KV)),
                const((1, W_ATT)), const((1, W_KV))]
    args = [x, mod, wp["norm_mix"], wp["w_in"], wp["wg_hi"], wp["wg_lo"], wp["b_gate"],
            wp["ones_q"], wp["ones_k"], wp["q_norm"], wp["k_norm"]]
    if use_rope:
        seq = lambda w: pl.BlockSpec((tm, w), lambda i, j: (j, 0))
        in_specs += [seq(W_ATT), seq(W_ATT), seq(W_KV), seq(W_KV)]
        args += list(rope)
    widths = [(W_QK, F32), (W_QK, F32), (W_V, F32), (W_V, F32), (W_QK, F32), (W_QK, F32),
              (W_ATT, BF16), (W_KV, BF16), (W_KV, BF16), (W_KV, F32), (W_KV, F32)]
    return pl.pallas_call(
        functools.partial(_premix_kernel, use_rope=use_rope),
        grid=(b, s // tm),
        in_specs=in_specs,
        out_specs=[tok(w) for w, _ in widths],
        out_shape=[jax.ShapeDtypeStruct((b, s, w), dt) for w, dt in widths],
        compiler_params=pltpu.CompilerParams(
            dimension_semantics=("parallel", "parallel"), vmem_limit_bytes=VMEM_LIMIT),
        name="premix_rope" if use_rope else "premix",
    )(*args)


def _bcast_rows(x, n_par, p_rows, row):
    w = x.shape[-1]
    r = x.reshape(n_par, p_rows, w)[:, row:row + 1, :]
    return jnp.broadcast_to(r, (n_par, p_rows, w)).reshape(n_par * p_rows, w)


def _gla_kernel(q_ref, k_ref, v_ref, la_ref, s0_ref, o_ref, sf_ref, st_scr, *, rows, levels, reverse):
    blk = pl.program_id(1)

    @pl.when(blk == 0)
    def _():
        st_scr[...] = s0_ref[0]

    q = q_ref[0]
    k = k_ref[0]
    la = la_ref[0]
    vb = v_ref[0].astype(BF16)

    ri = lax.broadcasted_iota(jnp.int32, (rows, rows), 0)
    ci = lax.broadcasted_iota(jnp.int32, (rows, rows), 1)
    tri = jnp.where((ri <= ci) if reverse else (ri >= ci), 1.0, 0.0).astype(BF16)
    hi = la.astype(BF16)
    r1 = la - hi.astype(F32)
    mid = r1.astype(BF16)
    lo = (r1 - mid.astype(F32)).astype(BF16)
    cum = _dot(tri, hi) + _dot(tri, mid) + _dot(tri, lo)

    ridx = lax.broadcasted_iota(jnp.int32, (rows, W_QK), 0)
    s_acc = [None] * GLA_HEADS
    for (par, sub) in levels:
        groups = par // sub
        n_par = rows // par
        pi = (ridx % par) // sub
        edge = sub - 1 if reverse else 0
        if sub > 1:
            own = _bcast_rows(cum, rows // sub, sub, edge)
            qt = q * jnp.exp(cum - own)
        else:
            qt = q
        qps, kps = [], []
        for p in range(groups):
            if sub > 1 and p == (groups - 1 if reverse else 0):
                continue
            cp = _bcast_rows(cum, n_par, par, p * sub + edge)
            if reverse:
                km = (pi > p) if sub > 1 else (pi >= p)
            else:
                km = (pi < p) if sub > 1 else (pi <= p)
            ek = jnp.where(km, cp - cum, 0.0)
            kps.append(jnp.where(km, k * jnp.exp(ek), 0.0).astype(BF16))
            qps.append(jnp.where(pi == p, qt, 0.0).astype(BF16))
        ng = len(qps)
        lane_c = lax.broadcasted_iota(jnp.int32, (rows, LANES * ng), 1)
        same_parent = (ri // par) == (ci // par)
        for hp in range(GLA_HEADS // 2):
            qc = jnp.concatenate([a[:, LANES * hp:LANES * (hp + 1)] for a in qps], axis=1)
            kc = jnp.concatenate([a[:, LANES * hp:LANES * (hp + 1)] for a in kps], axis=1)
            for hh in range(2):
                qh = jnp.where(((lane_c % LANES) // GLA_DK) == hh, qc, jnp.zeros_like(qc))
                sl = _dot_nt(qh, kc)
                if par < rows:
                    sl = jnp.where(same_parent, sl, 0.0)
                h = 2 * hp + hh
                s_acc[h] = sl if s_acc[h] is None else s_acc[h] + sl

    q0 = (q * jnp.exp(cum)).astype(BF16)
    far = 0 if reverse else rows - 1
    last = cum[far:far + 1, :]
    kd = (k * jnp.exp(last - cum)).astype(BF16)
    br = lax.broadcasted_iota(jnp.int32, (2 * GLA_DV, 2 * GLA_DK), 0)
    bc = lax.broadcasted_iota(jnp.int32, (2 * GLA_DV, 2 * GLA_DK), 1)
    blockdiag = (br // GLA_DV) == (bc // GLA_DK)
    for hp in range(GLA_HEADS // 2):
        st = st_scr[hp]
        o_inter = _dot_nt(q0[:, LANES * hp:LANES * (hp + 1)], st.astype(BF16))
        o_intra = jnp.concatenate(
            [_dot(s_acc[2 * hp + hh].astype(BF16),
                  vb[:, GLA_DV * (2 * hp + hh):GLA_DV * (2 * hp + hh + 1)]) for hh in range(2)],
            axis=1)
        o_ref[0, :, 2 * GLA_DV * hp:2 * GLA_DV * (hp + 1)] = o_inter + o_intra
        upd = _dot_tn(vb[:, 2 * GLA_DV * hp:2 * GLA_DV * (hp + 1)],
                      kd[:, LANES * hp:LANES * (hp + 1)])
        st_scr[hp] = (jnp.exp(last[:, LANES * hp:LANES * (hp + 1)]) * st
                      + jnp.where(blockdiag, upd, 0.0))

    @pl.when(blk == pl.num_programs(1) - 1)
    def _():
        sf_ref[0] = st_scr[...]


def _gla(q, k, v, la, s0t, reverse):
    n, l, _ = q.shape
    rows = min(GLA_BLOCK, l)
    nb = l // rows
    levels = tuple((min(p, rows), s) for p, s in GLA_LEVELS)
    order = (lambda j: nb - 1 - j) if reverse else (lambda j: j)
    tok = lambda w: pl.BlockSpec((1, rows, w), lambda i, j: (i, order(j), 0))
    st_spec = pl.BlockSpec((1, 2, 2 * GLA_DV, 2 * GLA_DK), lambda i, j: (i, 0, 0, 0))
    return pl.pallas_call(
        functools.partial(_gla_kernel, rows=rows, levels=levels, reverse=reverse),
        grid=(n, nb),
        in_specs=[tok(W_QK), tok(W_QK), tok(W_V), tok(W_QK), st_spec],
        out_specs=[tok(W_V), st_spec],
        out_shape=[jax.ShapeDtypeStruct((n, l, W_V), F32),
                   jax.ShapeDtypeStruct(s0t.shape, F32)],
        scratch_shapes=[pltpu.VMEM((2, 2 * GLA_DV, 2 * GLA_DK), F32)],
        compiler_params=pltpu.CompilerParams(
            dimension_semantics=("parallel", "arbitrary"), vmem_limit_bytes=VMEM_LIMIT),
        name="gla_bwd" if reverse else "gla_fwd",
    )(q, k, v, la, s0t)


def _state_to_blockdiag_t(s):
    n = s.shape[0]
    st = jnp.swapaxes(s, -1, -2).reshape(n, 2, 2, GLA_DV, GLA_DK)
    z = jnp.zeros_like(st[:, :, 0])
    top = jnp.concatenate([st[:, :, 0], z], axis=-1)
    bot = jnp.concatenate([z, st[:, :, 1]], axis=-1)
    return jnp.concatenate([top, bot], axis=-2)


def _blockdiag_t_to_state(sb):
    n = sb.shape[0]
    h0 = sb[:, :, :GLA_DV, :GLA_DK]
    h1 = sb[:, :, GLA_DV:, GLA_DK:]
    st = jnp.stack([h0, h1], axis=2).reshape(n, GLA_HEADS, GLA_DV, GLA_DK)
    return jnp.swapaxes(st, -1, -2)


def _attn_kernel(*refs, s_self, n_ctx):
    if n_ctx:
        q_ref, k_ref, v_ref, ck_ref, cv_ref, o_ref, kg_scr, v_scr = refs
    else:
        q_ref, k_ref, v_ref, o_ref, kg_scr, v_scr = refs

    @pl.when(pl.program_id(1) == 0)
    def _():
        def put(kk, vv, start, n):
            lane = lax.broadcasted_iota(jnp.int32, kk.shape, 1)
            kg_scr[0, start:start + n, :] = jnp.where(lane < HEAD_DIM, kk, jnp.zeros_like(kk))
            kg_scr[1, start:start + n, :] = jnp.where(lane >= HEAD_DIM, kk, jnp.zeros_like(kk))
            v_scr[start:start + n, :] = vv
        put(k_ref[0], v_ref[0], 0, s_self)
        if n_ctx:
            put(ck_ref[0].astype(BF16), cv_ref[0].astype(BF16), s_self, n_ctx)

    q = q_ref[0]
    vv = v_scr[...]
    for m in range(ATT_HEADS // KV_HEADS):
        qm = q[:, LANES * m:LANES * (m + 1)]
        og = []
        for g in range(KV_HEADS):
            s = _dot_nt(qm, kg_scr[g])
            mx = jnp.max(s, axis=-1, keepdims=True)
            p = jnp.exp(s - mx)
            l = jnp.sum(p, axis=-1, keepdims=True)
            og.append(_dot(p.astype(BF16), vv) / l)
        lane = lax.broadcasted_iota(jnp.int32, og[0].shape, 1)
        o_ref[0, :, LANES * m:LANES * (m + 1)] = jnp.where(lane < HEAD_DIM, og[0], og[1]).astype(BF16)


def _attention(q, k, v, cache_k=None, cache_v=None):
    b, s, _ = q.shape
    n_ctx = 0 if cache_k is None else cache_k.shape[1]
    tq = min(ATT_Q_TILE, s)
    sk = s + n_ctx
    full = lambda n, w: pl.BlockSpec((1, n, w), lambda i, j: (i, 0, 0))
    in_specs = [pl.BlockSpec((1, tq, W_ATT), lambda i, j: (i, j, 0)), full(s, W_KV), full(s, W_KV)]
    args = [q, k, v]
    if n_ctx:
        in_specs += [full(n_ctx, W_KV), full(n_ctx, W_KV)]
        args += [cache_k, cache_v]
    return pl.pallas_call(
        functools.partial(_attn_kernel, s_self=s, n_ctx=n_ctx),
        grid=(b, s // tq),
        in_specs=in_specs,
        out_specs=pl.BlockSpec((1, tq, W_ATT), lambda i, j: (i, j, 0)),
        out_shape=jax.ShapeDtypeStruct((b, s, W_ATT), BF16),
        scratch_shapes=[pltpu.VMEM((KV_HEADS, sk, W_KV), BF16), pltpu.VMEM((sk, W_KV), BF16)],
        compiler_params=pltpu.CompilerParams(
            dimension_semantics=("parallel", "arbitrary"), vmem_limit_bytes=VMEM_LIMIT),
        name="attention_ctx" if n_ctx else "attention",
    )(*args)


def _postmix_kernel(of_ref, or_ref, ga_ref, oa_ref, x_ref, mod_ref, gn_ref, wout_ref, nffn_ref,
                    wrh_ref, wrl_ref, br_ref, x1_o, h2_o, te_o, tg_o):
    mod = mod_ref[0]
    g1, sh2, sc2 = mod[2:3], mod[3:4], mod[4:5]
    og = of_ref[0] + or_ref[0]
    ga = ga_ref[0]
    parts = []
    for h in range(GLA_HEADS):
        blk = og[:, GLA_DV * h:GLA_DV * (h + 1)]
        ms = jnp.mean(blk * blk, axis=-1, keepdims=True)
        gh = ga[:, GLA_DV * h:GLA_DV * (h + 1)]
        parts.append((blk * lax.rsqrt(ms + RMS_EPS) * gn_ref[...] * (gh * jax.nn.sigmoid(gh))).astype(BF16))
    mix = jnp.concatenate(parts + [oa_ref[0]], axis=1)
    mo = _dot(mix, wout_ref[...])
    x1 = x_ref[0] + g1 * mo
    x1_o[0] = x1
    ms = jnp.mean(x1 * x1, axis=-1, keepdims=True)
    h2 = x1 * lax.rsqrt(ms + RMS_EPS) * nffn_ref[...]
    h2 = h2 * (1.0 + sc2) + sh2
    h2_o[0] = h2

    hh, hl = _split_bf16(h2)
    lt = (_dot_nt(wrh_ref[...], hh) + _dot_nt(wrh_ref[...], hl) + _dot_nt(wrl_ref[...], hh)
          + br_ref[...])
    eidx = lax.broadcasted_iota(jnp.int32, lt.shape, 0)
    vals, ids = [], []
    for _ in range(TOP_K):
        mx = jnp.max(lt, axis=0, keepdims=True)
        idx = jnp.min(jnp.where(lt == mx, eidx, N_EXPERTS), axis=0, keepdims=True)
        lt = jnp.where(eidx == idx, -jnp.inf, lt)
        vals.append(mx)
        ids.append(idx)
    ws = [jnp.exp(vv - vals[0]) for vv in vals]
    tot = ws[0] + ws[1] + ws[2] + ws[3]
    tm = lt.shape[1]
    te_o[0] = jnp.concatenate(ids + [jnp.zeros((8 - TOP_K, tm), jnp.int32)], axis=0)
    tg_o[0] = jnp.concatenate([w / tot for w in ws] + [jnp.zeros((8 - TOP_K, tm), F32)], axis=0)


def _postmix(o_f, o_r, ga, o_att, x, mod, mod_shared, wp):
    b, s, d = x.shape
    tm = min(TOKEN_TILE, s)
    nt = s // tm
    const = lambda shape: pl.BlockSpec(shape, lambda i, j: (0,) * len(shape))
    tok = lambda w: pl.BlockSpec((1, tm, w), lambda i, j: (i, j, 0))
    mod_map = (lambda i, j: (0, 0, 0)) if mod_shared else (lambda i, j: (i, 0, 0))
    lane_tok = pl.BlockSpec((1, 8, tm), lambda i, j: (i * nt + j, 0, 0))
    return pl.pallas_call(
        _postmix_kernel,
        grid=(b, nt),
        in_specs=[tok(W_V), tok(W_V), tok(W_V), tok(W_ATT), tok(d), pl.BlockSpec((1, 8, d), mod_map),
                  const((1, GLA_DV)), const((d, d)), const((1, d)),
                  const((N_EXPERTS, d)), const((N_EXPERTS, d)), const((N_EXPERTS, 1))],
        out_specs=[tok(d), tok(d), lane_tok, lane_tok],
        out_shape=[jax.ShapeDtypeStruct((b, s, d), F32), jax.ShapeDtypeStruct((b, s, d), F32),
                   jax.ShapeDtypeStruct((b * nt, 8, tm), jnp.int32),
                   jax.ShapeDtypeStruct((b * nt, 8, tm), F32)],
        compiler_params=pltpu.CompilerParams(
            dimension_semantics=("parallel", "parallel"), vmem_limit_bytes=VMEM_LIMIT),
        name="postmix",
    )(o_f, o_r, ga, o_att, x, mod, wp["gla_norm"], wp["w_out"], wp["norm_ffn"],
      wp["wr_hi"], wp["wr_lo"], wp["b_router"])


def _row_gather(idx_ref, n_rows, src_hbm, dst, sem, idx_of):
    def body(i, carry):
        pltpu.make_async_copy(src_hbm.at[pl.ds(idx_of(idx_ref, i), 1)], dst(i), sem).start()
        return carry
    lax.fori_loop(0, n_rows, body, 0)


def _dispatch_kernel(src_ref, h_hbm, xs_ref, sem):
    tg = xs_ref.shape[0]

    def start(i, carry):
        pltpu.make_async_copy(h_hbm.at[pl.ds(src_ref[0, 0, i], 1)], xs_ref.at[pl.ds(i, 1)], sem).start()
        return carry
    lax.fori_loop(0, tg, start, 0)
    pltpu.make_async_copy(h_hbm.at[pl.ds(0, tg)], xs_ref, sem).wait()


def _dispatch(h2, src_rows, n_rows):
    t, d = h2.shape
    tg = EXPERT_TILE
    nt = n_rows // tg
    return pl.pallas_call(
        _dispatch_kernel,
        grid=(nt,),
        in_specs=[pl.BlockSpec((1, 1, tg), lambda i: (i, 0, 0), memory_space=pltpu.SMEM),
                  pl.BlockSpec(memory_space=pl.ANY)],
        out_specs=pl.BlockSpec((tg, d), lambda i: (i, 0)),
        out_shape=jax.ShapeDtypeStruct((n_rows, d), F32),
        scratch_shapes=[pltpu.SemaphoreType.DMA(())],
        compiler_params=pltpu.CompilerParams(dimension_semantics=("arbitrary",)),
        name="dispatch",
    )(src_rows.reshape(nt, 1, tg), h2)


def _experts_kernel(te_ref, nv_ref, xs_ref, wgu_ref, bgu_ref, wd_ref, bd_ref, o_ref, wgu_bf, wd_bf):
    i = pl.program_id(0)
    valid = i < nv_ref[0]
    new_expert = jnp.logical_or(i == 0, te_ref[i] != te_ref[jnp.maximum(i - 1, 0)])

    @pl.when(jnp.logical_and(valid, new_expert))
    def _():
        wgu_bf[...] = wgu_ref[0].astype(BF16)
        wd_bf[...] = wd_ref[0].astype(BF16)

    @pl.when(valid)
    def _():
        x = xs_ref[...].astype(BF16)
        gu = _dot(x, wgu_bf[...]) + bgu_ref[0]
        g = jnp.minimum(gu[:, :EXPERT_FF], SWIGLU_LIMIT)
        u = jnp.clip(gu[:, EXPERT_FF:], -SWIGLU_LIMIT, SWIGLU_LIMIT)
        act = (u + 1.0) * (g * jax.nn.sigmoid(SWIGLU_ALPHA * g))
        o_ref[...] = _dot(act.astype(BF16), wd_bf[...]) + bd_ref[0]

    @pl.when(jnp.logical_not(valid))
    def _():
        o_ref[...] = jnp.zeros_like(o_ref)


def _experts(xs, tile_expert, n_valid, w_gu, b_gu, w_down, b_down):
    n_rows, d = xs.shape
    tm = EXPERT_TILE
    nt = n_rows // tm
    ne, _, ff2 = w_gu.shape
    ff = ff2 // 2
    grid_spec = pltpu.PrefetchScalarGridSpec(
        num_scalar_prefetch=2,
        grid=(nt,),
        in_specs=[pl.BlockSpec((tm, d), lambda i, te, nv: (i, 0)),
                  pl.BlockSpec((1, d, ff2), lambda i, te, nv: (te[i], 0, 0)),
                  pl.BlockSpec((1, 1, ff2), lambda i, te, nv: (te[i], 0, 0)),
                  pl.BlockSpec((1, ff, d), lambda i, te, nv: (te[i], 0, 0)),
                  pl.BlockSpec((1, 1, d), lambda i, te, nv: (te[i], 0, 0))],
        out_specs=pl.BlockSpec((tm, d), lambda i, te, nv: (i, 0)),
        scratch_shapes=[pltpu.VMEM((d, ff2), BF16), pltpu.VMEM((ff, d), BF16)],
    )
    return pl.pallas_call(
        _experts_kernel,
        grid_spec=grid_spec,
        out_shape=jax.ShapeDtypeStruct((n_rows, d), F32),
        compiler_params=pltpu.CompilerParams(
            dimension_semantics=("arbitrary",), vmem_limit_bytes=VMEM_LIMIT),
        name="experts",
    )(tile_expert, n_valid, xs, w_gu, b_gu.reshape(ne, 1, ff2), w_down, b_down.reshape(ne, 1, d))


def _combine_kernel(pos_ref, ys_hbm, tg_ref, x1_ref, mod_ref, fn_ref, y_ref, buf, sem):
    tm = x1_ref.shape[1]

    def start(t, carry):
        for r in range(TOP_K):
            pltpu.make_async_copy(ys_hbm.at[pl.ds(pos_ref[0, 0, TOP_K * t + r], 1)],
                                  buf.at[r, pl.ds(t, 1)], sem).start()
        return carry
    lax.fori_loop(0, tm, start, 0)
    for r in range(TOP_K):
        pltpu.make_async_copy(ys_hbm.at[pl.ds(0, tm)], buf.at[r], sem).wait()

    gates = jnp.transpose(tg_ref[0])
    y = gates[:, 0:1] * buf[0]
    for r in range(1, TOP_K):
        y = y + gates[:, r:r + 1] * buf[r]
    g2 = mod_ref[0][5:6]
    x2 = x1_ref[0] + g2 * y
    ms = jnp.mean(x2 * x2, axis=-1, keepdims=True)
    y_ref[0] = x2 * lax.rsqrt(ms + RMS_EPS) * fn_ref[...]


def _combine(ys, pos, gates_t, x1, mod, mod_shared, final_norm):
    b, s, d = x1.shape
    tm = min(COMBINE_TILE, s)
    nt = s // tm
    mod_map = (lambda i, j: (0, 0, 0)) if mod_shared else (lambda i, j: (i, 0, 0))
    return pl.pallas_call(
        _combine_kernel,
        grid=(b, nt),
        in_specs=[pl.BlockSpec((1, 1, TOP_K * tm), lambda i, j: (i * nt + j, 0, 0), memory_space=pltpu.SMEM),
                  pl.BlockSpec(memory_space=pl.ANY),
                  pl.BlockSpec((1, LANES, tm), lambda i, j: (i * nt + j, 0, 0)),
                  pl.BlockSpec((1, tm, d), lambda i, j: (i, j, 0)),
                  pl.BlockSpec((1, 8, d), mod_map),
                  pl.BlockSpec((1, d), lambda i, j: (0, 0))],
        out_specs=pl.BlockSpec((1, tm, d), lambda i, j: (i, j, 0)),
        out_shape=jax.ShapeDtypeStruct((b, s, d), F32),
        scratch_shapes=[pltpu.VMEM((TOP_K, tm, d), F32), pltpu.SemaphoreType.DMA(())],
        compiler_params=pltpu.CompilerParams(
            dimension_semantics=("arbitrary", "arbitrary"), vmem_limit_bytes=VMEM_LIMIT),
        name="combine",
    )(pos.reshape(b * nt, 1, TOP_K * tm), ys, gates_t, x1, mod, final_norm.reshape(1, d))


def _route(top_e, n_tok, n_rows):
    tm = EXPERT_TILE
    onehot = jnp.sum((top_e[:, :, None] == jnp.arange(N_EXPERTS)[None, None, :]).astype(jnp.int32), axis=1)
    excl = jnp.cumsum(onehot, axis=0) - onehot
    counts = jnp.sum(onehot, axis=0)
    tiles = (counts + tm - 1) // tm
    tile_end = jnp.cumsum(tiles)
    row_off = (tile_end - tiles) * tm
    rank = jnp.take_along_axis(excl, top_e, axis=1)
    pos = row_off[top_e] + rank
    src = jnp.zeros((n_rows,), jnp.int32).at[pos.reshape(-1)].set(
        jnp.repeat(jnp.arange(n_tok, dtype=jnp.int32), TOP_K))
    n_valid = tile_end[-1].astype(jnp.int32).reshape(1)
    tile_ids = jnp.minimum(jnp.arange(n_rows // tm), n_valid[0] - 1)
    tile_expert = jnp.searchsorted(tile_end, tile_ids, side="right").astype(jnp.int32)
    return pos.astype(jnp.int32), src, tile_expert, n_valid


def _rope_tables(n_tok):
    rows = n_tok // GRID_W
    r, col = jnp.meshgrid(jnp.arange(rows), jnp.arange(GRID_W), indexing="ij")
    r = r.reshape(-1).astype(F32)
    col = col.reshape(-1).astype(F32)
    inv = 1.0 / (ROPE_THETA ** (jnp.arange(ROPE_AXIS_PAIRS, dtype=F32) / ROPE_AXIS_PAIRS))
    ang = jnp.concatenate([r[:, None] * inv, col[:, None] * inv], axis=-1)
    c64 = jnp.repeat(jnp.cos(ang), 2, axis=-1)
    sign = jnp.tile(jnp.array([-1.0, 1.0], F32), HEAD_DIM // 2)
    s64 = jnp.repeat(jnp.sin(ang), 2, axis=-1) * sign
    return (jnp.tile(c64, (1, ATT_HEADS)), jnp.tile(s64, (1, ATT_HEADS)),
            jnp.tile(c64, (1, KV_HEADS)), jnp.tile(s64, (1, KV_HEADS)))


def _prep_weights(w_in, w_gate_f, b_gate_f, w_gate_b, b_gate_b, gla_norm, q_norm, k_norm, w_out,
                  norm_mix, norm_ffn, w_router, b_router):
    d = w_in.shape[0]
    o_lr = 2 * W_QK + 2 * W_V
    o_q = o_lr + 2 * GLA_GATE_RANK
    o_k = o_q + W_ATT
    head_order = np.array([0, 4, 1, 5, 2, 6, 3, 7])
    perm = (head_order[:, None] * HEAD_DIM + np.arange(HEAD_DIM)[None, :]).reshape(-1)
    w_q = w_in[:, o_q:o_k][:, perm]
    lr_pad = jnp.zeros((d, LANES - 2 * GLA_GATE_RANK), w_in.dtype)
    w_in_p = jnp.concatenate([w_in[:, :o_lr], w_q, w_in[:, o_k:], w_in[:, o_lr:o_q], lr_pad], axis=1)
    wg = jnp.zeros((LANES, 2 * W_QK), F32)
    wg = wg.at[:GLA_GATE_RANK, :W_QK].set(w_gate_f)
    wg = wg.at[GLA_GATE_RANK:2 * GLA_GATE_RANK, W_QK:].set(w_gate_b)
    wg_hi, wg_lo = _split_bf16(wg)
    ones_q = jnp.asarray(np.kron(np.eye(ATT_HEADS), np.ones((HEAD_DIM, HEAD_DIM))), BF16)
    ones_k = jnp.asarray(np.kron(np.eye(KV_HEADS), np.ones((HEAD_DIM, HEAD_DIM))), BF16)
    w_out_p = jnp.concatenate([w_out[:W_V], w_out[W_V:][perm]], axis=0).astype(BF16)
    wr_hi, wr_lo = _split_bf16(w_router.T)
    return {
        "norm_mix": norm_mix.reshape(1, d), "w_in": w_in_p.astype(BF16),
        "wg_hi": wg_hi, "wg_lo": wg_lo,
        "b_gate": jnp.concatenate([b_gate_f, b_gate_b]).reshape(1, 2 * W_QK),
        "ones_q": ones_q, "ones_k": ones_k,
        "q_norm": jnp.tile(q_norm, ATT_HEADS).reshape(1, W_ATT),
        "k_norm": jnp.tile(k_norm, KV_HEADS).reshape(1, W_KV),
        "gla_norm": gla_norm.reshape(1, GLA_DV), "w_out": w_out_p,
        "norm_ffn": norm_ffn.reshape(1, d), "wr_hi": wr_hi, "wr_lo": wr_lo,
        "b_router": b_router.reshape(N_EXPERTS, 1),
    }


def _mixer(x, mod, mod_shared, wp, rope, cache_k, cache_v, s0_f, s0_b):
    b, s, _ = x.shape
    qa, ka, va, ga, la_f, la_b, q, k, v, kc, vc = _premix(x, mod, mod_shared, wp, rope)
    o_f, sf = _gla(qa, ka, va, la_f, _state_to_blockdiag_t(s0_f), False)
    o_r, sb = _gla(qa, ka, va, la_b, _state_to_blockdiag_t(s0_b), True)
    o_att = _attention(q, k, v, cache_k, cache_v)
    x1, h2, te, tg = _postmix(o_f, o_r, ga, o_att, x, mod, mod_shared, wp)
    return x1, h2, te, tg, kc, vc, _blockdiag_t_to_state(sf), _blockdiag_t_to_state(sb)


def kernel(x_prompt, x_sample, c, cache_k, cache_v, state_gla_fwd, state_gla_bwd, c_ctx, w_ada, b_ada,
           norm_mix, w_in, w_gate_f, b_gate_f, w_gate_b, b_gate_b, gla_norm, q_norm, k_norm, w_out,
           norm_ffn, w_router, b_router, w_gu, b_gu, w_down, b_down, final_norm):
    bp, sp, d = x_prompt.shape
    bs, ss, _ = x_sample.shape
    assert w_ada.shape[0] == 1, "single-layer trunk"
    wp = _prep_weights(w_in[0], w_gate_f[0], b_gate_f[0], w_gate_b[0], b_gate_b[0], gla_norm[0],
                       q_norm[0], k_norm[0], w_out[0], norm_mix[0], norm_ffn[0], w_router[0], b_router[0])

    n_cond = -(-(1 + bs) // 8) * 8
    conds = jnp.zeros((n_cond, d), F32).at[0].set(c_ctx).at[1:1 + bs].set(c)
    mod = _adaln(conds, w_ada[0], b_ada[0]).reshape(n_cond, 6, d)
    mod = jnp.concatenate([mod, jnp.zeros((n_cond, 2, d), F32)], axis=1)
    mod_p, mod_s = mod[0:1], mod[1:1 + bs]

    zero_state = jnp.zeros((bp, GLA_HEADS, GLA_DK, GLA_DV), F32)
    x1p, h2p, tep, tgp, kc, vc, sf, sb = _mixer(x_prompt, mod_p, True, wp, None, None, None,
                                                zero_state, zero_state)
    n_ctx = cache_k.shape[2]
    x1s, h2s, tes, tgs, _, _, _, _ = _mixer(
        x_sample, mod_s, False, wp, _rope_tables(ss),
        cache_k[:, 0].reshape(bs, n_ctx, W_KV), cache_v[:, 0].reshape(bs, n_ctx, W_KV),
        state_gla_fwd[:, 0].astype(F32), state_gla_bwd[:, 0].astype(F32))

    n_p, n_s = bp * sp, bs * ss
    n_tok = n_p + n_s
    h2 = jnp.concatenate([h2p.reshape(n_p, d), h2s.reshape(n_s, d)], axis=0)
    lane_major = lambda t: jnp.swapaxes(t, 0, 1).reshape(8, -1)
    te = jnp.concatenate([lane_major(tep), lane_major(tes)], axis=1)[:TOP_K].T
    n_rows = -(-(n_tok * TOP_K + N_EXPERTS * (EXPERT_TILE - 1)) // EXPERT_TILE) * EXPERT_TILE
    pos, src, tile_expert, n_valid = _route(te, n_tok, n_rows)
    xs = _dispatch(h2, src, n_rows)
    ys = _experts(xs, tile_expert, n_valid, w_gu[0], b_gu[0], w_down[0], b_down[0])

    def gates_for(tg, b, s):
        tm = min(COMBINE_TILE, s)
        g = lane_major(tg)
        g = jnp.concatenate([g, jnp.zeros((LANES - 8, b * s), F32)], axis=0)
        return jnp.swapaxes(g.reshape(LANES, b * s // tm, tm), 0, 1)

    y_prompt = _combine(ys, pos[:n_p], gates_for(tgp, bp, sp), x1p, mod_p, True, final_norm)
    y_sample = _combine(ys, pos[n_p:], gates_for(tgs, bs, ss), x1s, mod_s, False, final_norm)

    new_cache_k = kc.reshape(bp, 1, sp, KV_HEADS, HEAD_DIM)
    new_cache_v = vc.reshape(bp, 1, sp, KV_HEADS, HEAD_DIM)
    return (y_prompt, y_sample, new_cache_k, new_cache_v, sf[:, None], sb[:, None])
```

```python
import functools

import numpy as np
import jax
import jax.numpy as jnp
from jax import lax
from jax.experimental import pallas as pl
from jax.experimental.pallas import tpu as pltpu

F32 = jnp.float32
BF16 = jnp.bfloat16

D_MODEL = 1024
GRID_W = 64
GLA_HEADS = 4
GLA_DV = 128
GLA_DK = 64
GLA_GATE_RANK = 16
GLA_GATE_NORM = 16.0
HEAD_DIM = 64
ATT_HEADS = 8
KV_HEADS = 2
ROPE_THETA = 10000.0
ROPE_AXIS_PAIRS = HEAD_DIM // 4
N_EXPERTS = 32
TOP_K = 4
EXPERT_FF = 1024
SWIGLU_LIMIT = 7.0
SWIGLU_ALPHA = 1.702
RMS_EPS = 1e-6

W_QK = GLA_HEADS * GLA_DK
W_V = GLA_HEADS * GLA_DV
W_ATT = ATT_HEADS * HEAD_DIM
W_KV = KV_HEADS * HEAD_DIM
LANES = 128
D_IN_PAD = 2 * W_QK + 2 * W_V + W_ATT + 2 * W_KV + LANES

TOKEN_TILE = 256
GLA_BLOCK = 256
GLA_LEVELS = ((256, 32), (32, 8), (8, 1))
ATT_Q_TILE = 256
EXPERT_TILE = 256
VMEM_LIMIT = 56 * 1024 * 1024


def _split_bf16(x):
    hi = x.astype(BF16)
    lo = (x - hi.astype(F32)).astype(BF16)
    return hi, lo


def _dot(a, b):
    return jnp.dot(a, b, preferred_element_type=F32)


def _dot_nt(a, b):
    return lax.dot_general(a, b, (((1,), (1,)), ((), ())), preferred_element_type=F32)


def _dot_tn(a, b):
    return lax.dot_general(a, b, (((0,), (0,)), ((), ())), preferred_element_type=F32)


def _adaln_kernel(c_ref, w_ref, b_ref, o_ref):
    c = c_ref[...]
    a = c * jax.nn.sigmoid(c)
    ah, al = _split_bf16(a)
    wh, wl = _split_bf16(w_ref[...])
    o_ref[...] = _dot(ah, wh) + _dot(al, wh) + _dot(ah, wl) + b_ref[...]


def _adaln(conds, w_ada, b_ada):
    m, d = conds.shape
    n = w_ada.shape[1]
    tn = 512
    return pl.pallas_call(
        _adaln_kernel,
        grid=(n // tn,),
        in_specs=[pl.BlockSpec((m, d), lambda j: (0, 0)),
                  pl.BlockSpec((d, tn), lambda j: (0, j)),
                  pl.BlockSpec((1, tn), lambda j: (0, j))],
        out_specs=pl.BlockSpec((m, tn), lambda j: (0, j)),
        out_shape=jax.ShapeDtypeStruct((m, n), F32),
        name="adaln",
    )(conds, w_ada, b_ada.reshape(1, n))


def _rope_rotate(x, c, s):
    n = x.shape[-1]
    lane = lax.broadcasted_iota(jnp.int32, x.shape, 1)
    partner = jnp.where((lane & 1) == 0, pltpu.roll(x, n - 1, 1), pltpu.roll(x, 1, 1))
    return x * c + partner * s


def _premix_kernel(*refs, use_rope):
    (x_ref, mod_ref, nmix_ref, win_ref, wgh_ref, wgl_ref, bg_ref, bq_ref, bk_ref,
     qn_ref, kn_ref) = refs[:11]
    n_in = 11
    if use_rope:
        cq_ref, sq_ref, ck_ref, sk_ref = refs[11:15]
        n_in = 15
    (qa_o, ka_o, va_o, ga_o, laf_o, lab_o, q_o, k_o, v_o, kc_o, vc_o) = refs[n_in:]

    x = x_ref[0]
    mod = mod_ref[0]
    sh1 = mod[0:1]
    sc1 = mod[1:2]
    ms = jnp.mean(x * x, axis=-1, keepdims=True)
    h = x * lax.rsqrt(ms + RMS_EPS) * nmix_ref[...]
    h = h * (1.0 + sc1) + sh1
    proj = _dot(h.astype(BF16), win_ref[...])

    o = 0
    qa_o[0] = proj[:, o:o + W_QK] * (GLA_DK ** -0.5); o += W_QK
    ka_o[0] = proj[:, o:o + W_QK]; o += W_QK
    va_o[0] = proj[:, o:o + W_V]; o += W_V
    ga_o[0] = proj[:, o:o + W_V]; o += W_V
    qb = proj[:, o:o + W_ATT]; o += W_ATT
    kb = proj[:, o:o + W_KV]; o += W_KV
    vb = proj[:, o:o + W_KV]; o += W_KV
    lr = proj[:, o:o + LANES]

    lh, ll = _split_bf16(lr)
    xg = _dot(lh, wgh_ref[...]) + _dot(ll, wgh_ref[...]) + _dot(lh, wgl_ref[...]) + bg_ref[...]
    la = (jnp.minimum(xg, 0.0) - jnp.log1p(jnp.exp(-jnp.abs(xg)))) * (1.0 / GLA_GATE_NORM)
    laf_o[0] = la[:, :W_QK]
    lab_o[0] = la[:, W_QK:]

    def head_norm(t, ones_ref, g_ref):
        sh, sl = _split_bf16(t * t)
        msq = (_dot(sh, ones_ref[...]) + _dot(sl, ones_ref[...])) * (1.0 / HEAD_DIM)
        return t * lax.rsqrt(msq + RMS_EPS) * g_ref[...]

    qh = head_norm(qb, bq_ref, qn_ref)
    kh = head_norm(kb, bk_ref, kn_ref)
    kc_o[0] = kh
    vc_o[0] = vb
    if use_rope:
        qh = _rope_rotate(qh, cq_ref[...], sq_ref[...])
        kr = _rope_rotate(kh, ck_ref[...], sk_ref[...])
    else:
        kr = kh
    q_o[0] = (qh * (HEAD_DIM ** -0.5)).astype(BF16)
    k_o[0] = kr.astype(BF16)
    v_o[0] = vb.astype(BF16)


def _premix(x, mod, mod_shared, wp, rope):
    b, s, d = x.shape
    tm = min(TOKEN_TILE, s)
    use_rope = rope is not None
    const = lambda shape: pl.BlockSpec(shape, lambda i, j: (0,) * len(shape))
    tok = lambda w: pl.BlockSpec((1, tm, w), lambda i, j: (i, j, 0))
    mod_map = (lambda i, j: (0, 0, 0)) if mod_shared else (lambda i, j: (i, 0, 0))
    in_specs = [tok(d), pl.BlockSpec((1, 8, d), mod_map), const((1, d)),
                const((d, D_IN_PAD)), const((LANES, 2 * W_QK)), const((LANES, 2 * W_QK)),
                const((1, 2 * W_QK)), const((W_ATT, W_ATT)), const((W_KV, W_KV)),
                const((1, W_ATT)), const((1, W_KV))]
    args = [x, mod, wp["norm_mix"], wp["w_in"], wp["wg_hi"], wp["wg_lo"], wp["b_gate"],
            wp["ones_q"], wp["ones_k"], wp["q_norm"], wp["k_norm"]]
    if use_rope:
        seq = lambda w: pl.BlockSpec((tm, w), lambda i, j: (j, 0))
        in_specs += [seq(W_ATT), seq(W_ATT), seq(W_KV), seq(W_KV)]
        args += list(rope)
    widths = [(W_QK, F32), (W_QK, F32), (W_V, F32), (W_V, F32), (W_QK, F32), (W_QK, F32),
              (W_ATT, BF16), (W_KV, BF16), (W_KV, BF16), (W_KV, F32), (W_KV, F32)]
    return pl.pallas_call(
        functools.partial(_premix_kernel, use_rope=use_rope),
        grid=(b, s // tm),
        in_specs=in_specs,
        out_specs=[tok(w) for w, _ in widths],
        out_shape=[jax.ShapeDtypeStruct((b, s, w), dt) for w, dt in widths],
        compiler_params=pltpu.CompilerParams(
            dimension_semantics=("parallel", "parallel"), vmem_limit_bytes=VMEM_LIMIT),
        name="premix_rope" if use_rope else "premix",
    )(*args)


def _bcast_rows(x, n_par, p_rows, row):
    w = x.shape[-1]
    r = x.reshape(n_par, p_rows, w)[:, row:row + 1, :]
    return jnp.broadcast_to(r, (n_par, p_rows, w)).reshape(n_par * p_rows, w)


def _gla_kernel(q_ref, k_ref, v_ref, la_ref, s0_ref, o_ref, sf_ref, st_scr, *, rows, levels, reverse):
    blk = pl.program_id(1)

    @pl.when(blk == 0)
    def _():
        st_scr[...] = s0_ref[0]

    q = q_ref[0]
    k = k_ref[0]
    la = la_ref[0]
    vb = v_ref[0].astype(BF16)

    ri = lax.broadcasted_iota(jnp.int32, (rows, rows), 0)
    ci = lax.broadcasted_iota(jnp.int32, (rows, rows), 1)
    tri = jnp.where((ri <= ci) if reverse else (ri >= ci), 1.0, 0.0).astype(BF16)
    hi = la.astype(BF16)
    r1 = la - hi.astype(F32)
    mid = r1.astype(BF16)
    lo = (r1 - mid.astype(F32)).astype(BF16)
    cum = _dot(tri, hi) + _dot(tri, mid) + _dot(tri, lo)

    ridx = lax.broadcasted_iota(jnp.int32, (rows, W_QK), 0)
    s_acc = [None] * GLA_HEADS
    for (par, sub) in levels:
        groups = par // sub
        n_par = rows // par
        pi = (ridx % par) // sub
        edge = sub - 1 if reverse else 0
        if sub > 1:
            own = _bcast_rows(cum, rows // sub, sub, edge)
            qt = q * jnp.exp(cum - own)
        else:
            qt = q
        qps, kps = [], []
        for p in range(groups):
            if sub > 1 and p == (groups - 1 if reverse else 0):
                continue
            cp = _bcast_rows(cum, n_par, par, p * sub + edge)
            if reverse:
                km = (pi > p) if sub > 1 else (pi >= p)
            else:
                km = (pi < p) if sub > 1 else (pi <= p)
            ek = jnp.where(km, cp - cum, 0.0)
            kps.append(jnp.where(km, k * jnp.exp(ek), 0.0).astype(BF16))
            qps.append(jnp.where(pi == p, qt, 0.0).astype(BF16))
        ng = len(qps)
        lane_c = lax.broadcasted_iota(jnp.int32, (rows, LANES * ng), 1)
        same_parent = (ri // par) == (ci // par)
        for hp in range(GLA_HEADS // 2):
            qc = jnp.concatenate([a[:, LANES * hp:LANES * (hp + 1)] for a in qps], axis=1)
            kc = jnp.concatenate([a[:, LANES * hp:LANES * (hp + 1)] for a in kps], axis=1)
            for hh in range(2):
                qh = jnp.where(((lane_c % LANES) // GLA_DK) == hh, qc, jnp.zeros_like(qc))
                sl = _dot_nt(qh, kc)
                if par < rows:
                    sl = jnp.where(same_parent, sl, 0.0)
                h = 2 * hp + hh
                s_acc[h] = sl if s_acc[h] is None else s_acc[h] + sl

    q0 = (q * jnp.exp(cum)).astype(BF16)
    far = 0 if reverse else rows - 1
    last = cum[far:far + 1, :]
    kd = (k * jnp.exp(last - cum)).astype(BF16)
    br = lax.broadcasted_iota(jnp.int32, (2 * GLA_DV, 2 * GLA_DK), 0)
    bc = lax.broadcasted_iota(jnp.int32, (2 * GLA_DV, 2 * GLA_DK), 1)
    blockdiag = (br // GLA_DV) == (bc // GLA_DK)
    for hp in range(GLA_HEADS // 2):
        st = st_scr[hp]
        o_inter = _dot_nt(q0[:, LANES * hp:LANES * (hp + 1)], st.astype(BF16))
        o_intra = jnp.concatenate(
            [_dot(s_acc[2 * hp + hh].astype(BF16),
                  vb[:, GLA_DV * (2 * hp + hh):GLA_DV * (2 * hp + hh + 1)]) for hh in range(2)],
            axis=1)
        o_ref[0, :, 2 * GLA_DV * hp:2 * GLA_DV * (hp + 1)] = o_inter + o_intra
        upd = _dot_tn(vb[:, 2 * GLA_DV * hp:2 * GLA_DV * (hp + 1)],
                      kd[:, LANES * hp:LANES * (hp + 1)])
        st_scr[hp] = (jnp.exp(last[:, LANES * hp:LANES * (hp + 1)]) * st
                      + jnp.where(blockdiag, upd, 0.0))

    @pl.when(blk == pl.num_programs(1) - 1)
    def _():
        sf_ref[0] = st_scr[...]


def _gla(q, k, v, la, s0t, reverse):
    n, l, _ = q.shape
    rows = min(GLA_BLOCK, l)
    nb = l // rows
    levels = tuple((min(p, rows), s) for p, s in GLA_LEVELS)
    order = (lambda j: nb - 1 - j) if reverse else (lambda j: j)
    tok = lambda w: pl.BlockSpec((1, rows, w), lambda i, j: (i, order(j), 0))
    st_spec = pl.BlockSpec((1, 2, 2 * GLA_DV, 2 * GLA_DK), lambda i, j: (i, 0, 0, 0))
    return pl.pallas_call(
        functools.partial(_gla_kernel, rows=rows, levels=levels, reverse=reverse),
        grid=(n, nb),
        in_specs=[tok(W_QK), tok(W_QK), tok(W_V), tok(W_QK), st_spec],
        out_specs=[tok(W_V), st_spec],
        out_shape=[jax.ShapeDtypeStruct((n, l, W_V), F32),
                   jax.ShapeDtypeStruct(s0t.shape, F32)],
        scratch_shapes=[pltpu.VMEM((2, 2 * GLA_DV, 2 * GLA_DK), F32)],
        compiler_params=pltpu.CompilerParams(
            dimension_semantics=("parallel", "arbitrary"), vmem_limit_bytes=VMEM_LIMIT),
        name="gla_bwd" if reverse else "gla_fwd",
    )(q, k, v, la, s0t)


def _state_to_blockdiag_t(s):
    n = s.shape[0]
    st = jnp.swapaxes(s, -1, -2).reshape(n, 2, 2, GLA_DV, GLA_DK)
    z = jnp.zeros_like(st[:, :, 0])
    top = jnp.concatenate([st[:, :, 0], z], axis=-1)
    bot = jnp.concatenate([z, st[:, :, 1]], axis=-1)
    return jnp.concatenate([top, bot], axis=-2)


def _blockdiag_t_to_state(sb):
    n = sb.shape[0]
    h0 = sb[:, :, :GLA_DV, :GLA_DK]
    h1 = sb[:, :, GLA_DV:, GLA_DK:]
    st = jnp.stack([h0, h1], axis=2).reshape(n, GLA_HEADS, GLA_DV, GLA_DK)
    return jnp.swapaxes(st, -1, -2)


def _attn_kernel(*refs, s_self, n_ctx):
    if n_ctx:
        q_ref, k_ref, v_ref, ck_ref, cv_ref, o_ref, kg_scr, v_scr = refs
    else:
        q_ref, k_ref, v_ref, o_ref, kg_scr, v_scr = refs

    @pl.when(pl.program_id(1) == 0)
    def _():
        def put(kk, vv, start, n):
            lane = lax.broadcasted_iota(jnp.int32, kk.shape, 1)
            kg_scr[0, start:start + n, :] = jnp.where(lane < HEAD_DIM, kk, jnp.zeros_like(kk))
            kg_scr[1, start:start + n, :] = jnp.where(lane >= HEAD_DIM, kk, jnp.zeros_like(kk))
            v_scr[start:start + n, :] = vv
        put(k_ref[0], v_ref[0], 0, s_self)
        if n_ctx:
            put(ck_ref[0].astype(BF16), cv_ref[0].astype(BF16), s_self, n_ctx)

    q = q_ref[0]
    vv = v_scr[...]
    for m in range(ATT_HEADS // KV_HEADS):
        qm = q[:, LANES * m:LANES * (m + 1)]
        og = []
        for g in range(KV_HEADS):
            s = _dot_nt(qm, kg_scr[g])
            mx = jnp.max(s, axis=-1, keepdims=True)
            p = jnp.exp(s - mx)
            l = jnp.sum(p, axis=-1, keepdims=True)
            og.append(_dot(p.astype(BF16), vv) / l)
        lane = lax.broadcasted_iota(jnp.int32, og[0].shape, 1)
        o_ref[0, :, LANES * m:LANES * (m + 1)] = jnp.where(lane < HEAD_DIM, og[0], og[1]).astype(BF16)


def _attention(q, k, v, cache_k=None, cache_v=None):
    b, s, _ = q.shape
    n_ctx = 0 if cache_k is None else cache_k.shape[1]
    tq = min(ATT_Q_TILE, s)
    sk = s + n_ctx
    full = lambda n, w: pl.BlockSpec((1, n, w), lambda i, j: (i, 0, 0))
    in_specs = [pl.BlockSpec((1, tq, W_ATT), lambda i, j: (i, j, 0)), full(s, W_KV), full(s, W_KV)]
    args = [q, k, v]
    if n_ctx:
        in_specs += [full(n_ctx, W_KV), full(n_ctx, W_KV)]
        args += [cache_k, cache_v]
    return pl.pallas_call(
        functools.partial(_attn_kernel, s_self=s, n_ctx=n_ctx),
        grid=(b, s // tq),
        in_specs=in_specs,
        out_specs=pl.BlockSpec((1, tq, W_ATT), lambda i, j: (i, j, 0)),
        out_shape=jax.ShapeDtypeStruct((b, s, W_ATT), BF16),
        scratch_shapes=[pltpu.VMEM((KV_HEADS, sk, W_KV), BF16), pltpu.VMEM((sk, W_KV), BF16)],
        compiler_params=pltpu.CompilerParams(
            dimension_semantics=("parallel", "arbitrary"), vmem_limit_bytes=VMEM_LIMIT),
        name="attention_ctx" if n_ctx else "attention",
    )(*args)


def _postmix_kernel(of_ref, or_ref, ga_ref, oa_ref, x_ref, mod_ref, gn_ref, wout_ref, nffn_ref,
                    wrh_ref, wrl_ref, br_ref, x1_o, h2_o, lp_o, tg_o, meta_o):
    mod = mod_ref[0]
    g1, sh2, sc2 = mod[2:3], mod[3:4], mod[4:5]
    og = of_ref[0] + or_ref[0]
    ga = ga_ref[0]
    parts = []
    for h in range(GLA_HEADS):
        blk = og[:, GLA_DV * h:GLA_DV * (h + 1)]
        ms = jnp.mean(blk * blk, axis=-1, keepdims=True)
        gh = ga[:, GLA_DV * h:GLA_DV * (h + 1)]
        parts.append((blk * lax.rsqrt(ms + RMS_EPS) * gn_ref[...] * (gh * jax.nn.sigmoid(gh))).astype(BF16))
    mix = jnp.concatenate(parts + [oa_ref[0]], axis=1)
    mo = _dot(mix, wout_ref[...])
    x1 = x_ref[0] + g1 * mo
    x1_o[0] = x1
    ms = jnp.mean(x1 * x1, axis=-1, keepdims=True)
    h2 = x1 * lax.rsqrt(ms + RMS_EPS) * nffn_ref[...]
    h2 = h2 * (1.0 + sc2) + sh2
    hh, hl = _split_bf16(h2)
    h2_o[0] = hh

    lt = (_dot_nt(wrh_ref[...], hh) + _dot_nt(wrh_ref[...], hl) + _dot_nt(wrl_ref[...], hh)
          + br_ref[...])
    tm = lt.shape[1]
    eidx = lax.broadcasted_iota(jnp.int32, lt.shape, 0)
    vals, sels = [], []
    for _ in range(TOP_K):
        mx = jnp.max(lt, axis=0, keepdims=True)
        idx = jnp.min(jnp.where(lt == mx, eidx, N_EXPERTS), axis=0, keepdims=True)
        sel = eidx == idx
        lt = jnp.where(sel, -jnp.inf, lt)
        vals.append(mx)
        sels.append(sel)
    ws = [jnp.exp(vv - vals[0]) for vv in vals]
    tot = ws[0] + ws[1] + ws[2] + ws[3]
    tg_o[0] = jnp.concatenate([w / tot for w in ws] + [jnp.zeros((8 - TOP_K, tm), F32)], axis=0)

    onehot = sum(jnp.where(s, 1.0, 0.0) for s in sels)
    ti = lax.broadcasted_iota(jnp.int32, (tm, tm), 0)
    tj = lax.broadcasted_iota(jnp.int32, (tm, tm), 1)
    rank = _dot(onehot.astype(BF16), jnp.where(ti < tj, 1.0, 0.0).astype(BF16))
    cnt = jnp.sum(onehot, axis=1, keepdims=True)
    cnt = jnp.floor((cnt + 7.0) * 0.125) * 8.0
    ei = lax.broadcasted_iota(jnp.int32, (N_EXPERTS, N_EXPERTS), 0)
    ej = lax.broadcasted_iota(jnp.int32, (N_EXPERTS, N_EXPERTS), 1)
    cnt_b = jnp.broadcast_to(cnt, (N_EXPERTS, tm))
    seg = _dot(jnp.where(ej < ei, 1.0, 0.0).astype(BF16), cnt_b.astype(BF16))
    base = seg + rank
    lpos = [jnp.sum(jnp.where(s, base, 0.0), axis=0, keepdims=True).astype(jnp.int32) for s in sels]
    lp_o[0] = jnp.concatenate(lpos + [jnp.zeros((8 - TOP_K, tm), jnp.int32)], axis=0)
    meta_o[0] = jnp.concatenate([cnt_b[:, :LANES], seg[:, :LANES]], axis=0).astype(jnp.int32)


def _postmix(o_f, o_r, ga, o_att, x, mod, mod_shared, wp):
    b, s, d = x.shape
    tm = min(TOKEN_TILE, s)
    nt = s // tm
    const = lambda shape: pl.BlockSpec(shape, lambda i, j: (0,) * len(shape))
    tok = lambda w: pl.BlockSpec((1, tm, w), lambda i, j: (i, j, 0))
    mod_map = (lambda i, j: (0, 0, 0)) if mod_shared else (lambda i, j: (i, 0, 0))
    lane_tok = pl.BlockSpec((1, 8, tm), lambda i, j: (i * nt + j, 0, 0))
    return pl.pallas_call(
        _postmix_kernel,
        grid=(b, nt),
        in_specs=[tok(W_V), tok(W_V), tok(W_V), tok(W_ATT), tok(d), pl.BlockSpec((1, 8, d), mod_map),
                  const((1, GLA_DV)), const((d, d)), const((1, d)),
                  const((N_EXPERTS, d)), const((N_EXPERTS, d)), const((N_EXPERTS, 1))],
        out_specs=[tok(d), tok(d), lane_tok, lane_tok,
                   pl.BlockSpec((1, 2 * N_EXPERTS, LANES), lambda i, j: (i * nt + j, 0, 0))],
        out_shape=[jax.ShapeDtypeStruct((b, s, d), F32), jax.ShapeDtypeStruct((b, s, d), BF16),
                   jax.ShapeDtypeStruct((b * nt, 8, tm), jnp.int32),
                   jax.ShapeDtypeStruct((b * nt, 8, tm), F32),
                   jax.ShapeDtypeStruct((b * nt, 2 * N_EXPERTS, LANES), jnp.int32)],
        compiler_params=pltpu.CompilerParams(
            dimension_semantics=("parallel", "parallel"), vmem_limit_bytes=VMEM_LIMIT),
        name="postmix",
    )(o_f, o_r, ga, o_att, x, mod, wp["gla_norm"], wp["w_out"], wp["norm_ffn"],
      wp["wr_hi"], wp["wr_lo"], wp["b_router"])


SUBLANES = 8
SEG_PIECES = tuple(TOKEN_TILE >> b for b in range(6))
PAIR_ROWS = TOKEN_TILE * TOP_K
LOCAL_ROWS = PAIR_ROWS + N_EXPERTS * SUBLANES
TAB_REM = 3 * N_EXPERTS


def _pieces(n, emit):
    for size in SEG_PIECES:
        done = n & ~(2 * size - 1)
        pl.when((n & size) != 0)(functools.partial(emit, done, size))


def _segment_copies(tab_ref, copy):
    def body(e, carry):
        dst0 = tab_ref[0, 0, e]
        n = tab_ref[0, 0, N_EXPERTS + e]
        src0 = tab_ref[0, 0, 2 * N_EXPERTS + e]
        _pieces(n, lambda done, size: copy(pl.multiple_of(src0 + done, SUBLANES),
                                           pl.multiple_of(dst0 + done, SUBLANES), size))
        return carry
    lax.fori_loop(0, N_EXPERTS, body, 0)


def _wait_rows(rem, wait_piece):
    wait_piece(PAIR_ROWS)
    _pieces(rem, lambda done, size: wait_piece(size))


def _pair_onehot(lp_ref, weights=None):
    tm = lp_ref.shape[-1]
    rows = lax.broadcasted_iota(jnp.int32, (LOCAL_ROWS, tm), 0)
    acc = jnp.zeros((LOCAL_ROWS, tm), F32)
    for r in range(TOP_K):
        w = 1.0 if weights is None else weights[r]
        acc = jnp.where(rows == lp_ref[0, r:r + 1, :], w, acc)
    return acc


def _dispatch_kernel(tab_ref, pad_ref, h_ref, lp_ref, xs_hbm, xl, zbuf, rem_prev, sem, zsem):
    i = pl.program_id(0)
    slot = i % 2
    perm = _pair_onehot(lp_ref).astype(BF16)
    xl[slot] = _dot(perm, h_ref[...])

    def copy(local, glob, size):
        pltpu.make_async_copy(xl.at[slot, pl.ds(local, size)], xs_hbm.at[pl.ds(glob, size)],
                              sem.at[slot]).start()
    _segment_copies(tab_ref, copy)

    def wait_slot(s, rem):
        _wait_rows(rem, lambda size: pltpu.make_async_copy(
            xl.at[s, pl.ds(0, size)], xs_hbm.at[pl.ds(0, size)], sem.at[s]).wait())

    @pl.when(i > 0)
    def _():
        wait_slot(1 - slot, rem_prev[0])
    rem_prev[0] = tab_ref[0, 0, TAB_REM]

    @pl.when(i == pl.num_programs(0) - 1)
    def _():
        wait_slot(slot, tab_ref[0, 0, TAB_REM])
        zbuf[...] = jnp.zeros_like(zbuf)

        def pads(wait):
            def body(e, carry):
                start = pad_ref[0, e]
                def emit(done, size):
                    cp = pltpu.make_async_copy(
                        zbuf.at[pl.ds(0, size)],
                        xs_hbm.at[pl.ds(pl.multiple_of(start + done, SUBLANES), size)], zsem)
                    cp.wait() if wait else cp.start()
                _pieces(pad_ref[0, N_EXPERTS + e], emit)
                return carry
            lax.fori_loop(0, N_EXPERTS, body, 0)
        n_tiles = xs_hbm.shape[0] // EXPERT_TILE

        def tail(wait):
            def body(t, carry):
                cp = pltpu.make_async_copy(
                    zbuf, xs_hbm.at[pl.ds(pl.multiple_of(t * EXPERT_TILE, EXPERT_TILE), EXPERT_TILE)], zsem)
                cp.wait() if wait else cp.start()
                return carry
            lax.fori_loop(pad_ref[0, 2 * N_EXPERTS], n_tiles, body, 0)
        pads(False)
        tail(False)
        pads(True)
        tail(True)


def _dispatch(h2, lpos, table, pad_table, n_rows):
    t, d = h2.shape
    tm = TOKEN_TILE
    nt = t // tm
    return pl.pallas_call(
        _dispatch_kernel,
        grid=(nt,),
        in_specs=[pl.BlockSpec((1, 1, LANES), lambda i: (i, 0, 0), memory_space=pltpu.SMEM),
                  pl.BlockSpec((1, LANES), lambda i: (0, 0), memory_space=pltpu.SMEM),
                  pl.BlockSpec((tm, d), lambda i: (i, 0)),
                  pl.BlockSpec((1, 8, tm), lambda i: (i, 0, 0))],
        out_specs=pl.BlockSpec(memory_space=pl.ANY),
        out_shape=jax.ShapeDtypeStruct((n_rows, d), F32),
        scratch_shapes=[pltpu.VMEM((2, LOCAL_ROWS, d), F32), pltpu.VMEM((EXPERT_TILE, d), F32),
                        pltpu.SMEM((1,), jnp.int32),
                        pltpu.SemaphoreType.DMA((2,)), pltpu.SemaphoreType.DMA(())],
        compiler_params=pltpu.CompilerParams(
            dimension_semantics=("arbitrary",), vmem_limit_bytes=VMEM_LIMIT),
        name="dispatch",
    )(table, pad_table, h2, lpos)


def _experts_kernel(te_ref, nv_ref, xs_ref, wgu_ref, bgu_ref, wd_ref, bd_ref, o_ref, wgu_bf, wd_bf):
    i = pl.program_id(0)
    valid = i < nv_ref[0]
    new_expert = jnp.logical_or(i == 0, te_ref[i] != te_ref[jnp.maximum(i - 1, 0)])

    @pl.when(jnp.logical_and(valid, new_expert))
    def _():
        wgu_bf[...] = wgu_ref[0].astype(BF16)
        wd_bf[...] = wd_ref[0].astype(BF16)

    @pl.when(valid)
    def _():
        x = xs_ref[...].astype(BF16)
        gu = _dot(x, wgu_bf[...]) + bgu_ref[0]
        g = jnp.minimum(gu[:, :EXPERT_FF], SWIGLU_LIMIT)
        u = jnp.clip(gu[:, EXPERT_FF:], -SWIGLU_LIMIT, SWIGLU_LIMIT)
        act = (u + 1.0) * (g * jax.nn.sigmoid(SWIGLU_ALPHA * g))
        o_ref[...] = _dot(act.astype(BF16), wd_bf[...]) + bd_ref[0]

    @pl.when(jnp.logical_not(valid))
    def _():
        o_ref[...] = jnp.zeros_like(o_ref)


def _experts(xs, tile_expert, n_valid, w_gu, b_gu, w_down, b_down):
    n_rows, d = xs.shape
    tm = EXPERT_TILE
    nt = n_rows // tm
    ne, _, ff2 = w_gu.shape
    ff = ff2 // 2
    grid_spec = pltpu.PrefetchScalarGridSpec(
        num_scalar_prefetch=2,
        grid=(nt,),
        in_specs=[pl.BlockSpec((tm, d), lambda i, te, nv: (i, 0)),
                  pl.BlockSpec((1, d, ff2), lambda i, te, nv: (te[i], 0, 0)),
                  pl.BlockSpec((1, 1, ff2), lambda i, te, nv: (te[i], 0, 0)),
                  pl.BlockSpec((1, ff, d), lambda i, te, nv: (te[i], 0, 0)),
                  pl.BlockSpec((1, 1, d), lambda i, te, nv: (te[i], 0, 0))],
        out_specs=pl.BlockSpec((tm, d), lambda i, te, nv: (i, 0)),
        scratch_shapes=[pltpu.VMEM((d, ff2), BF16), pltpu.VMEM((ff, d), BF16)],
    )
    return pl.pallas_call(
        _experts_kernel,
        grid_spec=grid_spec,
        out_shape=jax.ShapeDtypeStruct((n_rows, d), F32),
        compiler_params=pltpu.CompilerParams(
            dimension_semantics=("arbitrary",), vmem_limit_bytes=VMEM_LIMIT),
        name="experts",
    )(tile_expert, n_valid, xs, w_gu, b_gu.reshape(ne, 1, ff2), w_down, b_down.reshape(ne, 1, d))


def _combine_kernel(tab_ref, ys_hbm, lp_ref, tg_ref, x1_ref, mod_ref, fn_ref, y_ref, buf, sem):
    def copy(local, glob, size):
        pltpu.make_async_copy(ys_hbm.at[pl.ds(glob, size)], buf.at[pl.ds(local, size)], sem).start()
    _segment_copies(tab_ref, copy)
    tg = tg_ref[0]
    wh, wl = _split_bf16(_pair_onehot(lp_ref, [tg[r:r + 1, :] for r in range(TOP_K)]))
    _wait_rows(tab_ref[0, 0, TAB_REM], lambda size: pltpu.make_async_copy(
        ys_hbm.at[pl.ds(0, size)], buf.at[pl.ds(0, size)], sem).wait())
    nrow = PAIR_ROWS + tab_ref[0, 0, TAB_REM]
    rowi = lax.broadcasted_iota(jnp.int32, buf.shape, 0)
    yb = jnp.where(rowi < nrow, buf[...], 0.0).astype(BF16)
    y = _dot_tn(wh, yb) + _dot_tn(wl, yb)
    g2 = mod_ref[0][5:6]
    x2 = x1_ref[0] + g2 * y
    ms = jnp.mean(x2 * x2, axis=-1, keepdims=True)
    y_ref[0] = x2 * lax.rsqrt(ms + RMS_EPS) * fn_ref[...]


def _combine(ys, table, lpos, gates, x1, mod, mod_shared, final_norm):
    b, s, d = x1.shape
    tm = TOKEN_TILE
    nt = s // tm
    mod_map = (lambda i, j: (0, 0, 0)) if mod_shared else (lambda i, j: (i, 0, 0))
    tile = lambda shape: pl.BlockSpec(shape, lambda i, j: (i * nt + j, 0, 0))
    return pl.pallas_call(
        _combine_kernel,
        grid=(b, nt),
        in_specs=[pl.BlockSpec((1, 1, LANES), lambda i, j: (i * nt + j, 0, 0), memory_space=pltpu.SMEM),
                  pl.BlockSpec(memory_space=pl.ANY),
                  tile((1, 8, tm)), tile((1, 8, tm)),
                  pl.BlockSpec((1, tm, d), lambda i, j: (i, j, 0)),
                  pl.BlockSpec((1, 8, d), mod_map),
                  pl.BlockSpec((1, d), lambda i, j: (0, 0))],
        out_specs=pl.BlockSpec((1, tm, d), lambda i, j: (i, j, 0)),
        out_shape=jax.ShapeDtypeStruct((b, s, d), F32),
        scratch_shapes=[pltpu.VMEM((LOCAL_ROWS, d), F32), pltpu.SemaphoreType.DMA(())],
        compiler_params=pltpu.CompilerParams(
            dimension_semantics=("arbitrary", "arbitrary"), vmem_limit_bytes=VMEM_LIMIT),
        name="combine",
    )(table, ys, lpos, gates, x1, mod, final_norm.reshape(1, d))


def _route_tables(meta, n_tiles_e):
    tm = EXPERT_TILE
    cnt = meta[:, :N_EXPERTS, 0]
    lstart = meta[:, N_EXPERTS:, 0]
    tot = jnp.sum(cnt, axis=0)
    tiles = (tot + tm - 1) // tm
    tile_end = jnp.cumsum(tiles)
    off = (tile_end - tiles) * tm
    dest = off[None, :] + jnp.cumsum(cnt, axis=0) - cnt
    rem = jnp.sum(cnt, axis=1, keepdims=True) - PAIR_ROWS
    table = jnp.concatenate([dest, cnt, lstart, jnp.broadcast_to(rem, cnt.shape)], axis=1).astype(jnp.int32)
    n_valid = tile_end[-1].astype(jnp.int32).reshape(1)
    pad_table = jnp.concatenate([off + tot, tiles * tm - tot, jnp.broadcast_to(n_valid, (2 * N_EXPERTS,))])
    pad_table = pad_table.astype(jnp.int32).reshape(1, LANES)
    tile_ids = jnp.minimum(jnp.arange(n_tiles_e), n_valid[0] - 1)
    tile_expert = jnp.searchsorted(tile_end, tile_ids, side="right").astype(jnp.int32)
    return table.reshape(-1, 1, LANES), pad_table, tile_expert, n_valid


def _rope_tables(n_tok):
    rows = n_tok // GRID_W
    r, col = jnp.meshgrid(jnp.arange(rows), jnp.arange(GRID_W), indexing="ij")
    r = r.reshape(-1).astype(F32)
    col = col.reshape(-1).astype(F32)
    inv = 1.0 / (ROPE_THETA ** (jnp.arange(ROPE_AXIS_PAIRS, dtype=F32) / ROPE_AXIS_PAIRS))
    ang = jnp.concatenate([r[:, None] * inv, col[:, None] * inv], axis=-1)
    c64 = jnp.repeat(jnp.cos(ang), 2, axis=-1)
    sign = jnp.tile(jnp.array([-1.0, 1.0], F32), HEAD_DIM // 2)
    s64 = jnp.repeat(jnp.sin(ang), 2, axis=-1) * sign
    return (jnp.tile(c64, (1, ATT_HEADS)), jnp.tile(s64, (1, ATT_HEADS)),
            jnp.tile(c64, (1, KV_HEADS)), jnp.tile(s64, (1, KV_HEADS)))


def _prep_weights(w_in, w_gate_f, b_gate_f, w_gate_b, b_gate_b, gla_norm, q_norm, k_norm, w_out,
                  norm_mix, norm_ffn, w_router, b_router):
    d = w_in.shape[0]
    o_lr = 2 * W_QK + 2 * W_V
    o_q = o_lr + 2 * GLA_GATE_RANK
    o_k = o_q + W_ATT
    head_order = np.array([0, 4, 1, 5, 2, 6, 3, 7])
    perm = (head_order[:, None] * HEAD_DIM + np.arange(HEAD_DIM)[None, :]).reshape(-1)
    w_q = w_in[:, o_q:o_k][:, perm]
    lr_pad = jnp.zeros((d, LANES - 2 * GLA_GATE_RANK), w_in.dtype)
    w_in_p = jnp.concatenate([w_in[:, :o_lr], w_q, w_in[:, o_k:], w_in[:, o_lr:o_q], lr_pad], axis=1)
    wg = jnp.zeros((LANES, 2 * W_QK), F32)
    wg = wg.at[:GLA_GATE_RANK, :W_QK].set(w_gate_f)
    wg = wg.at[GLA_GATE_RANK:2 * GLA_GATE_RANK, W_QK:].set(w_gate_b)
    wg_hi, wg_lo = _split_bf16(wg)
    ones_q = jnp.asarray(np.kron(np.eye(ATT_HEADS), np.ones((HEAD_DIM, HEAD_DIM))), BF16)
    ones_k = jnp.asarray(np.kron(np.eye(KV_HEADS), np.ones((HEAD_DIM, HEAD_DIM))), BF16)
    w_out_p = jnp.concatenate([w_out[:W_V], w_out[W_V:][perm]], axis=0).astype(BF16)
    wr_hi, wr_lo = _split_bf16(w_router.T)
    return {
        "norm_mix": norm_mix.reshape(1, d), "w_in": w_in_p.astype(BF16),
        "wg_hi": wg_hi, "wg_lo": wg_lo,
        "b_gate": jnp.concatenate([b_gate_f, b_gate_b]).reshape(1, 2 * W_QK),
        "ones_q": ones_q, "ones_k": ones_k,
        "q_norm": jnp.tile(q_norm, ATT_HEADS).reshape(1, W_ATT),
        "k_norm": jnp.tile(k_norm, KV_HEADS).reshape(1, W_KV),
        "gla_norm": gla_norm.reshape(1, GLA_DV), "w_out": w_out_p,
        "norm_ffn": norm_ffn.reshape(1, d), "wr_hi": wr_hi, "wr_lo": wr_lo,
        "b_router": b_router.reshape(N_EXPERTS, 1),
    }


def _mixer(x, mod, mod_shared, wp, rope, cache_k, cache_v, s0_f, s0_b):
    b, s, _ = x.shape
    qa, ka, va, ga, la_f, la_b, q, k, v, kc, vc = _premix(x, mod, mod_shared, wp, rope)
    o_f, sf = _gla(qa, ka, va, la_f, _state_to_blockdiag_t(s0_f), False)
    o_r, sb = _gla(qa, ka, va, la_b, _state_to_blockdiag_t(s0_b), True)
    o_att = _attention(q, k, v, cache_k, cache_v)
    x1, h2, lpos, gates, meta = _postmix(o_f, o_r, ga, o_att, x, mod, mod_shared, wp)
    return x1, h2, lpos, gates, meta, kc, vc, _blockdiag_t_to_state(sf), _blockdiag_t_to_state(sb)


def kernel(x_prompt, x_sample, c, cache_k, cache_v, state_gla_fwd, state_gla_bwd, c_ctx, w_ada, b_ada,
           norm_mix, w_in, w_gate_f, b_gate_f, w_gate_b, b_gate_b, gla_norm, q_norm, k_norm, w_out,
           norm_ffn, w_router, b_router, w_gu, b_gu, w_down, b_down, final_norm):
    bp, sp, d = x_prompt.shape
    bs, ss, _ = x_sample.shape
    assert w_ada.shape[0] == 1, "single-layer trunk"
    wp = _prep_weights(w_in[0], w_gate_f[0], b_gate_f[0], w_gate_b[0], b_gate_b[0], gla_norm[0],
                       q_norm[0], k_norm[0], w_out[0], norm_mix[0], norm_ffn[0], w_router[0], b_router[0])

    n_cond = -(-(1 + bs) // 8) * 8
    conds = jnp.zeros((n_cond, d), F32).at[0].set(c_ctx).at[1:1 + bs].set(c)
    mod = _adaln(conds, w_ada[0], b_ada[0]).reshape(n_cond, 6, d)
    mod = jnp.concatenate([mod, jnp.zeros((n_cond, 2, d), F32)], axis=1)
    mod_p, mod_s = mod[0:1], mod[1:1 + bs]

    zero_state = jnp.zeros((bp, GLA_HEADS, GLA_DK, GLA_DV), F32)
    x1p, h2p, lpp, tgp, metap, kc, vc, sf, sb = _mixer(x_prompt, mod_p, True, wp, None, None, None,
                                                       zero_state, zero_state)
    n_ctx = cache_k.shape[2]
    x1s, h2s, lps, tgs, metas, _, _, _, _ = _mixer(
        x_sample, mod_s, False, wp, _rope_tables(ss),
        cache_k[:, 0].reshape(bs, n_ctx, W_KV), cache_v[:, 0].reshape(bs, n_ctx, W_KV),
        state_gla_fwd[:, 0].astype(F32), state_gla_bwd[:, 0].astype(F32))

    n_p, n_s = bp * sp, bs * ss
    n_tok = n_p + n_s
    ntp = n_p // TOKEN_TILE
    h2 = jnp.concatenate([h2p.reshape(n_p, d), h2s.reshape(n_s, d)], axis=0)
    lpos = jnp.concatenate([lpp, lps], axis=0)
    worst = n_tok * TOP_K + (n_tok // TOKEN_TILE) * N_EXPERTS * (SUBLANES - 1) + N_EXPERTS * (EXPERT_TILE - SUBLANES)
    n_rows = -(-worst // EXPERT_TILE) * EXPERT_TILE
    table, pad_table, tile_expert, n_valid = _route_tables(
        jnp.concatenate([metap, metas], axis=0), n_rows // EXPERT_TILE)
    xs = _dispatch(h2, lpos, table, pad_table, n_rows)
    ys = _experts(xs, tile_expert, n_valid, w_gu[0], b_gu[0], w_down[0], b_down[0])
    y_prompt = _combine(ys, table[:ntp], lpp, tgp, x1p, mod_p, True, final_norm)
    y_sample = _combine(ys, table[ntp:], lps, tgs, x1s, mod_s, False, final_norm)

    new_cache_k = kc.reshape(bp, 1, sp, KV_HEADS, HEAD_DIM)
    new_cache_v = vc.reshape(bp, 1, sp, KV_HEADS, HEAD_DIM)
    return (y_prompt, y_sample, new_cache_k, new_cache_v, sf[:, None], sb[:, None])
```

```python
import functools

import numpy as np
import jax
import jax.numpy as jnp
from jax import lax
from jax.experimental import pallas as pl
from jax.experimental.pallas import tpu as pltpu

F32 = jnp.float32
BF16 = jnp.bfloat16

D_MODEL = 1024
GRID_W = 64
GLA_HEADS = 4
GLA_DV = 128
GLA_DK = 64
GLA_GATE_RANK = 16
GLA_GATE_NORM = 16.0
HEAD_DIM = 64
ATT_HEADS = 8
KV_HEADS = 2
ROPE_THETA = 10000.0
ROPE_AXIS_PAIRS = HEAD_DIM // 4
N_EXPERTS = 32
TOP_K = 4
EXPERT_FF = 1024
SWIGLU_LIMIT = 7.0
SWIGLU_ALPHA = 1.702
RMS_EPS = 1e-6

W_QK = GLA_HEADS * GLA_DK
W_V = GLA_HEADS * GLA_DV
W_ATT = ATT_HEADS * HEAD_DIM
W_KV = KV_HEADS * HEAD_DIM
LANES = 128
D_IN_PAD = 2 * W_QK + 2 * W_V + W_ATT + 2 * W_KV + LANES

TOKEN_TILE = 256
GLA_BLOCK = 256
GLA_LEVELS = ((256, 32), (32, 8), (8, 1))
ATT_Q_TILE = 256
EXPERT_TILE = 512
VMEM_LIMIT = 56 * 1024 * 1024


def _split_bf16(x):
    hi = x.astype(BF16)
    lo = (x - hi.astype(F32)).astype(BF16)
    return hi, lo


def _dot(a, b):
    return jnp.dot(a, b, preferred_element_type=F32)


def _dot_nt(a, b):
    return lax.dot_general(a, b, (((1,), (1,)), ((), ())), preferred_element_type=F32)


def _dot_tn(a, b):
    return lax.dot_general(a, b, (((0,), (0,)), ((), ())), preferred_element_type=F32)


def _adaln_kernel(c_ref, w_ref, b_ref, o_ref):
    c = c_ref[...]
    a = c * jax.nn.sigmoid(c)
    ah, al = _split_bf16(a)
    wh, wl = _split_bf16(w_ref[...])
    o_ref[...] = _dot(ah, wh) + _dot(al, wh) + _dot(ah, wl) + b_ref[...]


def _adaln(conds, w_ada, b_ada):
    m, d = conds.shape
    n = w_ada.shape[1]
    tn = 512
    return pl.pallas_call(
        _adaln_kernel,
        grid=(n // tn,),
        in_specs=[pl.BlockSpec((m, d), lambda j: (0, 0)),
                  pl.BlockSpec((d, tn), lambda j: (0, j)),
                  pl.BlockSpec((1, tn), lambda j: (0, j))],
        out_specs=pl.BlockSpec((m, tn), lambda j: (0, j)),
        out_shape=jax.ShapeDtypeStruct((m, n), F32),
        name="adaln",
    )(conds, w_ada, b_ada.reshape(1, n))


def _rope_rotate(x, c, s):
    n = x.shape[-1]
    lane = lax.broadcasted_iota(jnp.int32, x.shape, 1)
    partner = jnp.where((lane & 1) == 0, pltpu.roll(x, n - 1, 1), pltpu.roll(x, 1, 1))
    return x * c + partner * s


def _premix_kernel(*refs, use_rope):
    (x_ref, mod_ref, nmix_ref, win_ref, wgh_ref, wgl_ref, bg_ref, bq_ref, bk_ref,
     qn_ref, kn_ref) = refs[:11]
    n_in = 11
    if use_rope:
        cq_ref, sq_ref, ck_ref, sk_ref = refs[11:15]
        n_in = 15
    (qa_o, ka_o, va_o, ga_o, laf_o, lab_o, q_o, k_o, v_o, kc_o, vc_o) = refs[n_in:]

    x = x_ref[0]
    mod = mod_ref[0]
    sh1 = mod[0:1]
    sc1 = mod[1:2]
    ms = jnp.mean(x * x, axis=-1, keepdims=True)
    h = x * lax.rsqrt(ms + RMS_EPS) * nmix_ref[...]
    h = h * (1.0 + sc1) + sh1
    proj = _dot(h.astype(BF16), win_ref[...])

    o = 0
    qa_o[0] = proj[:, o:o + W_QK] * (GLA_DK ** -0.5); o += W_QK
    ka_o[0] = proj[:, o:o + W_QK]; o += W_QK
    va_o[0] = proj[:, o:o + W_V]; o += W_V
    ga_o[0] = proj[:, o:o + W_V]; o += W_V
    qb = proj[:, o:o + W_ATT]; o += W_ATT
    kb = proj[:, o:o + W_KV]; o += W_KV
    vb = proj[:, o:o + W_KV]; o += W_KV
    lr = proj[:, o:o + LANES]

    lh, ll = _split_bf16(lr)
    xg = _dot(lh, wgh_ref[...]) + _dot(ll, wgh_ref[...]) + _dot(lh, wgl_ref[...]) + bg_ref[...]
    la = (jnp.minimum(xg, 0.0) - jnp.log1p(jnp.exp(-jnp.abs(xg)))) * (1.0 / GLA_GATE_NORM)
    laf_o[0] = la[:, :W_QK]
    lab_o[0] = la[:, W_QK:]

    def head_norm(t, ones_ref, g_ref):
        sh, sl = _split_bf16(t * t)
        msq = (_dot(sh, ones_ref[...]) + _dot(sl, ones_ref[...])) * (1.0 / HEAD_DIM)
        return t * lax.rsqrt(msq + RMS_EPS) * g_ref[...]

    qh = head_norm(qb, bq_ref, qn_ref)
    kh = head_norm(kb, bk_ref, kn_ref)
    kc_o[0] = kh
    vc_o[0] = vb
    if use_rope:
        qh = _rope_rotate(qh, cq_ref[...], sq_ref[...])
        kr = _rope_rotate(kh, ck_ref[...], sk_ref[...])
    else:
        kr = kh
    q_o[0] = (qh * (HEAD_DIM ** -0.5)).astype(BF16)
    k_o[0] = kr.astype(BF16)
    v_o[0] = vb.astype(BF16)


def _premix(x, mod, mod_shared, wp, rope):
    b, s, d = x.shape
    tm = min(TOKEN_TILE, s)
    use_rope = rope is not None
    const = lambda shape: pl.BlockSpec(shape, lambda i, j: (0,) * len(shape))
    tok = lambda w: pl.BlockSpec((1, tm, w), lambda i, j: (i, j, 0))
    mod_map = (lambda i, j: (0, 0, 0)) if mod_shared else (lambda i, j: (i, 0, 0))
    in_specs = [tok(d), pl.BlockSpec((1, 8, d), mod_map), const((1, d)),
                const((d, D_IN_PAD)), const((LANES, 2 * W_QK)), const((LANES, 2 * W_QK)),
                const((1, 2 * W_QK)), const((W_ATT, W_ATT)), const((W_KV, W_KV)),
                const((1, W_ATT)), const((1, W_KV))]
    args = [x, mod, wp["norm_mix"], wp["w_in"], wp["wg_hi"], wp["wg_lo"], wp["b_gate"],
            wp["ones_q"], wp["ones_k"], wp["q_norm"], wp["k_norm"]]
    if use_rope:
        seq = lambda w: pl.BlockSpec((tm, w), lambda i, j: (j, 0))
        in_specs += [seq(W_ATT), seq(W_ATT), seq(W_KV), seq(W_KV)]
        args += list(rope)
    widths = [(W_QK, F32), (W_QK, F32), (W_V, F32), (W_V, F32), (W_QK, F32), (W_QK, F32),
              (W_ATT, BF16), (W_KV, BF16), (W_KV, BF16), (W_KV, F32), (W_KV, F32)]
    return pl.pallas_call(
        functools.partial(_premix_kernel, use_rope=use_rope),
        grid=(b, s // tm),
        in_specs=in_specs,
        out_specs=[tok(w) for w, _ in widths],
        out_shape=[jax.ShapeDtypeStruct((b, s, w), dt) for w, dt in widths],
        compiler_params=pltpu.CompilerParams(
            dimension_semantics=("parallel", "parallel"), vmem_limit_bytes=VMEM_LIMIT),
        name="premix_rope" if use_rope else "premix",
    )(*args)


def _bcast_rows(x, n_par, p_rows, row):
    w = x.shape[-1]
    r = x.reshape(n_par, p_rows, w)[:, row:row + 1, :]
    return jnp.broadcast_to(r, (n_par, p_rows, w)).reshape(n_par * p_rows, w)


def _gla_kernel(q_ref, k_ref, v_ref, la_ref, s0_ref, o_ref, sf_ref, st_scr, *, rows, levels, reverse):
    blk = pl.program_id(1)

    @pl.when(blk == 0)
    def _():
        st_scr[...] = s0_ref[0]

    q = q_ref[0]
    k = k_ref[0]
    la = la_ref[0]
    vb = v_ref[0].astype(BF16)

    ri = lax.broadcasted_iota(jnp.int32, (rows, rows), 0)
    ci = lax.broadcasted_iota(jnp.int32, (rows, rows), 1)
    tri = jnp.where((ri <= ci) if reverse else (ri >= ci), 1.0, 0.0).astype(BF16)
    hi = la.astype(BF16)
    r1 = la - hi.astype(F32)
    mid = r1.astype(BF16)
    lo = (r1 - mid.astype(F32)).astype(BF16)
    cum = _dot(tri, hi) + _dot(tri, mid) + _dot(tri, lo)

    ridx = lax.broadcasted_iota(jnp.int32, (rows, W_QK), 0)
    s_acc = [None] * GLA_HEADS
    for (par, sub) in levels:
        groups = par // sub
        n_par = rows // par
        pi = (ridx % par) // sub
        edge = sub - 1 if reverse else 0
        if sub > 1:
            own = _bcast_rows(cum, rows // sub, sub, edge)
            qt = q * jnp.exp(cum - own)
        else:
            qt = q
        qps, kps = [], []
        for p in range(groups):
            if sub > 1 and p == (groups - 1 if reverse else 0):
                continue
            cp = _bcast_rows(cum, n_par, par, p * sub + edge)
            if reverse:
                km = (pi > p) if sub > 1 else (pi >= p)
            else:
                km = (pi < p) if sub > 1 else (pi <= p)
            ek = jnp.where(km, cp - cum, 0.0)
            kps.append(jnp.where(km, k * jnp.exp(ek), 0.0).astype(BF16))
            qps.append(jnp.where(pi == p, qt, 0.0).astype(BF16))
        ng = len(qps)
        lane_c = lax.broadcasted_iota(jnp.int32, (rows, LANES * ng), 1)
        same_parent = (ri // par) == (ci // par)
        for hp in range(GLA_HEADS // 2):
            qc = jnp.concatenate([a[:, LANES * hp:LANES * (hp + 1)] for a in qps], axis=1)
            kc = jnp.concatenate([a[:, LANES * hp:LANES * (hp + 1)] for a in kps], axis=1)
            for hh in range(2):
                qh = jnp.where(((lane_c % LANES) // GLA_DK) == hh, qc, jnp.zeros_like(qc))
                sl = _dot_nt(qh, kc)
                if par < rows:
                    sl = jnp.where(same_parent, sl, 0.0)
                h = 2 * hp + hh
                s_acc[h] = sl if s_acc[h] is None else s_acc[h] + sl

    q0 = (q * jnp.exp(cum)).astype(BF16)
    far = 0 if reverse else rows - 1
    last = cum[far:far + 1, :]
    kd = (k * jnp.exp(last - cum)).astype(BF16)
    br = lax.broadcasted_iota(jnp.int32, (2 * GLA_DV, 2 * GLA_DK), 0)
    bc = lax.broadcasted_iota(jnp.int32, (2 * GLA_DV, 2 * GLA_DK), 1)
    blockdiag = (br // GLA_DV) == (bc // GLA_DK)
    for hp in range(GLA_HEADS // 2):
        st = st_scr[hp]
        o_inter = _dot_nt(q0[:, LANES * hp:LANES * (hp + 1)], st.astype(BF16))
        o_intra = jnp.concatenate(
            [_dot(s_acc[2 * hp + hh].astype(BF16),
                  vb[:, GLA_DV * (2 * hp + hh):GLA_DV * (2 * hp + hh + 1)]) for hh in range(2)],
            axis=1)
        o_ref[0, :, 2 * GLA_DV * hp:2 * GLA_DV * (hp + 1)] = o_inter + o_intra
        upd = _dot_tn(vb[:, 2 * GLA_DV * hp:2 * GLA_DV * (hp + 1)],
                      kd[:, LANES * hp:LANES * (hp + 1)])
        st_scr[hp] = (jnp.exp(last[:, LANES * hp:LANES * (hp + 1)]) * st
                      + jnp.where(blockdiag, upd, 0.0))

    @pl.when(blk == pl.num_programs(1) - 1)
    def _():
        sf_ref[0] = st_scr[...]


def _gla(q, k, v, la, s0t, reverse):
    n, l, _ = q.shape
    rows = min(GLA_BLOCK, l)
    nb = l // rows
    levels = tuple((min(p, rows), s) for p, s in GLA_LEVELS)
    order = (lambda j: nb - 1 - j) if reverse else (lambda j: j)
    tok = lambda w: pl.BlockSpec((1, rows, w), lambda i, j: (i, order(j), 0))
    st_spec = pl.BlockSpec((1, 2, 2 * GLA_DV, 2 * GLA_DK), lambda i, j: (i, 0, 0, 0))
    return pl.pallas_call(
        functools.partial(_gla_kernel, rows=rows, levels=levels, reverse=reverse),
        grid=(n, nb),
        in_specs=[tok(W_QK), tok(W_QK), tok(W_V), tok(W_QK), st_spec],
        out_specs=[tok(W_V), st_spec],
        out_shape=[jax.ShapeDtypeStruct((n, l, W_V), F32),
                   jax.ShapeDtypeStruct(s0t.shape, F32)],
        scratch_shapes=[pltpu.VMEM((2, 2 * GLA_DV, 2 * GLA_DK), F32)],
        compiler_params=pltpu.CompilerParams(
            dimension_semantics=("parallel", "arbitrary"), vmem_limit_bytes=VMEM_LIMIT),
        name="gla_bwd" if reverse else "gla_fwd",
    )(q, k, v, la, s0t)


def _state_to_blockdiag_t(s):
    n = s.shape[0]
    st = jnp.swapaxes(s, -1, -2).reshape(n, 2, 2, GLA_DV, GLA_DK)
    z = jnp.zeros_like(st[:, :, 0])
    top = jnp.concatenate([st[:, :, 0], z], axis=-1)
    bot = jnp.concatenate([z, st[:, :, 1]], axis=-1)
    return jnp.concatenate([top, bot], axis=-2)


def _blockdiag_t_to_state(sb):
    n = sb.shape[0]
    h0 = sb[:, :, :GLA_DV, :GLA_DK]
    h1 = sb[:, :, GLA_DV:, GLA_DK:]
    st = jnp.stack([h0, h1], axis=2).reshape(n, GLA_HEADS, GLA_DV, GLA_DK)
    return jnp.swapaxes(st, -1, -2)


def _attn_kernel(*refs, s_self, n_ctx):
    if n_ctx:
        q_ref, k_ref, v_ref, ck_ref, cv_ref, o_ref, kg_scr, v_scr = refs
    else:
        q_ref, k_ref, v_ref, o_ref, kg_scr, v_scr = refs

    @pl.when(pl.program_id(1) == 0)
    def _():
        def put(kk, vv, start, n):
            lane = lax.broadcasted_iota(jnp.int32, kk.shape, 1)
            kg_scr[0, start:start + n, :] = jnp.where(lane < HEAD_DIM, kk, jnp.zeros_like(kk))
            kg_scr[1, start:start + n, :] = jnp.where(lane >= HEAD_DIM, kk, jnp.zeros_like(kk))
            v_scr[start:start + n, :] = vv
        put(k_ref[0], v_ref[0], 0, s_self)
        if n_ctx:
            put(ck_ref[0].astype(BF16), cv_ref[0].astype(BF16), s_self, n_ctx)

    q = q_ref[0]
    vv = v_scr[...]
    for m in range(ATT_HEADS // KV_HEADS):
        qm = q[:, LANES * m:LANES * (m + 1)]
        og = []
        for g in range(KV_HEADS):
            s = _dot_nt(qm, kg_scr[g])
            mx = jnp.max(s, axis=-1, keepdims=True)
            p = jnp.exp(s - mx)
            l = jnp.sum(p, axis=-1, keepdims=True)
            og.append(_dot(p.astype(BF16), vv) / l)
        lane = lax.broadcasted_iota(jnp.int32, og[0].shape, 1)
        o_ref[0, :, LANES * m:LANES * (m + 1)] = jnp.where(lane < HEAD_DIM, og[0], og[1]).astype(BF16)


def _attention(q, k, v, cache_k=None, cache_v=None):
    b, s, _ = q.shape
    n_ctx = 0 if cache_k is None else cache_k.shape[1]
    tq = min(ATT_Q_TILE, s)
    sk = s + n_ctx
    full = lambda n, w: pl.BlockSpec((1, n, w), lambda i, j: (i, 0, 0))
    in_specs = [pl.BlockSpec((1, tq, W_ATT), lambda i, j: (i, j, 0)), full(s, W_KV), full(s, W_KV)]
    args = [q, k, v]
    if n_ctx:
        in_specs += [full(n_ctx, W_KV), full(n_ctx, W_KV)]
        args += [cache_k, cache_v]
    return pl.pallas_call(
        functools.partial(_attn_kernel, s_self=s, n_ctx=n_ctx),
        grid=(b, s // tq),
        in_specs=in_specs,
        out_specs=pl.BlockSpec((1, tq, W_ATT), lambda i, j: (i, j, 0)),
        out_shape=jax.ShapeDtypeStruct((b, s, W_ATT), BF16),
        scratch_shapes=[pltpu.VMEM((KV_HEADS, sk, W_KV), BF16), pltpu.VMEM((sk, W_KV), BF16)],
        compiler_params=pltpu.CompilerParams(
            dimension_semantics=("parallel", "arbitrary"), vmem_limit_bytes=VMEM_LIMIT),
        name="attention_ctx" if n_ctx else "attention",
    )(*args)


def _postmix_kernel(of_ref, or_ref, ga_ref, oa_ref, x_ref, mod_ref, gn_ref, wout_ref, nffn_ref,
                    wrh_ref, wrl_ref, br_ref, x1_o, h2_o, lp_o, tg_o, meta_o):
    mod = mod_ref[0]
    g1, sh2, sc2 = mod[2:3], mod[3:4], mod[4:5]
    og = of_ref[0] + or_ref[0]
    ga = ga_ref[0]
    parts = []
    for h in range(GLA_HEADS):
        blk = og[:, GLA_DV * h:GLA_DV * (h + 1)]
        ms = jnp.mean(blk * blk, axis=-1, keepdims=True)
        gh = ga[:, GLA_DV * h:GLA_DV * (h + 1)]
        parts.append((blk * lax.rsqrt(ms + RMS_EPS) * gn_ref[...] * (gh * jax.nn.sigmoid(gh))).astype(BF16))
    mix = jnp.concatenate(parts + [oa_ref[0]], axis=1)
    mo = _dot(mix, wout_ref[...])
    x1 = x_ref[0] + g1 * mo
    x1_o[0] = x1
    ms = jnp.mean(x1 * x1, axis=-1, keepdims=True)
    h2 = x1 * lax.rsqrt(ms + RMS_EPS) * nffn_ref[...]
    h2 = h2 * (1.0 + sc2) + sh2
    hh, hl = _split_bf16(h2)
    h2_o[0] = hh

    lt = (_dot_nt(wrh_ref[...], hh) + _dot_nt(wrh_ref[...], hl) + _dot_nt(wrl_ref[...], hh)
          + br_ref[...])
    tm = lt.shape[1]
    eidx = lax.broadcasted_iota(jnp.int32, lt.shape, 0)
    vals, sels = [], []
    for _ in range(TOP_K):
        mx = jnp.max(lt, axis=0, keepdims=True)
        idx = jnp.min(jnp.where(lt == mx, eidx, N_EXPERTS), axis=0, keepdims=True)
        sel = eidx == idx
        lt = jnp.where(sel, -jnp.inf, lt)
        vals.append(mx)
        sels.append(sel)
    ws = [jnp.exp(vv - vals[0]) for vv in vals]
    tot = ws[0] + ws[1] + ws[2] + ws[3]
    tg_o[0] = jnp.concatenate([w / tot for w in ws] + [jnp.zeros((8 - TOP_K, tm), F32)], axis=0)

    onehot = sum(jnp.where(s, 1.0, 0.0) for s in sels)
    ti = lax.broadcasted_iota(jnp.int32, (tm, tm), 0)
    tj = lax.broadcasted_iota(jnp.int32, (tm, tm), 1)
    rank = _dot(onehot.astype(BF16), jnp.where(ti < tj, 1.0, 0.0).astype(BF16))
    cnt = jnp.sum(onehot, axis=1, keepdims=True)
    cnt = jnp.floor((cnt + 7.0) * 0.125) * 8.0
    ei = lax.broadcasted_iota(jnp.int32, (N_EXPERTS, N_EXPERTS), 0)
    ej = lax.broadcasted_iota(jnp.int32, (N_EXPERTS, N_EXPERTS), 1)
    cnt_b = jnp.broadcast_to(cnt, (N_EXPERTS, tm))
    seg = _dot(jnp.where(ej < ei, 1.0, 0.0).astype(BF16), cnt_b.astype(BF16))
    base = seg + rank
    lpos = [jnp.sum(jnp.where(s, base, 0.0), axis=0, keepdims=True).astype(jnp.int32) for s in sels]
    lp_o[0] = jnp.concatenate(lpos + [jnp.zeros((8 - TOP_K, tm), jnp.int32)], axis=0)
    meta_o[0] = jnp.concatenate([cnt_b[:, :LANES], seg[:, :LANES]], axis=0).astype(jnp.int32)


def _postmix(o_f, o_r, ga, o_att, x, mod, mod_shared, wp):
    b, s, d = x.shape
    tm = min(TOKEN_TILE, s)
    nt = s // tm
    const = lambda shape: pl.BlockSpec(shape, lambda i, j: (0,) * len(shape))
    tok = lambda w: pl.BlockSpec((1, tm, w), lambda i, j: (i, j, 0))
    mod_map = (lambda i, j: (0, 0, 0)) if mod_shared else (lambda i, j: (i, 0, 0))
    lane_tok = pl.BlockSpec((1, 8, tm), lambda i, j: (i * nt + j, 0, 0))
    return pl.pallas_call(
        _postmix_kernel,
        grid=(b, nt),
        in_specs=[tok(W_V), tok(W_V), tok(W_V), tok(W_ATT), tok(d), pl.BlockSpec((1, 8, d), mod_map),
                  const((1, GLA_DV)), const((d, d)), const((1, d)),
                  const((N_EXPERTS, d)), const((N_EXPERTS, d)), const((N_EXPERTS, 1))],
        out_specs=[tok(d), tok(d), lane_tok, lane_tok,
                   pl.BlockSpec((1, 2 * N_EXPERTS, LANES), lambda i, j: (i * nt + j, 0, 0))],
        out_shape=[jax.ShapeDtypeStruct((b, s, d), F32), jax.ShapeDtypeStruct((b, s, d), BF16),
                   jax.ShapeDtypeStruct((b * nt, 8, tm), jnp.int32),
                   jax.ShapeDtypeStruct((b * nt, 8, tm), F32),
                   jax.ShapeDtypeStruct((b * nt, 2 * N_EXPERTS, LANES), jnp.int32)],
        compiler_params=pltpu.CompilerParams(
            dimension_semantics=("parallel", "parallel"), vmem_limit_bytes=VMEM_LIMIT),
        name="postmix",
    )(o_f, o_r, ga, o_att, x, mod, wp["gla_norm"], wp["w_out"], wp["norm_ffn"],
      wp["wr_hi"], wp["wr_lo"], wp["b_router"])


SUBLANES = 8
SEG_PIECES = tuple(TOKEN_TILE >> b for b in range(6))
PAIR_ROWS = TOKEN_TILE * TOP_K
LOCAL_ROWS = PAIR_ROWS + N_EXPERTS * SUBLANES
TAB_REM = 3 * N_EXPERTS


def _pieces(n, emit):
    for size in SEG_PIECES:
        done = n & ~(2 * size - 1)
        pl.when((n & size) != 0)(functools.partial(emit, done, size))


def _segment_copies(tab_ref, copy):
    def body(e, carry):
        dst0 = tab_ref[0, 0, e]
        n = tab_ref[0, 0, N_EXPERTS + e]
        src0 = tab_ref[0, 0, 2 * N_EXPERTS + e]
        _pieces(n, lambda done, size: copy(pl.multiple_of(src0 + done, SUBLANES),
                                           pl.multiple_of(dst0 + done, SUBLANES), size))
        return carry
    lax.fori_loop(0, N_EXPERTS, body, 0)


def _wait_rows(rem, wait_piece):
    wait_piece(PAIR_ROWS)
    _pieces(rem, lambda done, size: wait_piece(size))


def _pair_onehot(lp_ref, weights=None):
    tm = lp_ref.shape[-1]
    rows = lax.broadcasted_iota(jnp.int32, (LOCAL_ROWS, tm), 0)
    acc = jnp.zeros((LOCAL_ROWS, tm), F32)
    for r in range(TOP_K):
        w = 1.0 if weights is None else weights[r]
        acc = jnp.where(rows == lp_ref[0, r:r + 1, :], w, acc)
    return acc


def _dispatch_kernel(tab_ref, pad_ref, h_ref, lp_ref, xs_hbm, xl, zbuf, rem_prev, sem, zsem):
    i = pl.program_id(0)
    slot = i % 2
    perm = _pair_onehot(lp_ref).astype(BF16)
    xl[slot] = _dot(perm, h_ref[...])

    def copy(local, glob, size):
        pltpu.make_async_copy(xl.at[slot, pl.ds(local, size)], xs_hbm.at[pl.ds(glob, size)],
                              sem.at[slot]).start()
    _segment_copies(tab_ref, copy)

    def wait_slot(s, rem):
        _wait_rows(rem, lambda size: pltpu.make_async_copy(
            xl.at[s, pl.ds(0, size)], xs_hbm.at[pl.ds(0, size)], sem.at[s]).wait())

    @pl.when(i > 0)
    def _():
        wait_slot(1 - slot, rem_prev[0])
    rem_prev[0] = tab_ref[0, 0, TAB_REM]

    @pl.when(i == pl.num_programs(0) - 1)
    def _():
        wait_slot(slot, tab_ref[0, 0, TAB_REM])
        zbuf[...] = jnp.zeros_like(zbuf)

        def pads(wait):
            def body(e, carry):
                start = pad_ref[0, e]
                def emit(done, size):
                    cp = pltpu.make_async_copy(
                        zbuf.at[pl.ds(0, size)],
                        xs_hbm.at[pl.ds(pl.multiple_of(start + done, SUBLANES), size)], zsem)
                    cp.wait() if wait else cp.start()
                _pieces(pad_ref[0, N_EXPERTS + e], emit)
                return carry
            lax.fori_loop(0, N_EXPERTS, body, 0)
        n_tiles = xs_hbm.shape[0] // EXPERT_TILE

        def tail(wait):
            def body(t, carry):
                cp = pltpu.make_async_copy(
                    zbuf, xs_hbm.at[pl.ds(pl.multiple_of(t * EXPERT_TILE, EXPERT_TILE), EXPERT_TILE)], zsem)
                cp.wait() if wait else cp.start()
                return carry
            lax.fori_loop(pad_ref[0, 2 * N_EXPERTS], n_tiles, body, 0)
        pads(False)
        tail(False)
        pads(True)
        tail(True)


def _dispatch(h2, lpos, table, pad_table, n_rows):
    t, d = h2.shape
    tm = TOKEN_TILE
    nt = t // tm
    return pl.pallas_call(
        _dispatch_kernel,
        grid=(nt,),
        in_specs=[pl.BlockSpec((1, 1, LANES), lambda i: (i, 0, 0), memory_space=pltpu.SMEM),
                  pl.BlockSpec((1, LANES), lambda i: (0, 0), memory_space=pltpu.SMEM),
                  pl.BlockSpec((tm, d), lambda i: (i, 0)),
                  pl.BlockSpec((1, 8, tm), lambda i: (i, 0, 0))],
        out_specs=pl.BlockSpec(memory_space=pl.ANY),
        out_shape=jax.ShapeDtypeStruct((n_rows, d), F32),
        scratch_shapes=[pltpu.VMEM((2, LOCAL_ROWS, d), F32), pltpu.VMEM((EXPERT_TILE, d), F32),
                        pltpu.SMEM((1,), jnp.int32),
                        pltpu.SemaphoreType.DMA((2,)), pltpu.SemaphoreType.DMA(())],
        compiler_params=pltpu.CompilerParams(
            dimension_semantics=("arbitrary",), vmem_limit_bytes=VMEM_LIMIT),
        name="dispatch",
    )(table, pad_table, h2, lpos)


def _experts_kernel(te_ref, nv_ref, xs_ref, wgu_ref, bgu_ref, wd_ref, bd_ref, o_ref, wgu_bf, wd_bf):
    i = pl.program_id(0)
    valid = i < nv_ref[0]
    new_expert = jnp.logical_or(i == 0, te_ref[i] != te_ref[jnp.maximum(i - 1, 0)])

    @pl.when(jnp.logical_and(valid, new_expert))
    def _():
        wgu_bf[...] = wgu_ref[0].astype(BF16)
        wd_bf[...] = wd_ref[0].astype(BF16)

    @pl.when(valid)
    def _():
        x = xs_ref[...].astype(BF16)
        gu = _dot(x, wgu_bf[...]) + bgu_ref[0]
        g = jnp.minimum(gu[:, :EXPERT_FF], SWIGLU_LIMIT)
        u = jnp.clip(gu[:, EXPERT_FF:], -SWIGLU_LIMIT, SWIGLU_LIMIT)
        act = (u + 1.0) * (g * jax.nn.sigmoid(SWIGLU_ALPHA * g))
        o_ref[...] = _dot(act.astype(BF16), wd_bf[...]) + bd_ref[0]

    @pl.when(jnp.logical_not(valid))
    def _():
        o_ref[...] = jnp.zeros_like(o_ref)


def _experts(xs, tile_expert, n_valid, w_gu, b_gu, w_down, b_down):
    n_rows, d = xs.shape
    tm = EXPERT_TILE
    nt = n_rows // tm
    ne, _, ff2 = w_gu.shape
    ff = ff2 // 2
    grid_spec = pltpu.PrefetchScalarGridSpec(
        num_scalar_prefetch=2,
        grid=(nt,),
        in_specs=[pl.BlockSpec((tm, d), lambda i, te, nv: (i, 0)),
                  pl.BlockSpec((1, d, ff2), lambda i, te, nv: (te[i], 0, 0)),
                  pl.BlockSpec((1, 1, ff2), lambda i, te, nv: (te[i], 0, 0)),
                  pl.BlockSpec((1, ff, d), lambda i, te, nv: (te[i], 0, 0)),
                  pl.BlockSpec((1, 1, d), lambda i, te, nv: (te[i], 0, 0))],
        out_specs=pl.BlockSpec((tm, d), lambda i, te, nv: (i, 0)),
        scratch_shapes=[pltpu.VMEM((d, ff2), BF16), pltpu.VMEM((ff, d), BF16)],
    )
    return pl.pallas_call(
        _experts_kernel,
        grid_spec=grid_spec,
        out_shape=jax.ShapeDtypeStruct((n_rows, d), F32),
        compiler_params=pltpu.CompilerParams(
            dimension_semantics=("arbitrary",), vmem_limit_bytes=VMEM_LIMIT),
        name="experts",
    )(tile_expert, n_valid, xs, w_gu, b_gu.reshape(ne, 1, ff2), w_down, b_down.reshape(ne, 1, d))


def _combine_kernel(tab_ref, ys_hbm, lp_ref, tg_ref, x1_ref, mod_ref, fn_ref, y_ref, buf, sem):
    def copy(local, glob, size):
        pltpu.make_async_copy(ys_hbm.at[pl.ds(glob, size)], buf.at[pl.ds(local, size)], sem).start()
    _segment_copies(tab_ref, copy)
    tg = tg_ref[0]
    wh, wl = _split_bf16(_pair_onehot(lp_ref, [tg[r:r + 1, :] for r in range(TOP_K)]))
    _wait_rows(tab_ref[0, 0, TAB_REM], lambda size: pltpu.make_async_copy(
        ys_hbm.at[pl.ds(0, size)], buf.at[pl.ds(0, size)], sem).wait())
    nrow = PAIR_ROWS + tab_ref[0, 0, TAB_REM]
    rowi = lax.broadcasted_iota(jnp.int32, buf.shape, 0)
    yb = jnp.where(rowi < nrow, buf[...], 0.0).astype(BF16)
    y = _dot_tn(wh, yb) + _dot_tn(wl, yb)
    g2 = mod_ref[0][5:6]
    x2 = x1_ref[0] + g2 * y
    ms = jnp.mean(x2 * x2, axis=-1, keepdims=True)
    y_ref[0] = x2 * lax.rsqrt(ms + RMS_EPS) * fn_ref[...]


def _combine(ys, table, lpos, gates, x1, mod, mod_shared, final_norm):
    b, s, d = x1.shape
    tm = TOKEN_TILE
    nt = s // tm
    mod_map = (lambda i, j: (0, 0, 0)) if mod_shared else (lambda i, j: (i, 0, 0))
    tile = lambda shape: pl.BlockSpec(shape, lambda i, j: (i * nt + j, 0, 0))
    return pl.pallas_call(
        _combine_kernel,
        grid=(b, nt),
        in_specs=[pl.BlockSpec((1, 1, LANES), lambda i, j: (i * nt + j, 0, 0), memory_space=pltpu.SMEM),
                  pl.BlockSpec(memory_space=pl.ANY),
                  tile((1, 8, tm)), tile((1, 8, tm)),
                  pl.BlockSpec((1, tm, d), lambda i, j: (i, j, 0)),
                  pl.BlockSpec((1, 8, d), mod_map),
                  pl.BlockSpec((1, d), lambda i, j: (0, 0))],
        out_specs=pl.BlockSpec((1, tm, d), lambda i, j: (i, j, 0)),
        out_shape=jax.ShapeDtypeStruct((b, s, d), F32),
        scratch_shapes=[pltpu.VMEM((LOCAL_ROWS, d), F32), pltpu.SemaphoreType.DMA(())],
        compiler_params=pltpu.CompilerParams(
            dimension_semantics=("arbitrary", "arbitrary"), vmem_limit_bytes=VMEM_LIMIT),
        name="combine",
    )(table, ys, lpos, gates, x1, mod, final_norm.reshape(1, d))


def _route_tables(meta, n_tiles_e):
    tm = EXPERT_TILE
    cnt = meta[:, :N_EXPERTS, 0]
    lstart = meta[:, N_EXPERTS:, 0]
    tot = jnp.sum(cnt, axis=0)
    tiles = (tot + tm - 1) // tm
    tile_end = jnp.cumsum(tiles)
    off = (tile_end - tiles) * tm
    dest = off[None, :] + jnp.cumsum(cnt, axis=0) - cnt
    rem = jnp.sum(cnt, axis=1, keepdims=True) - PAIR_ROWS
    table = jnp.concatenate([dest, cnt, lstart, jnp.broadcast_to(rem, cnt.shape)], axis=1).astype(jnp.int32)
    n_valid = tile_end[-1].astype(jnp.int32).reshape(1)
    pad_table = jnp.concatenate([off + tot, tiles * tm - tot, jnp.broadcast_to(n_valid, (2 * N_EXPERTS,))])
    pad_table = pad_table.astype(jnp.int32).reshape(1, LANES)
    tile_ids = jnp.minimum(jnp.arange(n_tiles_e), n_valid[0] - 1)
    tile_expert = jnp.sum((tile_end[None, :] <= tile_ids[:, None]).astype(jnp.int32), axis=1)
    return table.reshape(-1, 1, LANES), pad_table, tile_expert, n_valid


def _rope_tables(n_tok):
    rows = n_tok // GRID_W
    r, col = jnp.meshgrid(jnp.arange(rows), jnp.arange(GRID_W), indexing="ij")
    r = r.reshape(-1).astype(F32)
    col = col.reshape(-1).astype(F32)
    inv = 1.0 / (ROPE_THETA ** (jnp.arange(ROPE_AXIS_PAIRS, dtype=F32) / ROPE_AXIS_PAIRS))
    ang = jnp.concatenate([r[:, None] * inv, col[:, None] * inv], axis=-1)
    c64 = jnp.repeat(jnp.cos(ang), 2, axis=-1)
    sign = jnp.tile(jnp.array([-1.0, 1.0], F32), HEAD_DIM // 2)
    s64 = jnp.repeat(jnp.sin(ang), 2, axis=-1) * sign
    return (jnp.tile(c64, (1, ATT_HEADS)), jnp.tile(s64, (1, ATT_HEADS)),
            jnp.tile(c64, (1, KV_HEADS)), jnp.tile(s64, (1, KV_HEADS)))


def _prep_weights(w_in, w_gate_f, b_gate_f, w_gate_b, b_gate_b, gla_norm, q_norm, k_norm, w_out,
                  norm_mix, norm_ffn, w_router, b_router):
    d = w_in.shape[0]
    o_lr = 2 * W_QK + 2 * W_V
    o_q = o_lr + 2 * GLA_GATE_RANK
    o_k = o_q + W_ATT
    head_order = np.array([0, 4, 1, 5, 2, 6, 3, 7])
    perm = (head_order[:, None] * HEAD_DIM + np.arange(HEAD_DIM)[None, :]).reshape(-1)
    w_q = w_in[:, o_q:o_k][:, perm]
    lr_pad = jnp.zeros((d, LANES - 2 * GLA_GATE_RANK), w_in.dtype)
    w_in_p = jnp.concatenate([w_in[:, :o_lr], w_q, w_in[:, o_k:], w_in[:, o_lr:o_q], lr_pad], axis=1)
    wg = jnp.zeros((LANES, 2 * W_QK), F32)
    wg = wg.at[:GLA_GATE_RANK, :W_QK].set(w_gate_f)
    wg = wg.at[GLA_GATE_RANK:2 * GLA_GATE_RANK, W_QK:].set(w_gate_b)
    wg_hi, wg_lo = _split_bf16(wg)
    ones_q = jnp.asarray(np.kron(np.eye(ATT_HEADS), np.ones((HEAD_DIM, HEAD_DIM))), BF16)
    ones_k = jnp.asarray(np.kron(np.eye(KV_HEADS), np.ones((HEAD_DIM, HEAD_DIM))), BF16)
    w_out_p = jnp.concatenate([w_out[:W_V], w_out[W_V:][perm]], axis=0).astype(BF16)
    wr_hi, wr_lo = _split_bf16(w_router.T)
    return {
        "norm_mix": norm_mix.reshape(1, d), "w_in": w_in_p.astype(BF16),
        "wg_hi": wg_hi, "wg_lo": wg_lo,
        "b_gate": jnp.concatenate([b_gate_f, b_gate_b]).reshape(1, 2 * W_QK),
        "ones_q": ones_q, "ones_k": ones_k,
        "q_norm": jnp.tile(q_norm, ATT_HEADS).reshape(1, W_ATT),
        "k_norm": jnp.tile(k_norm, KV_HEADS).reshape(1, W_KV),
        "gla_norm": gla_norm.reshape(1, GLA_DV), "w_out": w_out_p,
        "norm_ffn": norm_ffn.reshape(1, d), "wr_hi": wr_hi, "wr_lo": wr_lo,
        "b_router": b_router.reshape(N_EXPERTS, 1),
    }


def _mixer(x, mod, mod_shared, wp, rope, cache_k, cache_v, s0_f, s0_b):
    b, s, _ = x.shape
    qa, ka, va, ga, la_f, la_b, q, k, v, kc, vc = _premix(x, mod, mod_shared, wp, rope)
    o_f, sf = _gla(qa, ka, va, la_f, _state_to_blockdiag_t(s0_f), False)
    o_r, sb = _gla(qa, ka, va, la_b, _state_to_blockdiag_t(s0_b), True)
    o_att = _attention(q, k, v, cache_k, cache_v)
    x1, h2, lpos, gates, meta = _postmix(o_f, o_r, ga, o_att, x, mod, mod_shared, wp)
    return x1, h2, lpos, gates, meta, kc, vc, _blockdiag_t_to_state(sf), _blockdiag_t_to_state(sb)


def kernel(x_prompt, x_sample, c, cache_k, cache_v, state_gla_fwd, state_gla_bwd, c_ctx, w_ada, b_ada,
           norm_mix, w_in, w_gate_f, b_gate_f, w_gate_b, b_gate_b, gla_norm, q_norm, k_norm, w_out,
           norm_ffn, w_router, b_router, w_gu, b_gu, w_down, b_down, final_norm):
    bp, sp, d = x_prompt.shape
    bs, ss, _ = x_sample.shape
    assert w_ada.shape[0] == 1, "single-layer trunk"
    wp = _prep_weights(w_in[0], w_gate_f[0], b_gate_f[0], w_gate_b[0], b_gate_b[0], gla_norm[0],
                       q_norm[0], k_norm[0], w_out[0], norm_mix[0], norm_ffn[0], w_router[0], b_router[0])

    n_cond = -(-(1 + bs) // 8) * 8
    conds = jnp.zeros((n_cond, d), F32).at[0].set(c_ctx).at[1:1 + bs].set(c)
    mod = _adaln(conds, w_ada[0], b_ada[0]).reshape(n_cond, 6, d)
    mod = jnp.concatenate([mod, jnp.zeros((n_cond, 2, d), F32)], axis=1)
    mod_p, mod_s = mod[0:1], mod[1:1 + bs]

    zero_state = jnp.zeros((bp, GLA_HEADS, GLA_DK, GLA_DV), F32)
    x1p, h2p, lpp, tgp, metap, kc, vc, sf, sb = _mixer(x_prompt, mod_p, True, wp, None, None, None,
                                                       zero_state, zero_state)
    n_ctx = cache_k.shape[2]
    x1s, h2s, lps, tgs, metas, _, _, _, _ = _mixer(
        x_sample, mod_s, False, wp, _rope_tables(ss),
        cache_k[:, 0].reshape(bs, n_ctx, W_KV), cache_v[:, 0].reshape(bs, n_ctx, W_KV),
        state_gla_fwd[:, 0].astype(F32), state_gla_bwd[:, 0].astype(F32))

    n_p, n_s = bp * sp, bs * ss
    n_tok = n_p + n_s
    ntp = n_p // TOKEN_TILE
    h2 = jnp.concatenate([h2p.reshape(n_p, d), h2s.reshape(n_s, d)], axis=0)
    lpos = jnp.concatenate([lpp, lps], axis=0)
    worst = n_tok * TOP_K + (n_tok // TOKEN_TILE) * N_EXPERTS * (SUBLANES - 1) + N_EXPERTS * (EXPERT_TILE - SUBLANES)
    n_rows = -(-worst // EXPERT_TILE) * EXPERT_TILE
    table, pad_table, tile_expert, n_valid = _route_tables(
        jnp.concatenate([metap, metas], axis=0), n_rows // EXPERT_TILE)
    xs = _dispatch(h2, lpos, table, pad_table, n_rows)
    ys = _experts(xs, tile_expert, n_valid, w_gu[0], b_gu[0], w_down[0], b_down[0])
    y_prompt = _combine(ys, table[:ntp], lpp, tgp, x1p, mod_p, True, final_norm)
    y_sample = _combine(ys, table[ntp:], lps, tgs, x1s, mod_s, False, final_norm)

    new_cache_k = kc.reshape(bp, 1, sp, KV_HEADS, HEAD_DIM)
    new_cache_v = vc.reshape(bp, 1, sp, KV_HEADS, HEAD_DIM)
    return (y_prompt, y_sample, new_cache_k, new_cache_v, sf[:, None], sb[:, None])
```

```python
import functools

import numpy as np
import jax
import jax.numpy as jnp
from jax import lax
from jax.experimental import pallas as pl
from jax.experimental.pallas import tpu as pltpu

F32 = jnp.float32
BF16 = jnp.bfloat16

D_MODEL = 1024
GRID_W = 64
GLA_HEADS = 4
GLA_DV = 128
GLA_DK = 64
GLA_GATE_RANK = 16
GLA_GATE_NORM = 16.0
HEAD_DIM = 64
ATT_HEADS = 8
KV_HEADS = 2
ROPE_THETA = 10000.0
ROPE_AXIS_PAIRS = HEAD_DIM // 4
N_EXPERTS = 32
TOP_K = 4
EXPERT_FF = 1024
SWIGLU_LIMIT = 7.0
SWIGLU_ALPHA = 1.702
RMS_EPS = 1e-6

W_QK = GLA_HEADS * GLA_DK
W_V = GLA_HEADS * GLA_DV
W_ATT = ATT_HEADS * HEAD_DIM
W_KV = KV_HEADS * HEAD_DIM
LANES = 128
D_IN_PAD = 2 * W_QK + 2 * W_V + W_ATT + 2 * W_KV + LANES

TOKEN_TILE = 256
GLA_BLOCK = 256
GLA_LEVELS = ((256, 32), (32, 8), (8, 1))
ATT_Q_TILE = 256
EXPERT_TILE = 512
VMEM_LIMIT = 56 * 1024 * 1024


def _split_bf16(x):
    hi = x.astype(BF16)
    lo = (x - hi.astype(F32)).astype(BF16)
    return hi, lo


def _dot(a, b):
    return jnp.dot(a, b, preferred_element_type=F32)


def _dot_nt(a, b):
    return lax.dot_general(a, b, (((1,), (1,)), ((), ())), preferred_element_type=F32)


def _dot_tn(a, b):
    return lax.dot_general(a, b, (((0,), (0,)), ((), ())), preferred_element_type=F32)


U32 = jnp.uint32


def _pack_bf16_pairs(x):
    w = x.shape[1] // 2
    bits = lambda t: lax.bitcast_convert_type(t.astype(BF16).astype(F32), U32)
    return (bits(x[:, :w]) >> 16) | (bits(x[:, w:]) & jnp.uint32(0xFFFF0000))


def _unpack_bf16_pairs(w):
    lo = lax.bitcast_convert_type(w << 16, F32)
    hi = lax.bitcast_convert_type(w & jnp.uint32(0xFFFF0000), F32)
    return jnp.concatenate([lo, hi], axis=1).astype(BF16)


def _adaln_kernel(c_ref, w_ref, b_ref, o_ref):
    c = c_ref[...]
    a = c * jax.nn.sigmoid(c)
    ah, al = _split_bf16(a)
    wh, wl = _split_bf16(w_ref[...])
    o_ref[...] = _dot(ah, wh) + _dot(al, wh) + _dot(ah, wl) + b_ref[...]


def _adaln(conds, w_ada, b_ada):
    m, d = conds.shape
    n = w_ada.shape[1]
    tn = 512
    return pl.pallas_call(
        _adaln_kernel,
        grid=(n // tn,),
        in_specs=[pl.BlockSpec((m, d), lambda j: (0, 0)),
                  pl.BlockSpec((d, tn), lambda j: (0, j)),
                  pl.BlockSpec((1, tn), lambda j: (0, j))],
        out_specs=pl.BlockSpec((m, tn), lambda j: (0, j)),
        out_shape=jax.ShapeDtypeStruct((m, n), F32),
        name="adaln",
    )(conds, w_ada, b_ada.reshape(1, n))


def _rope_rotate(x, c, s):
    n = x.shape[-1]
    lane = lax.broadcasted_iota(jnp.int32, x.shape, 1)
    partner = jnp.where((lane & 1) == 0, pltpu.roll(x, n - 1, 1), pltpu.roll(x, 1, 1))
    return x * c + partner * s


def _premix_kernel(*refs, use_rope):
    (x_ref, mod_ref, nmix_ref, win_ref, wgh_ref, wgl_ref, bg_ref, bq_ref, bk_ref,
     qn_ref, kn_ref) = refs[:11]
    n_in = 11
    if use_rope:
        cq_ref, sq_ref, ck_ref, sk_ref = refs[11:15]
        n_in = 15
    (qa_o, ka_o, va_o, ga_o, laf_o, lab_o, q_o, k_o, v_o, kc_o, vc_o) = refs[n_in:]

    x = x_ref[0]
    mod = mod_ref[0]
    sh1 = mod[0:1]
    sc1 = mod[1:2]
    ms = jnp.mean(x * x, axis=-1, keepdims=True)
    h = x * lax.rsqrt(ms + RMS_EPS) * nmix_ref[...]
    h = h * (1.0 + sc1) + sh1
    proj = _dot(h.astype(BF16), win_ref[...])

    o = 0
    qa_o[0] = proj[:, o:o + W_QK] * (GLA_DK ** -0.5); o += W_QK
    ka_o[0] = proj[:, o:o + W_QK]; o += W_QK
    va_o[0] = proj[:, o:o + W_V]; o += W_V
    ga_o[0] = proj[:, o:o + W_V]; o += W_V
    qb = proj[:, o:o + W_ATT]; o += W_ATT
    kb = proj[:, o:o + W_KV]; o += W_KV
    vb = proj[:, o:o + W_KV]; o += W_KV
    lr = proj[:, o:o + LANES]

    lh, ll = _split_bf16(lr)
    xg = _dot(lh, wgh_ref[...]) + _dot(ll, wgh_ref[...]) + _dot(lh, wgl_ref[...]) + bg_ref[...]
    la = (jnp.minimum(xg, 0.0) - jnp.log1p(jnp.exp(-jnp.abs(xg)))) * (1.0 / GLA_GATE_NORM)
    laf_o[0] = la[:, :W_QK]
    lab_o[0] = la[:, W_QK:]

    def head_norm(t, ones_ref, g_ref):
        sh, sl = _split_bf16(t * t)
        msq = (_dot(sh, ones_ref[...]) + _dot(sl, ones_ref[...])) * (1.0 / HEAD_DIM)
        return t * lax.rsqrt(msq + RMS_EPS) * g_ref[...]

    qh = head_norm(qb, bq_ref, qn_ref)
    kh = head_norm(kb, bk_ref, kn_ref)
    kc_o[0] = kh
    vc_o[0] = vb
    if use_rope:
        qh = _rope_rotate(qh, cq_ref[...], sq_ref[...])
        kr = _rope_rotate(kh, ck_ref[...], sk_ref[...])
    else:
        kr = kh
    q_o[0] = (qh * (HEAD_DIM ** -0.5)).astype(BF16)
    k_o[0] = kr.astype(BF16)
    v_o[0] = vb.astype(BF16)


def _premix(x, mod, mod_shared, wp, rope):
    b, s, d = x.shape
    tm = min(TOKEN_TILE, s)
    use_rope = rope is not None
    const = lambda shape: pl.BlockSpec(shape, lambda i, j: (0,) * len(shape))
    tok = lambda w: pl.BlockSpec((1, tm, w), lambda i, j: (i, j, 0))
    mod_map = (lambda i, j: (0, 0, 0)) if mod_shared else (lambda i, j: (i, 0, 0))
    in_specs = [tok(d), pl.BlockSpec((1, 8, d), mod_map), const((1, d)),
                const((d, D_IN_PAD)), const((LANES, 2 * W_QK)), const((LANES, 2 * W_QK)),
                const((1, 2 * W_QK)), const((W_ATT, W_ATT)), const((W_KV, W_KV)),
                const((1, W_ATT)), const((1, W_KV))]
    args = [x, mod, wp["norm_mix"], wp["w_in"], wp["wg_hi"], wp["wg_lo"], wp["b_gate"],
            wp["ones_q"], wp["ones_k"], wp["q_norm"], wp["k_norm"]]
    if use_rope:
        seq = lambda w: pl.BlockSpec((tm, w), lambda i, j: (j, 0))
        in_specs += [seq(W_ATT), seq(W_ATT), seq(W_KV), seq(W_KV)]
        args += list(rope)
    widths = [(W_QK, F32), (W_QK, F32), (W_V, F32), (W_V, F32), (W_QK, F32), (W_QK, F32),
              (W_ATT, BF16), (W_KV, BF16), (W_KV, BF16), (W_KV, F32), (W_KV, F32)]
    return pl.pallas_call(
        functools.partial(_premix_kernel, use_rope=use_rope),
        grid=(b, s // tm),
        in_specs=in_specs,
        out_specs=[tok(w) for w, _ in widths],
        out_shape=[jax.ShapeDtypeStruct((b, s, w), dt) for w, dt in widths],
        compiler_params=pltpu.CompilerParams(
            dimension_semantics=("parallel", "parallel"), vmem_limit_bytes=VMEM_LIMIT),
        name="premix_rope" if use_rope else "premix",
    )(*args)


def _bcast_rows(x, n_par, p_rows, row):
    w = x.shape[-1]
    r = x.reshape(n_par, p_rows, w)[:, row:row + 1, :]
    return jnp.broadcast_to(r, (n_par, p_rows, w)).reshape(n_par * p_rows, w)


def _gla_kernel(q_ref, k_ref, v_ref, la_ref, s0_ref, o_ref, sf_ref, st_scr, *, rows, levels, reverse):
    blk = pl.program_id(1)

    @pl.when(blk == 0)
    def _():
        st_scr[...] = s0_ref[0]

    q = q_ref[0]
    k = k_ref[0]
    la = la_ref[0]
    vb = v_ref[0].astype(BF16)

    ri = lax.broadcasted_iota(jnp.int32, (rows, rows), 0)
    ci = lax.broadcasted_iota(jnp.int32, (rows, rows), 1)
    tri = jnp.where((ri <= ci) if reverse else (ri >= ci), 1.0, 0.0).astype(BF16)
    hi = la.astype(BF16)
    r1 = la - hi.astype(F32)
    mid = r1.astype(BF16)
    lo = (r1 - mid.astype(F32)).astype(BF16)
    cum = _dot(tri, hi) + _dot(tri, mid) + _dot(tri, lo)

    ridx = lax.broadcasted_iota(jnp.int32, (rows, W_QK), 0)
    s_acc = [None] * GLA_HEADS
    for (par, sub) in levels:
        groups = par // sub
        n_par = rows // par
        pi = (ridx % par) // sub
        edge = sub - 1 if reverse else 0
        if sub > 1:
            own = _bcast_rows(cum, rows // sub, sub, edge)
            qt = q * jnp.exp(cum - own)
        else:
            qt = q
        qps, kps = [], []
        for p in range(groups):
            if sub > 1 and p == (groups - 1 if reverse else 0):
                continue
            cp = _bcast_rows(cum, n_par, par, p * sub + edge)
            if reverse:
                km = (pi > p) if sub > 1 else (pi >= p)
            else:
                km = (pi < p) if sub > 1 else (pi <= p)
            ek = jnp.where(km, cp - cum, 0.0)
            kps.append(jnp.where(km, k * jnp.exp(ek), 0.0).astype(BF16))
            qps.append(jnp.where(pi == p, qt, 0.0).astype(BF16))
        ng = len(qps)
        lane_c = lax.broadcasted_iota(jnp.int32, (rows, LANES * ng), 1)
        same_parent = (ri // par) == (ci // par)
        for hp in range(GLA_HEADS // 2):
            qc = jnp.concatenate([a[:, LANES * hp:LANES * (hp + 1)] for a in qps], axis=1)
            kc = jnp.concatenate([a[:, LANES * hp:LANES * (hp + 1)] for a in kps], axis=1)
            for hh in range(2):
                qh = jnp.where(((lane_c % LANES) // GLA_DK) == hh, qc, jnp.zeros_like(qc))
                sl = _dot_nt(qh, kc)
                if par < rows:
                    sl = jnp.where(same_parent, sl, 0.0)
                h = 2 * hp + hh
                s_acc[h] = sl if s_acc[h] is None else s_acc[h] + sl

    q0 = (q * jnp.exp(cum)).astype(BF16)
    far = 0 if reverse else rows - 1
    last = cum[far:far + 1, :]
    kd = (k * jnp.exp(last - cum)).astype(BF16)
    br = lax.broadcasted_iota(jnp.int32, (2 * GLA_DV, 2 * GLA_DK), 0)
    bc = lax.broadcasted_iota(jnp.int32, (2 * GLA_DV, 2 * GLA_DK), 1)
    blockdiag = (br // GLA_DV) == (bc // GLA_DK)
    for hp in range(GLA_HEADS // 2):
        st = st_scr[hp]
        o_inter = _dot_nt(q0[:, LANES * hp:LANES * (hp + 1)], st.astype(BF16))
        o_intra = jnp.concatenate(
            [_dot(s_acc[2 * hp + hh].astype(BF16),
                  vb[:, GLA_DV * (2 * hp + hh):GLA_DV * (2 * hp + hh + 1)]) for hh in range(2)],
            axis=1)
        o_ref[0, :, 2 * GLA_DV * hp:2 * GLA_DV * (hp + 1)] = o_inter + o_intra
        upd = _dot_tn(vb[:, 2 * GLA_DV * hp:2 * GLA_DV * (hp + 1)],
                      kd[:, LANES * hp:LANES * (hp + 1)])
        st_scr[hp] = (jnp.exp(last[:, LANES * hp:LANES * (hp + 1)]) * st
                      + jnp.where(blockdiag, upd, 0.0))

    @pl.when(blk == pl.num_programs(1) - 1)
    def _():
        sf_ref[0] = st_scr[...]


def _gla(q, k, v, la, s0t, reverse):
    n, l, _ = q.shape
    rows = min(GLA_BLOCK, l)
    nb = l // rows
    levels = tuple((min(p, rows), s) for p, s in GLA_LEVELS)
    order = (lambda j: nb - 1 - j) if reverse else (lambda j: j)
    tok = lambda w: pl.BlockSpec((1, rows, w), lambda i, j: (i, order(j), 0))
    st_spec = pl.BlockSpec((1, 2, 2 * GLA_DV, 2 * GLA_DK), lambda i, j: (i, 0, 0, 0))
    return pl.pallas_call(
        functools.partial(_gla_kernel, rows=rows, levels=levels, reverse=reverse),
        grid=(n, nb),
        in_specs=[tok(W_QK), tok(W_QK), tok(W_V), tok(W_QK), st_spec],
        out_specs=[tok(W_V), st_spec],
        out_shape=[jax.ShapeDtypeStruct((n, l, W_V), F32),
                   jax.ShapeDtypeStruct(s0t.shape, F32)],
        scratch_shapes=[pltpu.VMEM((2, 2 * GLA_DV, 2 * GLA_DK), F32)],
        compiler_params=pltpu.CompilerParams(
            dimension_semantics=("parallel", "arbitrary"), vmem_limit_bytes=VMEM_LIMIT),
        name="gla_bwd" if reverse else "gla_fwd",
    )(q, k, v, la, s0t)


def _state_to_blockdiag_t(s):
    n = s.shape[0]
    st = jnp.swapaxes(s, -1, -2).reshape(n, 2, 2, GLA_DV, GLA_DK)
    z = jnp.zeros_like(st[:, :, 0])
    top = jnp.concatenate([st[:, :, 0], z], axis=-1)
    bot = jnp.concatenate([z, st[:, :, 1]], axis=-1)
    return jnp.concatenate([top, bot], axis=-2)


def _blockdiag_t_to_state(sb):
    n = sb.shape[0]
    h0 = sb[:, :, :GLA_DV, :GLA_DK]
    h1 = sb[:, :, GLA_DV:, GLA_DK:]
    st = jnp.stack([h0, h1], axis=2).reshape(n, GLA_HEADS, GLA_DV, GLA_DK)
    return jnp.swapaxes(st, -1, -2)


def _attn_kernel(*refs, s_self, n_ctx):
    if n_ctx:
        q_ref, k_ref, v_ref, ck_ref, cv_ref, o_ref, kg_scr, v_scr = refs
    else:
        q_ref, k_ref, v_ref, o_ref, kg_scr, v_scr = refs

    @pl.when(pl.program_id(1) == 0)
    def _():
        def put(kk, vv, start, n):
            lane = lax.broadcasted_iota(jnp.int32, kk.shape, 1)
            kg_scr[0, start:start + n, :] = jnp.where(lane < HEAD_DIM, kk, jnp.zeros_like(kk))
            kg_scr[1, start:start + n, :] = jnp.where(lane >= HEAD_DIM, kk, jnp.zeros_like(kk))
            v_scr[start:start + n, :] = vv
        put(k_ref[0], v_ref[0], 0, s_self)
        if n_ctx:
            put(ck_ref[0].astype(BF16), cv_ref[0].astype(BF16), s_self, n_ctx)

    q = q_ref[0]
    vv = v_scr[...]
    for m in range(ATT_HEADS // KV_HEADS):
        qm = q[:, LANES * m:LANES * (m + 1)]
        og = []
        for g in range(KV_HEADS):
            s = _dot_nt(qm, kg_scr[g])
            mx = jnp.max(s, axis=-1, keepdims=True)
            p = jnp.exp(s - mx)
            l = jnp.sum(p, axis=-1, keepdims=True)
            og.append(_dot(p.astype(BF16), vv) / l)
        lane = lax.broadcasted_iota(jnp.int32, og[0].shape, 1)
        o_ref[0, :, LANES * m:LANES * (m + 1)] = jnp.where(lane < HEAD_DIM, og[0], og[1]).astype(BF16)


def _attention(q, k, v, cache_k=None, cache_v=None):
    b, s, _ = q.shape
    n_ctx = 0 if cache_k is None else cache_k.shape[1]
    tq = min(ATT_Q_TILE, s)
    sk = s + n_ctx
    full = lambda n, w: pl.BlockSpec((1, n, w), lambda i, j: (i, 0, 0))
    in_specs = [pl.BlockSpec((1, tq, W_ATT), lambda i, j: (i, j, 0)), full(s, W_KV), full(s, W_KV)]
    args = [q, k, v]
    if n_ctx:
        in_specs += [full(n_ctx, W_KV), full(n_ctx, W_KV)]
        args += [cache_k, cache_v]
    return pl.pallas_call(
        functools.partial(_attn_kernel, s_self=s, n_ctx=n_ctx),
        grid=(b, s // tq),
        in_specs=in_specs,
        out_specs=pl.BlockSpec((1, tq, W_ATT), lambda i, j: (i, j, 0)),
        out_shape=jax.ShapeDtypeStruct((b, s, W_ATT), BF16),
        scratch_shapes=[pltpu.VMEM((KV_HEADS, sk, W_KV), BF16), pltpu.VMEM((sk, W_KV), BF16)],
        compiler_params=pltpu.CompilerParams(
            dimension_semantics=("parallel", "arbitrary"), vmem_limit_bytes=VMEM_LIMIT),
        name="attention_ctx" if n_ctx else "attention",
    )(*args)


def _postmix_kernel(of_ref, or_ref, ga_ref, oa_ref, x_ref, mod_ref, gn_ref, wout_ref, nffn_ref,
                    wrh_ref, wrl_ref, br_ref, x1_o, h2_o, lp_o, tg_o, meta_o):
    mod = mod_ref[0]
    g1, sh2, sc2 = mod[2:3], mod[3:4], mod[4:5]
    og = of_ref[0] + or_ref[0]
    ga = ga_ref[0]
    parts = []
    for h in range(GLA_HEADS):
        blk = og[:, GLA_DV * h:GLA_DV * (h + 1)]
        ms = jnp.mean(blk * blk, axis=-1, keepdims=True)
        gh = ga[:, GLA_DV * h:GLA_DV * (h + 1)]
        parts.append((blk * lax.rsqrt(ms + RMS_EPS) * gn_ref[...] * (gh * jax.nn.sigmoid(gh))).astype(BF16))
    mix = jnp.concatenate(parts + [oa_ref[0]], axis=1)
    mo = _dot(mix, wout_ref[...])
    x1 = x_ref[0] + g1 * mo
    x1_o[0] = x1
    ms = jnp.mean(x1 * x1, axis=-1, keepdims=True)
    h2 = x1 * lax.rsqrt(ms + RMS_EPS) * nffn_ref[...]
    h2 = h2 * (1.0 + sc2) + sh2
    hh, hl = _split_bf16(h2)
    h2_o[0] = hh

    lt = (_dot_nt(wrh_ref[...], hh) + _dot_nt(wrh_ref[...], hl) + _dot_nt(wrl_ref[...], hh)
          + br_ref[...])
    tm = lt.shape[1]
    eidx = lax.broadcasted_iota(jnp.int32, lt.shape, 0)
    vals, sels = [], []
    for _ in range(TOP_K):
        mx = jnp.max(lt, axis=0, keepdims=True)
        idx = jnp.min(jnp.where(lt == mx, eidx, N_EXPERTS), axis=0, keepdims=True)
        sel = eidx == idx
        lt = jnp.where(sel, -jnp.inf, lt)
        vals.append(mx)
        sels.append(sel)
    ws = [jnp.exp(vv - vals[0]) for vv in vals]
    tot = ws[0] + ws[1] + ws[2] + ws[3]
    tg_o[0] = jnp.concatenate([w / tot for w in ws] + [jnp.zeros((8 - TOP_K, tm), F32)], axis=0)

    onehot = sum(jnp.where(s, 1.0, 0.0) for s in sels)
    ti = lax.broadcasted_iota(jnp.int32, (tm, tm), 0)
    tj = lax.broadcasted_iota(jnp.int32, (tm, tm), 1)
    rank = _dot(onehot.astype(BF16), jnp.where(ti < tj, 1.0, 0.0).astype(BF16))
    cnt = jnp.sum(onehot, axis=1, keepdims=True)
    cnt = jnp.floor((cnt + 7.0) * 0.125) * 8.0
    ei = lax.broadcasted_iota(jnp.int32, (N_EXPERTS, N_EXPERTS), 0)
    ej = lax.broadcasted_iota(jnp.int32, (N_EXPERTS, N_EXPERTS), 1)
    cnt_b = jnp.broadcast_to(cnt, (N_EXPERTS, tm))
    seg = _dot(jnp.where(ej < ei, 1.0, 0.0).astype(BF16), cnt_b.astype(BF16))
    base = seg + rank
    lpos = [jnp.sum(jnp.where(s, base, 0.0), axis=0, keepdims=True).astype(jnp.int32) for s in sels]
    lp_o[0] = jnp.concatenate(lpos + [jnp.zeros((8 - TOP_K, tm), jnp.int32)], axis=0)
    meta_o[0] = jnp.concatenate([cnt_b[:, :LANES], seg[:, :LANES]], axis=0).astype(jnp.int32)


def _postmix(o_f, o_r, ga, o_att, x, mod, mod_shared, wp):
    b, s, d = x.shape
    tm = min(TOKEN_TILE, s)
    nt = s // tm
    const = lambda shape: pl.BlockSpec(shape, lambda i, j: (0,) * len(shape))
    tok = lambda w: pl.BlockSpec((1, tm, w), lambda i, j: (i, j, 0))
    mod_map = (lambda i, j: (0, 0, 0)) if mod_shared else (lambda i, j: (i, 0, 0))
    lane_tok = pl.BlockSpec((1, 8, tm), lambda i, j: (i * nt + j, 0, 0))
    return pl.pallas_call(
        _postmix_kernel,
        grid=(b, nt),
        in_specs=[tok(W_V), tok(W_V), tok(W_V), tok(W_ATT), tok(d), pl.BlockSpec((1, 8, d), mod_map),
                  const((1, GLA_DV)), const((d, d)), const((1, d)),
                  const((N_EXPERTS, d)), const((N_EXPERTS, d)), const((N_EXPERTS, 1))],
        out_specs=[tok(d), tok(d), lane_tok, lane_tok,
                   pl.BlockSpec((1, 2 * N_EXPERTS, LANES), lambda i, j: (i * nt + j, 0, 0))],
        out_shape=[jax.ShapeDtypeStruct((b, s, d), F32), jax.ShapeDtypeStruct((b, s, d), BF16),
                   jax.ShapeDtypeStruct((b * nt, 8, tm), jnp.int32),
                   jax.ShapeDtypeStruct((b * nt, 8, tm), F32),
                   jax.ShapeDtypeStruct((b * nt, 2 * N_EXPERTS, LANES), jnp.int32)],
        compiler_params=pltpu.CompilerParams(
            dimension_semantics=("parallel", "parallel"), vmem_limit_bytes=VMEM_LIMIT),
        name="postmix",
    )(o_f, o_r, ga, o_att, x, mod, wp["gla_norm"], wp["w_out"], wp["norm_ffn"],
      wp["wr_hi"], wp["wr_lo"], wp["b_router"])


SUBLANES = 8
SEG_PIECES = tuple(TOKEN_TILE >> b for b in range(6))
PAIR_ROWS = TOKEN_TILE * TOP_K
LOCAL_ROWS = PAIR_ROWS + N_EXPERTS * SUBLANES
TAB_REM = 3 * N_EXPERTS


def _pieces(n, emit):
    for size in SEG_PIECES:
        done = n & ~(2 * size - 1)
        pl.when((n & size) != 0)(functools.partial(emit, done, size))


def _segment_copies(tab_ref, copy):
    def body(e, carry):
        dst0 = tab_ref[0, 0, e]
        n = tab_ref[0, 0, N_EXPERTS + e]
        src0 = tab_ref[0, 0, 2 * N_EXPERTS + e]
        _pieces(n, lambda done, size: copy(pl.multiple_of(src0 + done, SUBLANES),
                                           pl.multiple_of(dst0 + done, SUBLANES), size))
        return carry
    lax.fori_loop(0, N_EXPERTS, body, 0)


def _wait_rows(rem, wait_piece):
    wait_piece(PAIR_ROWS)
    _pieces(rem, lambda done, size: wait_piece(size))


def _pair_onehot(lp_ref, weights=None):
    tm = lp_ref.shape[-1]
    rows = lax.broadcasted_iota(jnp.int32, (LOCAL_ROWS, tm), 0)
    acc = jnp.zeros((LOCAL_ROWS, tm), F32)
    for r in range(TOP_K):
        w = 1.0 if weights is None else weights[r]
        acc = jnp.where(rows == lp_ref[0, r:r + 1, :], w, acc)
    return acc


def _dispatch_kernel(tab_ref, pad_ref, h_ref, lp_ref, xs_hbm, xl, zbuf, rem_prev, sem, zsem):
    i = pl.program_id(0)
    slot = i % 2
    perm = _pair_onehot(lp_ref).astype(BF16)
    xl[slot] = _pack_bf16_pairs(_dot(perm, h_ref[...]))

    def copy(local, glob, size):
        pltpu.make_async_copy(xl.at[slot, pl.ds(local, size)], xs_hbm.at[pl.ds(glob, size)],
                              sem.at[slot]).start()
    _segment_copies(tab_ref, copy)

    def wait_slot(s, rem):
        _wait_rows(rem, lambda size: pltpu.make_async_copy(
            xl.at[s, pl.ds(0, size)], xs_hbm.at[pl.ds(0, size)], sem.at[s]).wait())

    @pl.when(i > 0)
    def _():
        wait_slot(1 - slot, rem_prev[0])
    rem_prev[0] = tab_ref[0, 0, TAB_REM]

    @pl.when(i == pl.num_programs(0) - 1)
    def _():
        wait_slot(slot, tab_ref[0, 0, TAB_REM])
        zbuf[...] = jnp.zeros_like(zbuf)

        def pads(wait):
            def body(e, carry):
                start = pad_ref[0, e]
                def emit(done, size):
                    cp = pltpu.make_async_copy(
                        zbuf.at[pl.ds(0, size)],
                        xs_hbm.at[pl.ds(pl.multiple_of(start + done, SUBLANES), size)], zsem)
                    cp.wait() if wait else cp.start()
                _pieces(pad_ref[0, N_EXPERTS + e], emit)
                return carry
            lax.fori_loop(0, N_EXPERTS, body, 0)
        n_tiles = xs_hbm.shape[0] // EXPERT_TILE

        def tail(wait):
            def body(t, carry):
                cp = pltpu.make_async_copy(
                    zbuf, xs_hbm.at[pl.ds(pl.multiple_of(t * EXPERT_TILE, EXPERT_TILE), EXPERT_TILE)], zsem)
                cp.wait() if wait else cp.start()
                return carry
            lax.fori_loop(pad_ref[0, 2 * N_EXPERTS], n_tiles, body, 0)
        pads(False)
        tail(False)
        pads(True)
        tail(True)


def _dispatch(h2, lpos, table, pad_table, n_rows):
    t, d = h2.shape
    tm = TOKEN_TILE
    nt = t // tm
    return pl.pallas_call(
        _dispatch_kernel,
        grid=(nt,),
        in_specs=[pl.BlockSpec((1, 1, LANES), lambda i: (i, 0, 0), memory_space=pltpu.SMEM),
                  pl.BlockSpec((1, LANES), lambda i: (0, 0), memory_space=pltpu.SMEM),
                  pl.BlockSpec((tm, d), lambda i: (i, 0)),
                  pl.BlockSpec((1, 8, tm), lambda i: (i, 0, 0))],
        out_specs=pl.BlockSpec(memory_space=pl.ANY),
        out_shape=jax.ShapeDtypeStruct((n_rows, d // 2), U32),
        scratch_shapes=[pltpu.VMEM((2, LOCAL_ROWS, d // 2), U32), pltpu.VMEM((EXPERT_TILE, d // 2), U32),
                        pltpu.SMEM((1,), jnp.int32),
                        pltpu.SemaphoreType.DMA((2,)), pltpu.SemaphoreType.DMA(())],
        compiler_params=pltpu.CompilerParams(
            dimension_semantics=("arbitrary",), vmem_limit_bytes=VMEM_LIMIT),
        name="dispatch",
    )(table, pad_table, h2, lpos)


def _experts_kernel(te_ref, nv_ref, xs_ref, wgu_ref, bgu_ref, wd_ref, bd_ref, o_ref, wgu_bf, wd_bf):
    i = pl.program_id(0)
    valid = i < nv_ref[0]
    new_expert = jnp.logical_or(i == 0, te_ref[i] != te_ref[jnp.maximum(i - 1, 0)])

    @pl.when(jnp.logical_and(valid, new_expert))
    def _():
        wgu_bf[...] = wgu_ref[0].astype(BF16)
        wd_bf[...] = wd_ref[0].astype(BF16)

    @pl.when(valid)
    def _():
        x = _unpack_bf16_pairs(xs_ref[...])
        gu = _dot(x, wgu_bf[...]) + bgu_ref[0]
        g = jnp.minimum(gu[:, :EXPERT_FF], SWIGLU_LIMIT)
        u = jnp.clip(gu[:, EXPERT_FF:], -SWIGLU_LIMIT, SWIGLU_LIMIT)
        act = (u + 1.0) * (g * jax.nn.sigmoid(SWIGLU_ALPHA * g))
        o_ref[...] = _pack_bf16_pairs(_dot(act.astype(BF16), wd_bf[...]) + bd_ref[0])

    @pl.when(jnp.logical_not(valid))
    def _():
        o_ref[...] = jnp.zeros_like(o_ref)


def _experts(xs, tile_expert, n_valid, w_gu, b_gu, w_down, b_down):
    n_rows = xs.shape[0]
    tm = EXPERT_TILE
    nt = n_rows // tm
    ne, d, ff2 = w_gu.shape
    ff = ff2 // 2
    grid_spec = pltpu.PrefetchScalarGridSpec(
        num_scalar_prefetch=2,
        grid=(nt,),
        in_specs=[pl.BlockSpec((tm, d // 2), lambda i, te, nv: (i, 0)),
                  pl.BlockSpec((1, d, ff2), lambda i, te, nv: (te[i], 0, 0)),
                  pl.BlockSpec((1, 1, ff2), lambda i, te, nv: (te[i], 0, 0)),
                  pl.BlockSpec((1, ff, d), lambda i, te, nv: (te[i], 0, 0)),
                  pl.BlockSpec((1, 1, d), lambda i, te, nv: (te[i], 0, 0))],
        out_specs=pl.BlockSpec((tm, d // 2), lambda i, te, nv: (i, 0)),
        scratch_shapes=[pltpu.VMEM((d, ff2), BF16), pltpu.VMEM((ff, d), BF16)],
    )
    return pl.pallas_call(
        _experts_kernel,
        grid_spec=grid_spec,
        out_shape=jax.ShapeDtypeStruct((n_rows, d // 2), U32),
        compiler_params=pltpu.CompilerParams(
            dimension_semantics=("arbitrary",), vmem_limit_bytes=VMEM_LIMIT),
        name="experts",
    )(tile_expert, n_valid, xs, w_gu, b_gu.reshape(ne, 1, ff2), w_down, b_down.reshape(ne, 1, d))


def _combine_kernel(tab_ref, ys_hbm, lp_ref, tg_ref, x1_ref, mod_ref, fn_ref, y_ref, buf, sem):
    def copy(local, glob, size):
        pltpu.make_async_copy(ys_hbm.at[pl.ds(glob, size)], buf.at[pl.ds(local, size)], sem).start()
    _segment_copies(tab_ref, copy)
    tg = tg_ref[0]
    wh, wl = _split_bf16(_pair_onehot(lp_ref, [tg[r:r + 1, :] for r in range(TOP_K)]))
    _wait_rows(tab_ref[0, 0, TAB_REM], lambda size: pltpu.make_async_copy(
        ys_hbm.at[pl.ds(0, size)], buf.at[pl.ds(0, size)], sem).wait())
    nrow = PAIR_ROWS + tab_ref[0, 0, TAB_REM]
    rowi = lax.broadcasted_iota(jnp.int32, buf.shape, 0)
    yb = _unpack_bf16_pairs(jnp.where(rowi < nrow, buf[...], jnp.uint32(0)))
    y = _dot_tn(wh, yb) + _dot_tn(wl, yb)
    g2 = mod_ref[0][5:6]
    x2 = x1_ref[0] + g2 * y
    ms = jnp.mean(x2 * x2, axis=-1, keepdims=True)
    y_ref[0] = x2 * lax.rsqrt(ms + RMS_EPS) * fn_ref[...]


def _combine(ys, table, lpos, gates, x1, mod, mod_shared, final_norm):
    b, s, d = x1.shape
    tm = TOKEN_TILE
    nt = s // tm
    mod_map = (lambda i, j: (0, 0, 0)) if mod_shared else (lambda i, j: (i, 0, 0))
    tile = lambda shape: pl.BlockSpec(shape, lambda i, j: (i * nt + j, 0, 0))
    return pl.pallas_call(
        _combine_kernel,
        grid=(b, nt),
        in_specs=[pl.BlockSpec((1, 1, LANES), lambda i, j: (i * nt + j, 0, 0), memory_space=pltpu.SMEM),
                  pl.BlockSpec(memory_space=pl.ANY),
                  tile((1, 8, tm)), tile((1, 8, tm)),
                  pl.BlockSpec((1, tm, d), lambda i, j: (i, j, 0)),
                  pl.BlockSpec((1, 8, d), mod_map),
                  pl.BlockSpec((1, d), lambda i, j: (0, 0))],
        out_specs=pl.BlockSpec((1, tm, d), lambda i, j: (i, j, 0)),
        out_shape=jax.ShapeDtypeStruct((b, s, d), F32),
        scratch_shapes=[pltpu.VMEM((LOCAL_ROWS, d // 2), U32), pltpu.SemaphoreType.DMA(())],
        compiler_params=pltpu.CompilerParams(
            dimension_semantics=("arbitrary", "arbitrary"), vmem_limit_bytes=VMEM_LIMIT),
        name="combine",
    )(table, ys, lpos, gates, x1, mod, final_norm.reshape(1, d))


def _route_tables(meta, n_tiles_e):
    tm = EXPERT_TILE
    cnt = meta[:, :N_EXPERTS, 0]
    lstart = meta[:, N_EXPERTS:, 0]
    tot = jnp.sum(cnt, axis=0)
    tiles = (tot + tm - 1) // tm
    tile_end = jnp.cumsum(tiles)
    off = (tile_end - tiles) * tm
    dest = off[None, :] + jnp.cumsum(cnt, axis=0) - cnt
    rem = jnp.sum(cnt, axis=1, keepdims=True) - PAIR_ROWS
    table = jnp.concatenate([dest, cnt, lstart, jnp.broadcast_to(rem, cnt.shape)], axis=1).astype(jnp.int32)
    n_valid = tile_end[-1].astype(jnp.int32).reshape(1)
    pad_table = jnp.concatenate([off + tot, tiles * tm - tot, jnp.broadcast_to(n_valid, (2 * N_EXPERTS,))])
    pad_table = pad_table.astype(jnp.int32).reshape(1, LANES)
    tile_ids = jnp.minimum(jnp.arange(n_tiles_e), n_valid[0] - 1)
    tile_expert = jnp.sum((tile_end[None, :] <= tile_ids[:, None]).astype(jnp.int32), axis=1)
    return table.reshape(-1, 1, LANES), pad_table, tile_expert, n_valid


def _rope_tables(n_tok):
    rows = n_tok // GRID_W
    r, col = jnp.meshgrid(jnp.arange(rows), jnp.arange(GRID_W), indexing="ij")
    r = r.reshape(-1).astype(F32)
    col = col.reshape(-1).astype(F32)
    inv = 1.0 / (ROPE_THETA ** (jnp.arange(ROPE_AXIS_PAIRS, dtype=F32) / ROPE_AXIS_PAIRS))
    ang = jnp.concatenate([r[:, None] * inv, col[:, None] * inv], axis=-1)
    c64 = jnp.repeat(jnp.cos(ang), 2, axis=-1)
    sign = jnp.tile(jnp.array([-1.0, 1.0], F32), HEAD_DIM // 2)
    s64 = jnp.repeat(jnp.sin(ang), 2, axis=-1) * sign
    return (jnp.tile(c64, (1, ATT_HEADS)), jnp.tile(s64, (1, ATT_HEADS)),
            jnp.tile(c64, (1, KV_HEADS)), jnp.tile(s64, (1, KV_HEADS)))


def _prep_weights(w_in, w_gate_f, b_gate_f, w_gate_b, b_gate_b, gla_norm, q_norm, k_norm, w_out,
                  norm_mix, norm_ffn, w_router, b_router):
    d = w_in.shape[0]
    o_lr = 2 * W_QK + 2 * W_V
    o_q = o_lr + 2 * GLA_GATE_RANK
    o_k = o_q + W_ATT
    head_order = np.array([0, 4, 1, 5, 2, 6, 3, 7])
    perm = (head_order[:, None] * HEAD_DIM + np.arange(HEAD_DIM)[None, :]).reshape(-1)
    w_q = w_in[:, o_q:o_k][:, perm]
    lr_pad = jnp.zeros((d, LANES - 2 * GLA_GATE_RANK), w_in.dtype)
    w_in_p = jnp.concatenate([w_in[:, :o_lr], w_q, w_in[:, o_k:], w_in[:, o_lr:o_q], lr_pad], axis=1)
    wg = jnp.zeros((LANES, 2 * W_QK), F32)
    wg = wg.at[:GLA_GATE_RANK, :W_QK].set(w_gate_f)
    wg = wg.at[GLA_GATE_RANK:2 * GLA_GATE_RANK, W_QK:].set(w_gate_b)
    wg_hi, wg_lo = _split_bf16(wg)
    ones_q = jnp.asarray(np.kron(np.eye(ATT_HEADS), np.ones((HEAD_DIM, HEAD_DIM))), BF16)
    ones_k = jnp.asarray(np.kron(np.eye(KV_HEADS), np.ones((HEAD_DIM, HEAD_DIM))), BF16)
    w_out_p = jnp.concatenate([w_out[:W_V], w_out[W_V:][perm]], axis=0).astype(BF16)
    wr_hi, wr_lo = _split_bf16(w_router.T)
    return {
        "norm_mix": norm_mix.reshape(1, d), "w_in": w_in_p.astype(BF16),
        "wg_hi": wg_hi, "wg_lo": wg_lo,
        "b_gate": jnp.concatenate([b_gate_f, b_gate_b]).reshape(1, 2 * W_QK),
        "ones_q": ones_q, "ones_k": ones_k,
        "q_norm": jnp.tile(q_norm, ATT_HEADS).reshape(1, W_ATT),
        "k_norm": jnp.tile(k_norm, KV_HEADS).reshape(1, W_KV),
        "gla_norm": gla_norm.reshape(1, GLA_DV), "w_out": w_out_p,
        "norm_ffn": norm_ffn.reshape(1, d), "wr_hi": wr_hi, "wr_lo": wr_lo,
        "b_router": b_router.reshape(N_EXPERTS, 1),
    }


def _mixer(x, mod, mod_shared, wp, rope, cache_k, cache_v, s0_f, s0_b):
    b, s, _ = x.shape
    qa, ka, va, ga, la_f, la_b, q, k, v, kc, vc = _premix(x, mod, mod_shared, wp, rope)
    o_f, sf = _gla(qa, ka, va, la_f, _state_to_blockdiag_t(s0_f), False)
    o_r, sb = _gla(qa, ka, va, la_b, _state_to_blockdiag_t(s0_b), True)
    o_att = _attention(q, k, v, cache_k, cache_v)
    x1, h2, lpos, gates, meta = _postmix(o_f, o_r, ga, o_att, x, mod, mod_shared, wp)
    return x1, h2, lpos, gates, meta, kc, vc, _blockdiag_t_to_state(sf), _blockdiag_t_to_state(sb)


def kernel(x_prompt, x_sample, c, cache_k, cache_v, state_gla_fwd, state_gla_bwd, c_ctx, w_ada, b_ada,
           norm_mix, w_in, w_gate_f, b_gate_f, w_gate_b, b_gate_b, gla_norm, q_norm, k_norm, w_out,
           norm_ffn, w_router, b_router, w_gu, b_gu, w_down, b_down, final_norm):
    bp, sp, d = x_prompt.shape
    bs, ss, _ = x_sample.shape
    assert w_ada.shape[0] == 1, "single-layer trunk"
    wp = _prep_weights(w_in[0], w_gate_f[0], b_gate_f[0], w_gate_b[0], b_gate_b[0], gla_norm[0],
                       q_norm[0], k_norm[0], w_out[0], norm_mix[0], norm_ffn[0], w_router[0], b_router[0])

    n_cond = -(-(1 + bs) // 8) * 8
    conds = jnp.zeros((n_cond, d), F32).at[0].set(c_ctx).at[1:1 + bs].set(c)
    mod = _adaln(conds, w_ada[0], b_ada[0]).reshape(n_cond, 6, d)
    mod = jnp.concatenate([mod, jnp.zeros((n_cond, 2, d), F32)], axis=1)
    mod_p, mod_s = mod[0:1], mod[1:1 + bs]

    zero_state = jnp.zeros((bp, GLA_HEADS, GLA_DK, GLA_DV), F32)
    x1p, h2p, lpp, tgp, metap, kc, vc, sf, sb = _mixer(x_prompt, mod_p, True, wp, None, None, None,
                                                       zero_state, zero_state)
    n_ctx = cache_k.shape[2]
    x1s, h2s, lps, tgs, metas, _, _, _, _ = _mixer(
        x_sample, mod_s, False, wp, _rope_tables(ss),
        cache_k[:, 0].reshape(bs, n_ctx, W_KV), cache_v[:, 0].reshape(bs, n_ctx, W_KV),
        state_gla_fwd[:, 0].astype(F32), state_gla_bwd[:, 0].astype(F32))

    n_p, n_s = bp * sp, bs * ss
    n_tok = n_p + n_s
    ntp = n_p // TOKEN_TILE
    h2 = jnp.concatenate([h2p.reshape(n_p, d), h2s.reshape(n_s, d)], axis=0)
    lpos = jnp.concatenate([lpp, lps], axis=0)
    worst = n_tok * TOP_K + (n_tok // TOKEN_TILE) * N_EXPERTS * (SUBLANES - 1) + N_EXPERTS * (EXPERT_TILE - SUBLANES)
    n_rows = -(-worst // EXPERT_TILE) * EXPERT_TILE
    table, pad_table, tile_expert, n_valid = _route_tables(
        jnp.concatenate([metap, metas], axis=0), n_rows // EXPERT_TILE)
    xs = _dispatch(h2, lpos, table, pad_table, n_rows)
    ys = _experts(xs, tile_expert, n_valid, w_gu[0], b_gu[0], w_down[0], b_down[0])
    y_prompt = _combine(ys, table[:ntp], lpp, tgp, x1p, mod_p, True, final_norm)
    y_sample = _combine(ys, table[ntp:], lps, tgs, x1s, mod_s, False, final_norm)

    new_cache_k = kc.reshape(bp, 1, sp, KV_HEADS, HEAD_DIM)
    new_cache_v = vc.reshape(bp, 1, sp, KV_HEADS, HEAD_DIM)
    return (y_prompt, y_sample, new_cache_k, new_cache_v, sf[:, None], sb[:, None])
```

```python
import functools

import numpy as np
import jax
import jax.numpy as jnp
from jax import lax
from jax.experimental import pallas as pl
from jax.experimental.pallas import tpu as pltpu

F32 = jnp.float32
BF16 = jnp.bfloat16

D_MODEL = 1024
GRID_W = 64
GLA_HEADS = 4
GLA_DV = 128
GLA_DK = 64
GLA_GATE_RANK = 16
GLA_GATE_NORM = 16.0
HEAD_DIM = 64
ATT_HEADS = 8
KV_HEADS = 2
ROPE_THETA = 10000.0
ROPE_AXIS_PAIRS = HEAD_DIM // 4
N_EXPERTS = 32
TOP_K = 4
EXPERT_FF = 1024
SWIGLU_LIMIT = 7.0
SWIGLU_ALPHA = 1.702
RMS_EPS = 1e-6

W_QK = GLA_HEADS * GLA_DK
W_V = GLA_HEADS * GLA_DV
W_ATT = ATT_HEADS * HEAD_DIM
W_KV = KV_HEADS * HEAD_DIM
LANES = 128
D_IN_PAD = 2 * W_QK + 2 * W_V + W_ATT + 2 * W_KV + LANES

PREMIX_TILE = 512
TOKEN_TILE = 256
GLA_BLOCK = 256
GLA_LEVELS = ((256, 32), (32, 8), (8, 1))
ATT_Q_TILE = 512
EXPERT_TILE = 512
VMEM_LIMIT = 56 * 1024 * 1024


def _split_bf16(x):
    hi = x.astype(BF16)
    lo = (x - hi.astype(F32)).astype(BF16)
    return hi, lo


def _dot(a, b):
    return jnp.dot(a, b, preferred_element_type=F32)


def _dot_nt(a, b):
    return lax.dot_general(a, b, (((1,), (1,)), ((), ())), preferred_element_type=F32)


def _dot_tn(a, b):
    return lax.dot_general(a, b, (((0,), (0,)), ((), ())), preferred_element_type=F32)


U32 = jnp.uint32


def _pack_bf16_pairs(x):
    w = x.shape[1] // 2
    bits = lambda t: lax.bitcast_convert_type(t.astype(BF16).astype(F32), U32)
    return (bits(x[:, :w]) >> 16) | (bits(x[:, w:]) & jnp.uint32(0xFFFF0000))


def _unpack_bf16_pairs(w):
    lo = lax.bitcast_convert_type(w << 16, F32)
    hi = lax.bitcast_convert_type(w & jnp.uint32(0xFFFF0000), F32)
    return jnp.concatenate([lo, hi], axis=1).astype(BF16)


def _adaln_kernel(c_ref, w_ref, b_ref, o_ref):
    c = c_ref[...]
    a = c * jax.nn.sigmoid(c)
    ah, al = _split_bf16(a)
    wh, wl = _split_bf16(w_ref[...])
    o_ref[...] = _dot(ah, wh) + _dot(al, wh) + _dot(ah, wl) + b_ref[...]


def _adaln(conds, w_ada, b_ada):
    m, d = conds.shape
    n = w_ada.shape[1]
    tn = 512
    return pl.pallas_call(
        _adaln_kernel,
        grid=(n // tn,),
        in_specs=[pl.BlockSpec((m, d), lambda j: (0, 0)),
                  pl.BlockSpec((d, tn), lambda j: (0, j)),
                  pl.BlockSpec((1, tn), lambda j: (0, j))],
        out_specs=pl.BlockSpec((m, tn), lambda j: (0, j)),
        out_shape=jax.ShapeDtypeStruct((m, n), F32),
        name="adaln",
    )(conds, w_ada, b_ada.reshape(1, n))


def _rope_rotate(x, c, s):
    n = x.shape[-1]
    lane = lax.broadcasted_iota(jnp.int32, x.shape, 1)
    partner = jnp.where((lane & 1) == 0, pltpu.roll(x, n - 1, 1), pltpu.roll(x, 1, 1))
    return x * c + partner * s


def _premix_kernel(*refs, use_rope):
    (x_ref, mod_ref, nmix_ref, win_ref, wgh_ref, wgl_ref, bg_ref, bq_ref, bk_ref,
     qn_ref, kn_ref) = refs[:11]
    n_in = 11
    if use_rope:
        cq_ref, sq_ref, ck_ref, sk_ref = refs[11:15]
        n_in = 15
    (qa_o, ka_o, va_o, ga_o, laf_o, lab_o, q_o, k_o, v_o, kc_o, vc_o) = refs[n_in:]

    x = x_ref[0]
    mod = mod_ref[0]
    sh1 = mod[0:1]
    sc1 = mod[1:2]
    ms = jnp.mean(x * x, axis=-1, keepdims=True)
    h = x * lax.rsqrt(ms + RMS_EPS) * nmix_ref[...]
    h = h * (1.0 + sc1) + sh1
    proj = _dot(h.astype(BF16), win_ref[...])

    o = 0
    qa_o[0] = proj[:, o:o + W_QK] * (GLA_DK ** -0.5); o += W_QK
    ka_o[0] = proj[:, o:o + W_QK]; o += W_QK
    va_o[0] = proj[:, o:o + W_V]; o += W_V
    ga_o[0] = proj[:, o:o + W_V]; o += W_V
    qb = proj[:, o:o + W_ATT]; o += W_ATT
    kb = proj[:, o:o + W_KV]; o += W_KV
    vb = proj[:, o:o + W_KV]; o += W_KV
    lr = proj[:, o:o + LANES]

    lh, ll = _split_bf16(lr)
    xg = _dot(lh, wgh_ref[...]) + _dot(ll, wgh_ref[...]) + _dot(lh, wgl_ref[...]) + bg_ref[...]
    la = (jnp.minimum(xg, 0.0) - jnp.log1p(jnp.exp(-jnp.abs(xg)))) * (1.0 / GLA_GATE_NORM)
    laf_o[0] = la[:, :W_QK]
    lab_o[0] = la[:, W_QK:]

    def head_norm(t, ones_ref, g_ref):
        sh, sl = _split_bf16(t * t)
        msq = (_dot(sh, ones_ref[...]) + _dot(sl, ones_ref[...])) * (1.0 / HEAD_DIM)
        return t * lax.rsqrt(msq + RMS_EPS) * g_ref[...]

    qh = head_norm(qb, bq_ref, qn_ref)
    kh = head_norm(kb, bk_ref, kn_ref)
    kc_o[0] = kh
    vc_o[0] = vb
    if use_rope:
        qh = _rope_rotate(qh, cq_ref[...], sq_ref[...])
        kr = _rope_rotate(kh, ck_ref[...], sk_ref[...])
    else:
        kr = kh
    q_o[0] = (qh * (HEAD_DIM ** -0.5)).astype(BF16)
    k_o[0] = kr.astype(BF16)
    v_o[0] = vb.astype(BF16)


def _premix(x, mod, mod_shared, wp, rope):
    b, s, d = x.shape
    tm = min(PREMIX_TILE, s)
    use_rope = rope is not None
    const = lambda shape: pl.BlockSpec(shape, lambda i, j: (0,) * len(shape))
    tok = lambda w: pl.BlockSpec((1, tm, w), lambda i, j: (i, j, 0))
    mod_map = (lambda i, j: (0, 0, 0)) if mod_shared else (lambda i, j: (i, 0, 0))
    in_specs = [tok(d), pl.BlockSpec((1, 8, d), mod_map), const((1, d)),
                const((d, D_IN_PAD)), const((LANES, 2 * W_QK)), const((LANES, 2 * W_QK)),
                const((1, 2 * W_QK)), const((W_ATT, W_ATT)), const((W_KV, W_KV)),
                const((1, W_ATT)), const((1, W_KV))]
    args = [x, mod, wp["norm_mix"], wp["w_in"], wp["wg_hi"], wp["wg_lo"], wp["b_gate"],
            wp["ones_q"], wp["ones_k"], wp["q_norm"], wp["k_norm"]]
    if use_rope:
        seq = lambda w: pl.BlockSpec((tm, w), lambda i, j: (j, 0))
        in_specs += [seq(W_ATT), seq(W_ATT), seq(W_KV), seq(W_KV)]
        args += list(rope)
    widths = [(W_QK, F32), (W_QK, F32), (W_V, F32), (W_V, F32), (W_QK, F32), (W_QK, F32),
              (W_ATT, BF16), (W_KV, BF16), (W_KV, BF16), (W_KV, F32), (W_KV, F32)]
    return pl.pallas_call(
        functools.partial(_premix_kernel, use_rope=use_rope),
        grid=(b, s // tm),
        in_specs=in_specs,
        out_specs=[tok(w) for w, _ in widths],
        out_shape=[jax.ShapeDtypeStruct((b, s, w), dt) for w, dt in widths],
        compiler_params=pltpu.CompilerParams(
            dimension_semantics=("parallel", "parallel"), vmem_limit_bytes=VMEM_LIMIT),
        name="premix_rope" if use_rope else "premix",
    )(*args)


def _bcast_rows(x, n_par, p_rows, row):
    w = x.shape[-1]
    r = x.reshape(n_par, p_rows, w)[:, row:row + 1, :]
    return jnp.broadcast_to(r, (n_par, p_rows, w)).reshape(n_par * p_rows, w)


def _gla_kernel(q_ref, k_ref, v_ref, la_ref, s0_ref, o_ref, sf_ref, st_scr, *, rows, levels, reverse):
    blk = pl.program_id(1)

    @pl.when(blk == 0)
    def _():
        st_scr[...] = s0_ref[0]

    q = q_ref[0]
    k = k_ref[0]
    la = la_ref[0]
    vb = v_ref[0].astype(BF16)

    ri = lax.broadcasted_iota(jnp.int32, (rows, rows), 0)
    ci = lax.broadcasted_iota(jnp.int32, (rows, rows), 1)
    tri = jnp.where((ri <= ci) if reverse else (ri >= ci), 1.0, 0.0).astype(BF16)
    hi = la.astype(BF16)
    r1 = la - hi.astype(F32)
    mid = r1.astype(BF16)
    lo = (r1 - mid.astype(F32)).astype(BF16)
    cum = _dot(tri, hi) + _dot(tri, mid) + _dot(tri, lo)

    ridx = lax.broadcasted_iota(jnp.int32, (rows, W_QK), 0)
    s_acc = [None] * GLA_HEADS
    for (par, sub) in levels:
        groups = par // sub
        n_par = rows // par
        pi = (ridx % par) // sub
        edge = sub - 1 if reverse else 0
        if sub > 1:
            own = _bcast_rows(cum, rows // sub, sub, edge)
            qt = q * jnp.exp(cum - own)
        else:
            qt = q
        qps, kps = [], []
        for p in range(groups):
            if sub > 1 and p == (groups - 1 if reverse else 0):
                continue
            cp = _bcast_rows(cum, n_par, par, p * sub + edge)
            if reverse:
                km = (pi > p) if sub > 1 else (pi >= p)
            else:
                km = (pi < p) if sub > 1 else (pi <= p)
            ek = jnp.where(km, cp - cum, 0.0)
            kps.append(jnp.where(km, k * jnp.exp(ek), 0.0).astype(BF16))
            qps.append(jnp.where(pi == p, qt, 0.0).astype(BF16))
        ng = len(qps)
        lane_c = lax.broadcasted_iota(jnp.int32, (rows, LANES * ng), 1)
        same_parent = (ri // par) == (ci // par)
        for hp in range(GLA_HEADS // 2):
            qc = jnp.concatenate([a[:, LANES * hp:LANES * (hp + 1)] for a in qps], axis=1)
            kc = jnp.concatenate([a[:, LANES * hp:LANES * (hp + 1)] for a in kps], axis=1)
            for hh in range(2):
                qh = jnp.where(((lane_c % LANES) // GLA_DK) == hh, qc, jnp.zeros_like(qc))
                sl = _dot_nt(qh, kc)
                if par < rows:
                    sl = jnp.where(same_parent, sl, 0.0)
                h = 2 * hp + hh
                s_acc[h] = sl if s_acc[h] is None else s_acc[h] + sl

    q0 = (q * jnp.exp(cum)).astype(BF16)
    far = 0 if reverse else rows - 1
    last = cum[far:far + 1, :]
    kd = (k * jnp.exp(last - cum)).astype(BF16)
    br = lax.broadcasted_iota(jnp.int32, (2 * GLA_DV, 2 * GLA_DK), 0)
    bc = lax.broadcasted_iota(jnp.int32, (2 * GLA_DV, 2 * GLA_DK), 1)
    blockdiag = (br // GLA_DV) == (bc // GLA_DK)
    for hp in range(GLA_HEADS // 2):
        st = st_scr[hp]
        o_inter = _dot_nt(q0[:, LANES * hp:LANES * (hp + 1)], st.astype(BF16))
        o_intra = jnp.concatenate(
            [_dot(s_acc[2 * hp + hh].astype(BF16),
                  vb[:, GLA_DV * (2 * hp + hh):GLA_DV * (2 * hp + hh + 1)]) for hh in range(2)],
            axis=1)
        o_ref[0, :, 2 * GLA_DV * hp:2 * GLA_DV * (hp + 1)] = o_inter + o_intra
        upd = _dot_tn(vb[:, 2 * GLA_DV * hp:2 * GLA_DV * (hp + 1)],
                      kd[:, LANES * hp:LANES * (hp + 1)])
        st_scr[hp] = (jnp.exp(last[:, LANES * hp:LANES * (hp + 1)]) * st
                      + jnp.where(blockdiag, upd, 0.0))

    @pl.when(blk == pl.num_programs(1) - 1)
    def _():
        sf_ref[0] = st_scr[...]


def _gla(q, k, v, la, s0t, reverse):
    n, l, _ = q.shape
    rows = min(GLA_BLOCK, l)
    nb = l // rows
    levels = tuple((min(p, rows), s) for p, s in GLA_LEVELS)
    order = (lambda j: nb - 1 - j) if reverse else (lambda j: j)
    tok = lambda w: pl.BlockSpec((1, rows, w), lambda i, j: (i, order(j), 0))
    st_spec = pl.BlockSpec((1, 2, 2 * GLA_DV, 2 * GLA_DK), lambda i, j: (i, 0, 0, 0))
    return pl.pallas_call(
        functools.partial(_gla_kernel, rows=rows, levels=levels, reverse=reverse),
        grid=(n, nb),
        in_specs=[tok(W_QK), tok(W_QK), tok(W_V), tok(W_QK), st_spec],
        out_specs=[tok(W_V), st_spec],
        out_shape=[jax.ShapeDtypeStruct((n, l, W_V), F32),
                   jax.ShapeDtypeStruct(s0t.shape, F32)],
        scratch_shapes=[pltpu.VMEM((2, 2 * GLA_DV, 2 * GLA_DK), F32)],
        compiler_params=pltpu.CompilerParams(
            dimension_semantics=("parallel", "arbitrary"), vmem_limit_bytes=VMEM_LIMIT),
        name="gla_bwd" if reverse else "gla_fwd",
    )(q, k, v, la, s0t)


def _state_to_blockdiag_t(s):
    n = s.shape[0]
    st = jnp.swapaxes(s, -1, -2).reshape(n, 2, 2, GLA_DV, GLA_DK)
    z = jnp.zeros_like(st[:, :, 0])
    top = jnp.concatenate([st[:, :, 0], z], axis=-1)
    bot = jnp.concatenate([z, st[:, :, 1]], axis=-1)
    return jnp.concatenate([top, bot], axis=-2)


def _blockdiag_t_to_state(sb):
    n = sb.shape[0]
    h0 = sb[:, :, :GLA_DV, :GLA_DK]
    h1 = sb[:, :, GLA_DV:, GLA_DK:]
    st = jnp.stack([h0, h1], axis=2).reshape(n, GLA_HEADS, GLA_DV, GLA_DK)
    return jnp.swapaxes(st, -1, -2)


def _attn_kernel(*refs, s_self, n_ctx):
    if n_ctx:
        q_ref, k_ref, v_ref, ck_ref, cv_ref, o_ref, kg_scr, v_scr = refs
    else:
        q_ref, k_ref, v_ref, o_ref, kg_scr, v_scr = refs

    @pl.when(pl.program_id(1) == 0)
    def _():
        def put(kk, vv, start, n):
            lane = lax.broadcasted_iota(jnp.int32, kk.shape, 1)
            kg_scr[0, start:start + n, :] = jnp.where(lane < HEAD_DIM, kk, jnp.zeros_like(kk))
            kg_scr[1, start:start + n, :] = jnp.where(lane >= HEAD_DIM, kk, jnp.zeros_like(kk))
            v_scr[start:start + n, :] = vv
        put(k_ref[0], v_ref[0], 0, s_self)
        if n_ctx:
            put(ck_ref[0].astype(BF16), cv_ref[0].astype(BF16), s_self, n_ctx)

    q = q_ref[0]
    vv = v_scr[...]
    for m in range(ATT_HEADS // KV_HEADS):
        qm = q[:, LANES * m:LANES * (m + 1)]
        og = []
        for g in range(KV_HEADS):
            s = _dot_nt(qm, kg_scr[g])
            mx = jnp.max(s, axis=-1, keepdims=True)
            p = jnp.exp(s - mx)
            l = jnp.sum(p, axis=-1, keepdims=True)
            og.append(_dot(p.astype(BF16), vv) / l)
        lane = lax.broadcasted_iota(jnp.int32, og[0].shape, 1)
        o_ref[0, :, LANES * m:LANES * (m + 1)] = jnp.where(lane < HEAD_DIM, og[0], og[1]).astype(BF16)


def _attention(q, k, v, cache_k=None, cache_v=None):
    b, s, _ = q.shape
    n_ctx = 0 if cache_k is None else cache_k.shape[1]
    tq = min(ATT_Q_TILE, s)
    sk = s + n_ctx
    full = lambda n, w: pl.BlockSpec((1, n, w), lambda i, j: (i, 0, 0))
    in_specs = [pl.BlockSpec((1, tq, W_ATT), lambda i, j: (i, j, 0)), full(s, W_KV), full(s, W_KV)]
    args = [q, k, v]
    if n_ctx:
        in_specs += [full(n_ctx, W_KV), full(n_ctx, W_KV)]
        args += [cache_k, cache_v]
    return pl.pallas_call(
        functools.partial(_attn_kernel, s_self=s, n_ctx=n_ctx),
        grid=(b, s // tq),
        in_specs=in_specs,
        out_specs=pl.BlockSpec((1, tq, W_ATT), lambda i, j: (i, j, 0)),
        out_shape=jax.ShapeDtypeStruct((b, s, W_ATT), BF16),
        scratch_shapes=[pltpu.VMEM((KV_HEADS, sk, W_KV), BF16), pltpu.VMEM((sk, W_KV), BF16)],
        compiler_params=pltpu.CompilerParams(
            dimension_semantics=("parallel", "arbitrary"), vmem_limit_bytes=VMEM_LIMIT),
        name="attention_ctx" if n_ctx else "attention",
    )(*args)


def _postmix_kernel(of_ref, or_ref, ga_ref, oa_ref, x_ref, mod_ref, gn_ref, wout_ref, nffn_ref,
                    wrh_ref, wrl_ref, br_ref, x1_o, h2_o, lp_o, tg_o, meta_o):
    mod = mod_ref[0]
    g1, sh2, sc2 = mod[2:3], mod[3:4], mod[4:5]
    og = of_ref[0] + or_ref[0]
    ga = ga_ref[0]
    parts = []
    for h in range(GLA_HEADS):
        blk = og[:, GLA_DV * h:GLA_DV * (h + 1)]
        ms = jnp.mean(blk * blk, axis=-1, keepdims=True)
        gh = ga[:, GLA_DV * h:GLA_DV * (h + 1)]
        parts.append((blk * lax.rsqrt(ms + RMS_EPS) * gn_ref[...] * (gh * jax.nn.sigmoid(gh))).astype(BF16))
    mix = jnp.concatenate(parts + [oa_ref[0]], axis=1)
    mo = _dot(mix, wout_ref[...])
    x1 = x_ref[0] + g1 * mo
    x1_o[0] = x1
    ms = jnp.mean(x1 * x1, axis=-1, keepdims=True)
    h2 = x1 * lax.rsqrt(ms + RMS_EPS) * nffn_ref[...]
    h2 = h2 * (1.0 + sc2) + sh2
    hh, hl = _split_bf16(h2)
    h2_o[0] = hh

    lg = _dot(hh, wrh_ref[...]) + _dot(hl, wrh_ref[...]) + _dot(hh, wrl_ref[...])
    lt = jnp.transpose(lg)[:N_EXPERTS] + br_ref[...]
    tm = lt.shape[1]
    eidx = lax.broadcasted_iota(jnp.int32, lt.shape, 0)
    vals, sels = [], []
    for _ in range(TOP_K):
        mx = jnp.max(lt, axis=0, keepdims=True)
        idx = jnp.min(jnp.where(lt == mx, eidx, N_EXPERTS), axis=0, keepdims=True)
        sel = eidx == idx
        lt = jnp.where(sel, -jnp.inf, lt)
        vals.append(mx)
        sels.append(sel)
    ws = [jnp.exp(vv - vals[0]) for vv in vals]
    tot = ws[0] + ws[1] + ws[2] + ws[3]
    tg_o[0] = jnp.concatenate([w / tot for w in ws] + [jnp.zeros((8 - TOP_K, tm), F32)], axis=0)

    onehot = sum(jnp.where(s, 1.0, 0.0) for s in sels)
    ti = lax.broadcasted_iota(jnp.int32, (tm, tm), 0)
    tj = lax.broadcasted_iota(jnp.int32, (tm, tm), 1)
    rank = _dot(onehot.astype(BF16), jnp.where(ti < tj, 1.0, 0.0).astype(BF16))
    cnt = jnp.sum(onehot, axis=1, keepdims=True)
    cnt = jnp.floor((cnt + 7.0) * 0.125) * 8.0
    ei = lax.broadcasted_iota(jnp.int32, (N_EXPERTS, N_EXPERTS), 0)
    ej = lax.broadcasted_iota(jnp.int32, (N_EXPERTS, N_EXPERTS), 1)
    cnt_b = jnp.broadcast_to(cnt, (N_EXPERTS, tm))
    seg = _dot(jnp.where(ej < ei, 1.0, 0.0).astype(BF16), cnt_b.astype(BF16))
    base = seg + rank
    lpos = [jnp.sum(jnp.where(s, base, 0.0), axis=0, keepdims=True).astype(jnp.int32) for s in sels]
    lp_o[0] = jnp.concatenate(lpos + [jnp.zeros((8 - TOP_K, tm), jnp.int32)], axis=0)
    meta_o[0] = jnp.concatenate([cnt_b[:, :LANES], seg[:, :LANES]], axis=0).astype(jnp.int32)


def _postmix(o_f, o_r, ga, o_att, x, mod, mod_shared, wp):
    b, s, d = x.shape
    tm = min(TOKEN_TILE, s)
    nt = s // tm
    const = lambda shape: pl.BlockSpec(shape, lambda i, j: (0,) * len(shape))
    tok = lambda w: pl.BlockSpec((1, tm, w), lambda i, j: (i, j, 0))
    mod_map = (lambda i, j: (0, 0, 0)) if mod_shared else (lambda i, j: (i, 0, 0))
    lane_tok = pl.BlockSpec((1, 8, tm), lambda i, j: (i * nt + j, 0, 0))
    return pl.pallas_call(
        _postmix_kernel,
        grid=(b, nt),
        in_specs=[tok(W_V), tok(W_V), tok(W_V), tok(W_ATT), tok(d), pl.BlockSpec((1, 8, d), mod_map),
                  const((1, GLA_DV)), const((d, d)), const((1, d)),
                  const((d, LANES)), const((d, LANES)), const((N_EXPERTS, 1))],
        out_specs=[tok(d), tok(d), lane_tok, lane_tok,
                   pl.BlockSpec((1, 2 * N_EXPERTS, LANES), lambda i, j: (i * nt + j, 0, 0))],
        out_shape=[jax.ShapeDtypeStruct((b, s, d), F32), jax.ShapeDtypeStruct((b, s, d), BF16),
                   jax.ShapeDtypeStruct((b * nt, 8, tm), jnp.int32),
                   jax.ShapeDtypeStruct((b * nt, 8, tm), F32),
                   jax.ShapeDtypeStruct((b * nt, 2 * N_EXPERTS, LANES), jnp.int32)],
        compiler_params=pltpu.CompilerParams(
            dimension_semantics=("parallel", "parallel"), vmem_limit_bytes=VMEM_LIMIT),
        name="postmix",
    )(o_f, o_r, ga, o_att, x, mod, wp["gla_norm"], wp["w_out"], wp["norm_ffn"],
      wp["wr_hi"], wp["wr_lo"], wp["b_router"])


SUBLANES = 8
SEG_PIECES = tuple(TOKEN_TILE >> b for b in range(6))
PAIR_ROWS = TOKEN_TILE * TOP_K
LOCAL_ROWS = PAIR_ROWS + N_EXPERTS * SUBLANES
TAB_REM = 3 * N_EXPERTS


def _pieces(n, emit):
    for size in SEG_PIECES:
        done = n & ~(2 * size - 1)
        pl.when((n & size) != 0)(functools.partial(emit, done, size))


def _segment_copies(tab_ref, copy):
    def body(e, carry):
        dst0 = tab_ref[0, 0, e]
        n = tab_ref[0, 0, N_EXPERTS + e]
        src0 = tab_ref[0, 0, 2 * N_EXPERTS + e]
        _pieces(n, lambda done, size: copy(pl.multiple_of(src0 + done, SUBLANES),
                                           pl.multiple_of(dst0 + done, SUBLANES), size))
        return carry
    lax.fori_loop(0, N_EXPERTS, body, 0)


def _wait_rows(rem, wait_piece):
    wait_piece(PAIR_ROWS)
    _pieces(rem, lambda done, size: wait_piece(size))


def _pair_onehot(lp_ref, weights=None):
    tm = lp_ref.shape[-1]
    rows = lax.broadcasted_iota(jnp.int32, (LOCAL_ROWS, tm), 0)
    acc = jnp.zeros((LOCAL_ROWS, tm), F32)
    for r in range(TOP_K):
        w = 1.0 if weights is None else weights[r]
        acc = jnp.where(rows == lp_ref[0, r:r + 1, :], w, acc)
    return acc


def _dispatch_kernel(tab_ref, pad_ref, h_ref, lp_ref, xs_hbm, xl, zbuf, rem_prev, sem, zsem):
    i = pl.program_id(0)
    slot = i % 2
    perm = _pair_onehot(lp_ref).astype(BF16)
    xl[slot] = _pack_bf16_pairs(_dot(perm, h_ref[...]))

    def copy(local, glob, size):
        pltpu.make_async_copy(xl.at[slot, pl.ds(local, size)], xs_hbm.at[pl.ds(glob, size)],
                              sem.at[slot]).start()
    _segment_copies(tab_ref, copy)

    def wait_slot(s, rem):
        _wait_rows(rem, lambda size: pltpu.make_async_copy(
            xl.at[s, pl.ds(0, size)], xs_hbm.at[pl.ds(0, size)], sem.at[s]).wait())

    @pl.when(i > 0)
    def _():
        wait_slot(1 - slot, rem_prev[0])
    rem_prev[0] = tab_ref[0, 0, TAB_REM]

    @pl.when(i == pl.num_programs(0) - 1)
    def _():
        wait_slot(slot, tab_ref[0, 0, TAB_REM])
        zbuf[...] = jnp.zeros_like(zbuf)

        def pads(wait):
            def body(e, carry):
                start = pad_ref[0, e]
                def emit(done, size):
                    cp = pltpu.make_async_copy(
                        zbuf.at[pl.ds(0, size)],
                        xs_hbm.at[pl.ds(pl.multiple_of(start + done, SUBLANES), size)], zsem)
                    cp.wait() if wait else cp.start()
                _pieces(pad_ref[0, N_EXPERTS + e], emit)
                return carry
            lax.fori_loop(0, N_EXPERTS, body, 0)
        n_tiles = xs_hbm.shape[0] // EXPERT_TILE

        def tail(wait):
            def body(t, carry):
                cp = pltpu.make_async_copy(
                    zbuf, xs_hbm.at[pl.ds(pl.multiple_of(t * EXPERT_TILE, EXPERT_TILE), EXPERT_TILE)], zsem)
                cp.wait() if wait else cp.start()
                return carry
            lax.fori_loop(pad_ref[0, 2 * N_EXPERTS], n_tiles, body, 0)
        pads(False)
        tail(False)
        pads(True)
        tail(True)


def _dispatch(h2, lpos, table, pad_table, n_rows):
    t, d = h2.shape
    tm = TOKEN_TILE
    nt = t // tm
    return pl.pallas_call(
        _dispatch_kernel,
        grid=(nt,),
        in_specs=[pl.BlockSpec((1, 1, LANES), lambda i: (i, 0, 0), memory_space=pltpu.SMEM),
                  pl.BlockSpec((1, LANES), lambda i: (0, 0), memory_space=pltpu.SMEM),
                  pl.BlockSpec((tm, d), lambda i: (i, 0)),
                  pl.BlockSpec((1, 8, tm), lambda i: (i, 0, 0))],
        out_specs=pl.BlockSpec(memory_space=pl.ANY),
        out_shape=jax.ShapeDtypeStruct((n_rows, d // 2), U32),
        scratch_shapes=[pltpu.VMEM((2, LOCAL_ROWS, d // 2), U32), pltpu.VMEM((EXPERT_TILE, d // 2), U32),
                        pltpu.SMEM((1,), jnp.int32),
                        pltpu.SemaphoreType.DMA((2,)), pltpu.SemaphoreType.DMA(())],
        compiler_params=pltpu.CompilerParams(
            dimension_semantics=("arbitrary",), vmem_limit_bytes=VMEM_LIMIT),
        name="dispatch",
    )(table, pad_table, h2, lpos)


def _experts_kernel(te_ref, nv_ref, xs_ref, wgu_ref, bgu_ref, wd_ref, bd_ref, o_ref, wgu_bf, wd_bf):
    i = pl.program_id(0)
    valid = i < nv_ref[0]
    new_expert = jnp.logical_or(i == 0, te_ref[i] != te_ref[jnp.maximum(i - 1, 0)])

    @pl.when(jnp.logical_and(valid, new_expert))
    def _():
        wgu_bf[...] = wgu_ref[0].astype(BF16)
        wd_bf[...] = wd_ref[0].astype(BF16)

    @pl.when(valid)
    def _():
        x = _unpack_bf16_pairs(xs_ref[...])
        gu = _dot(x, wgu_bf[...]) + bgu_ref[0]
        g = jnp.minimum(gu[:, :EXPERT_FF], SWIGLU_LIMIT)
        u = jnp.clip(gu[:, EXPERT_FF:], -SWIGLU_LIMIT, SWIGLU_LIMIT)
        act = (u + 1.0) * (g * jax.nn.sigmoid(SWIGLU_ALPHA * g))
        o_ref[...] = _pack_bf16_pairs(_dot(act.astype(BF16), wd_bf[...]) + bd_ref[0])

    @pl.when(jnp.logical_not(valid))
    def _():
        o_ref[...] = jnp.zeros_like(o_ref)


def _experts(xs, tile_expert, n_valid, w_gu, b_gu, w_down, b_down):
    n_rows = xs.shape[0]
    tm = EXPERT_TILE
    nt = n_rows // tm
    ne, d, ff2 = w_gu.shape
    ff = ff2 // 2
    grid_spec = pltpu.PrefetchScalarGridSpec(
        num_scalar_prefetch=2,
        grid=(nt,),
        in_specs=[pl.BlockSpec((tm, d // 2), lambda i, te, nv: (i, 0)),
                  pl.BlockSpec((1, d, ff2), lambda i, te, nv: (te[i], 0, 0)),
                  pl.BlockSpec((1, 1, ff2), lambda i, te, nv: (te[i], 0, 0)),
                  pl.BlockSpec((1, ff, d), lambda i, te, nv: (te[i], 0, 0)),
                  pl.BlockSpec((1, 1, d), lambda i, te, nv: (te[i], 0, 0))],
        out_specs=pl.BlockSpec((tm, d // 2), lambda i, te, nv: (i, 0)),
        scratch_shapes=[pltpu.VMEM((d, ff2), BF16), pltpu.VMEM((ff, d), BF16)],
    )
    return pl.pallas_call(
        _experts_kernel,
        grid_spec=grid_spec,
        out_shape=jax.ShapeDtypeStruct((n_rows, d // 2), U32),
        compiler_params=pltpu.CompilerParams(
            dimension_semantics=("arbitrary",), vmem_limit_bytes=VMEM_LIMIT),
        name="experts",
    )(tile_expert, n_valid, xs, w_gu, b_gu.reshape(ne, 1, ff2), w_down, b_down.reshape(ne, 1, d))


def _combine_kernel(tab_ref, ys_hbm, lp_ref, tg_ref, x1_ref, mod_ref, fn_ref, y_ref, buf, sem):
    def copy(local, glob, size):
        pltpu.make_async_copy(ys_hbm.at[pl.ds(glob, size)], buf.at[pl.ds(local, size)], sem).start()
    _segment_copies(tab_ref, copy)
    tg = tg_ref[0]
    wh, wl = _split_bf16(_pair_onehot(lp_ref, [tg[r:r + 1, :] for r in range(TOP_K)]))
    _wait_rows(tab_ref[0, 0, TAB_REM], lambda size: pltpu.make_async_copy(
        ys_hbm.at[pl.ds(0, size)], buf.at[pl.ds(0, size)], sem).wait())
    nrow = PAIR_ROWS + tab_ref[0, 0, TAB_REM]
    rowi = lax.broadcasted_iota(jnp.int32, buf.shape, 0)
    yb = _unpack_bf16_pairs(jnp.where(rowi < nrow, buf[...], jnp.uint32(0)))
    y = _dot_tn(wh, yb) + _dot_tn(wl, yb)
    g2 = mod_ref[0][5:6]
    x2 = x1_ref[0] + g2 * y
    ms = jnp.mean(x2 * x2, axis=-1, keepdims=True)
    y_ref[0] = x2 * lax.rsqrt(ms + RMS_EPS) * fn_ref[...]


def _combine(ys, table, lpos, gates, x1, mod, mod_shared, final_norm):
    b, s, d = x1.shape
    tm = TOKEN_TILE
    nt = s // tm
    mod_map = (lambda i, j: (0, 0, 0)) if mod_shared else (lambda i, j: (i, 0, 0))
    tile = lambda shape: pl.BlockSpec(shape, lambda i, j: (i * nt + j, 0, 0))
    return pl.pallas_call(
        _combine_kernel,
        grid=(b, nt),
        in_specs=[pl.BlockSpec((1, 1, LANES), lambda i, j: (i * nt + j, 0, 0), memory_space=pltpu.SMEM),
                  pl.BlockSpec(memory_space=pl.ANY),
                  tile((1, 8, tm)), tile((1, 8, tm)),
                  pl.BlockSpec((1, tm, d), lambda i, j: (i, j, 0)),
                  pl.BlockSpec((1, 8, d), mod_map),
                  pl.BlockSpec((1, d), lambda i, j: (0, 0))],
        out_specs=pl.BlockSpec((1, tm, d), lambda i, j: (i, j, 0)),
        out_shape=jax.ShapeDtypeStruct((b, s, d), F32),
        scratch_shapes=[pltpu.VMEM((LOCAL_ROWS, d // 2), U32), pltpu.SemaphoreType.DMA(())],
        compiler_params=pltpu.CompilerParams(
            dimension_semantics=("arbitrary", "arbitrary"), vmem_limit_bytes=VMEM_LIMIT),
        name="combine",
    )(table, ys, lpos, gates, x1, mod, final_norm.reshape(1, d))


def _route_tables(meta, n_tiles_e):
    tm = EXPERT_TILE
    cnt = meta[:, :N_EXPERTS, 0]
    lstart = meta[:, N_EXPERTS:, 0]
    tot = jnp.sum(cnt, axis=0)
    tiles = (tot + tm - 1) // tm
    tile_end = jnp.cumsum(tiles)
    off = (tile_end - tiles) * tm
    dest = off[None, :] + jnp.cumsum(cnt, axis=0) - cnt
    rem = jnp.sum(cnt, axis=1, keepdims=True) - PAIR_ROWS
    table = jnp.concatenate([dest, cnt, lstart, jnp.broadcast_to(rem, cnt.shape)], axis=1).astype(jnp.int32)
    n_valid = tile_end[-1].astype(jnp.int32).reshape(1)
    pad_table = jnp.concatenate([off + tot, tiles * tm - tot, jnp.broadcast_to(n_valid, (2 * N_EXPERTS,))])
    pad_table = pad_table.astype(jnp.int32).reshape(1, LANES)
    tile_ids = jnp.minimum(jnp.arange(n_tiles_e), n_valid[0] - 1)
    tile_expert = jnp.sum((tile_end[None, :] <= tile_ids[:, None]).astype(jnp.int32), axis=1)
    return table.reshape(-1, 1, LANES), pad_table, tile_expert, n_valid


def _rope_tables(n_tok):
    rows = n_tok // GRID_W
    r, col = jnp.meshgrid(jnp.arange(rows), jnp.arange(GRID_W), indexing="ij")
    r = r.reshape(-1).astype(F32)
    col = col.reshape(-1).astype(F32)
    inv = 1.0 / (ROPE_THETA ** (jnp.arange(ROPE_AXIS_PAIRS, dtype=F32) / ROPE_AXIS_PAIRS))
    ang = jnp.concatenate([r[:, None] * inv, col[:, None] * inv], axis=-1)
    c64 = jnp.repeat(jnp.cos(ang), 2, axis=-1)
    sign = jnp.tile(jnp.array([-1.0, 1.0], F32), HEAD_DIM // 2)
    s64 = jnp.repeat(jnp.sin(ang), 2, axis=-1) * sign
    return (jnp.tile(c64, (1, ATT_HEADS)), jnp.tile(s64, (1, ATT_HEADS)),
            jnp.tile(c64, (1, KV_HEADS)), jnp.tile(s64, (1, KV_HEADS)))


def _prep_weights(w_in, w_gate_f, b_gate_f, w_gate_b, b_gate_b, gla_norm, q_norm, k_norm, w_out,
                  norm_mix, norm_ffn, w_router, b_router):
    d = w_in.shape[0]
    o_lr = 2 * W_QK + 2 * W_V
    o_q = o_lr + 2 * GLA_GATE_RANK
    o_k = o_q + W_ATT
    head_order = np.array([0, 4, 1, 5, 2, 6, 3, 7])
    perm = (head_order[:, None] * HEAD_DIM + np.arange(HEAD_DIM)[None, :]).reshape(-1)
    w_q = w_in[:, o_q:o_k][:, perm]
    lr_pad = jnp.zeros((d, LANES - 2 * GLA_GATE_RANK), w_in.dtype)
    w_in_p = jnp.concatenate([w_in[:, :o_lr], w_q, w_in[:, o_k:], w_in[:, o_lr:o_q], lr_pad], axis=1)
    wg = jnp.zeros((LANES, 2 * W_QK), F32)
    wg = wg.at[:GLA_GATE_RANK, :W_QK].set(w_gate_f)
    wg = wg.at[GLA_GATE_RANK:2 * GLA_GATE_RANK, W_QK:].set(w_gate_b)
    wg_hi, wg_lo = _split_bf16(wg)
    ones_q = jnp.asarray(np.kron(np.eye(ATT_HEADS), np.ones((HEAD_DIM, HEAD_DIM))), BF16)
    ones_k = jnp.asarray(np.kron(np.eye(KV_HEADS), np.ones((HEAD_DIM, HEAD_DIM))), BF16)
    w_out_p = jnp.concatenate([w_out[:W_V], w_out[W_V:][perm]], axis=0).astype(BF16)
    wr_hi, wr_lo = _split_bf16(jnp.pad(w_router, ((0, 0), (0, LANES - N_EXPERTS))))
    return {
        "norm_mix": norm_mix.reshape(1, d), "w_in": w_in_p.astype(BF16),
        "wg_hi": wg_hi, "wg_lo": wg_lo,
        "b_gate": jnp.concatenate([b_gate_f, b_gate_b]).reshape(1, 2 * W_QK),
        "ones_q": ones_q, "ones_k": ones_k,
        "q_norm": jnp.tile(q_norm, ATT_HEADS).reshape(1, W_ATT),
        "k_norm": jnp.tile(k_norm, KV_HEADS).reshape(1, W_KV),
        "gla_norm": gla_norm.reshape(1, GLA_DV), "w_out": w_out_p,
        "norm_ffn": norm_ffn.reshape(1, d), "wr_hi": wr_hi, "wr_lo": wr_lo,
        "b_router": b_router.reshape(N_EXPERTS, 1),
    }


def _mixer(x, mod, mod_shared, wp, rope, cache_k, cache_v, s0_f, s0_b):
    b, s, _ = x.shape
    qa, ka, va, ga, la_f, la_b, q, k, v, kc, vc = _premix(x, mod, mod_shared, wp, rope)
    o_f, sf = _gla(qa, ka, va, la_f, _state_to_blockdiag_t(s0_f), False)
    o_r, sb = _gla(qa, ka, va, la_b, _state_to_blockdiag_t(s0_b), True)
    o_att = _attention(q, k, v, cache_k, cache_v)
    x1, h2, lpos, gates, meta = _postmix(o_f, o_r, ga, o_att, x, mod, mod_shared, wp)
    return x1, h2, lpos, gates, meta, kc, vc, _blockdiag_t_to_state(sf), _blockdiag_t_to_state(sb)


def kernel(x_prompt, x_sample, c, cache_k, cache_v, state_gla_fwd, state_gla_bwd, c_ctx, w_ada, b_ada,
           norm_mix, w_in, w_gate_f, b_gate_f, w_gate_b, b_gate_b, gla_norm, q_norm, k_norm, w_out,
           norm_ffn, w_router, b_router, w_gu, b_gu, w_down, b_down, final_norm):
    bp, sp, d = x_prompt.shape
    bs, ss, _ = x_sample.shape
    assert w_ada.shape[0] == 1, "single-layer trunk"
    wp = _prep_weights(w_in[0], w_gate_f[0], b_gate_f[0], w_gate_b[0], b_gate_b[0], gla_norm[0],
                       q_norm[0], k_norm[0], w_out[0], norm_mix[0], norm_ffn[0], w_router[0], b_router[0])

    n_cond = -(-(1 + bs) // 8) * 8
    conds = jnp.zeros((n_cond, d), F32).at[0].set(c_ctx).at[1:1 + bs].set(c)
    mod = _adaln(conds, w_ada[0], b_ada[0]).reshape(n_cond, 6, d)
    mod = jnp.concatenate([mod, jnp.zeros((n_cond, 2, d), F32)], axis=1)
    mod_p, mod_s = mod[0:1], mod[1:1 + bs]

    zero_state = jnp.zeros((bp, GLA_HEADS, GLA_DK, GLA_DV), F32)
    x1p, h2p, lpp, tgp, metap, kc, vc, sf, sb = _mixer(x_prompt, mod_p, True, wp, None, None, None,
                                                       zero_state, zero_state)
    n_ctx = cache_k.shape[2]
    x1s, h2s, lps, tgs, metas, _, _, _, _ = _mixer(
        x_sample, mod_s, False, wp, _rope_tables(ss),
        cache_k[:, 0].reshape(bs, n_ctx, W_KV), cache_v[:, 0].reshape(bs, n_ctx, W_KV),
        state_gla_fwd[:, 0].astype(F32), state_gla_bwd[:, 0].astype(F32))

    n_p, n_s = bp * sp, bs * ss
    n_tok = n_p + n_s
    ntp = n_p // TOKEN_TILE
    h2 = jnp.concatenate([h2p.reshape(n_p, d), h2s.reshape(n_s, d)], axis=0)
    lpos = jnp.concatenate([lpp, lps], axis=0)
    worst = n_tok * TOP_K + (n_tok // TOKEN_TILE) * N_EXPERTS * (SUBLANES - 1) + N_EXPERTS * (EXPERT_TILE - SUBLANES)
    n_rows = -(-worst // EXPERT_TILE) * EXPERT_TILE
    table, pad_table, tile_expert, n_valid = _route_tables(
        jnp.concatenate([metap, metas], axis=0), n_rows // EXPERT_TILE)
    xs = _dispatch(h2, lpos, table, pad_table, n_rows)
    ys = _experts(xs, tile_expert, n_valid, w_gu[0], b_gu[0], w_down[0], b_down[0])
    y_prompt = _combine(ys, table[:ntp], lpp, tgp, x1p, mod_p, True, final_norm)
    y_sample = _combine(ys, table[ntp:], lps, tgs, x1s, mod_s, False, final_norm)

    new_cache_k = kc.reshape(bp, 1, sp, KV_HEADS, HEAD_DIM)
    new_cache_v = vc.reshape(bp, 1, sp, KV_HEADS, HEAD_DIM)
    return (y_prompt, y_sample, new_cache_k, new_cache_v, sf[:, None], sb[:, None])
```

```python
import functools

import numpy as np
import jax
import jax.numpy as jnp
from jax import lax
from jax.experimental import pallas as pl
from jax.experimental.pallas import tpu as pltpu

F32 = jnp.float32
BF16 = jnp.bfloat16

D_MODEL = 1024
GRID_W = 64
GLA_HEADS = 4
GLA_DV = 128
GLA_DK = 64
GLA_GATE_RANK = 16
GLA_GATE_NORM = 16.0
HEAD_DIM = 64
ATT_HEADS = 8
KV_HEADS = 2
ROPE_THETA = 10000.0
ROPE_AXIS_PAIRS = HEAD_DIM // 4
N_EXPERTS = 32
TOP_K = 4
EXPERT_FF = 1024
SWIGLU_LIMIT = 7.0
SWIGLU_ALPHA = 1.702
RMS_EPS = 1e-6

W_QK = GLA_HEADS * GLA_DK
W_V = GLA_HEADS * GLA_DV
W_ATT = ATT_HEADS * HEAD_DIM
W_KV = KV_HEADS * HEAD_DIM
LANES = 128
D_IN_PAD = 2 * W_QK + 2 * W_V + W_ATT + 2 * W_KV + LANES

PREMIX_TILE = 512
TOKEN_TILE = 256
GLA_BLOCK = 256
GLA_LEVELS = ((256, 32), (32, 8), (8, 1))
ATT_Q_TILE = 512
EXPERT_TILE = 512
VMEM_LIMIT = 56 * 1024 * 1024


def _split_bf16(x):
    hi = x.astype(BF16)
    lo = (x - hi.astype(F32)).astype(BF16)
    return hi, lo


def _dot(a, b):
    return jnp.dot(a, b, preferred_element_type=F32)


def _dot_nt(a, b):
    return lax.dot_general(a, b, (((1,), (1,)), ((), ())), preferred_element_type=F32)


def _dot_tn(a, b):
    return lax.dot_general(a, b, (((0,), (0,)), ((), ())), preferred_element_type=F32)


U32 = jnp.uint32


def _pack_bf16_pairs(x):
    w = x.shape[1] // 2
    bits = lambda t: lax.bitcast_convert_type(t.astype(BF16).astype(F32), U32)
    return (bits(x[:, :w]) >> 16) | (bits(x[:, w:]) & jnp.uint32(0xFFFF0000))


def _unpack_bf16_pairs(w):
    lo = lax.bitcast_convert_type(w << 16, F32)
    hi = lax.bitcast_convert_type(w & jnp.uint32(0xFFFF0000), F32)
    return jnp.concatenate([lo, hi], axis=1).astype(BF16)


def _adaln_kernel(c_ref, w_ref, b_ref, o_ref):
    c = c_ref[...]
    a = c * jax.nn.sigmoid(c)
    ah, al = _split_bf16(a)
    wh, wl = _split_bf16(w_ref[...])
    o_ref[...] = _dot(ah, wh) + _dot(al, wh) + _dot(ah, wl) + b_ref[...]


def _adaln(conds, w_ada, b_ada):
    m, d = conds.shape
    n = w_ada.shape[1]
    tn = 512
    return pl.pallas_call(
        _adaln_kernel,
        grid=(n // tn,),
        in_specs=[pl.BlockSpec((m, d), lambda j: (0, 0)),
                  pl.BlockSpec((d, tn), lambda j: (0, j)),
                  pl.BlockSpec((1, tn), lambda j: (0, j))],
        out_specs=pl.BlockSpec((m, tn), lambda j: (0, j)),
        out_shape=jax.ShapeDtypeStruct((m, n), F32),
        name="adaln",
    )(conds, w_ada, b_ada.reshape(1, n))


def _rope_rotate(x, c, s):
    n = x.shape[-1]
    lane = lax.broadcasted_iota(jnp.int32, x.shape, 1)
    partner = jnp.where((lane & 1) == 0, pltpu.roll(x, n - 1, 1), pltpu.roll(x, 1, 1))
    return x * c + partner * s


def _premix_kernel(*refs, use_rope):
    (x_ref, mod_ref, nmix_ref, win_ref, wgh_ref, wgl_ref, bg_ref, bq_ref, bk_ref,
     qn_ref, kn_ref) = refs[:11]
    n_in = 11
    if use_rope:
        cq_ref, sq_ref, ck_ref, sk_ref = refs[11:15]
        n_in = 15
    (qa_o, ka_o, va_o, ga_o, laf_o, lab_o, q_o, k_o, v_o, kc_o, vc_o) = refs[n_in:]

    x = x_ref[0]
    mod = mod_ref[0]
    sh1 = mod[0:1]
    sc1 = mod[1:2]
    ms = jnp.mean(x * x, axis=-1, keepdims=True)
    h = x * lax.rsqrt(ms + RMS_EPS) * nmix_ref[...]
    h = h * (1.0 + sc1) + sh1
    proj = _dot(h.astype(BF16), win_ref[...])

    o = 0
    qa_o[0] = proj[:, o:o + W_QK] * (GLA_DK ** -0.5); o += W_QK
    ka_o[0] = proj[:, o:o + W_QK]; o += W_QK
    va_o[0] = proj[:, o:o + W_V]; o += W_V
    ga_o[0] = proj[:, o:o + W_V]; o += W_V
    qb = proj[:, o:o + W_ATT]; o += W_ATT
    kb = proj[:, o:o + W_KV]; o += W_KV
    vb = proj[:, o:o + W_KV]; o += W_KV
    lr = proj[:, o:o + LANES]

    lh, ll = _split_bf16(lr)
    xg = _dot(lh, wgh_ref[...]) + _dot(ll, wgh_ref[...]) + _dot(lh, wgl_ref[...]) + bg_ref[...]
    la = (jnp.minimum(xg, 0.0) - jnp.log1p(jnp.exp(-jnp.abs(xg)))) * (1.0 / GLA_GATE_NORM)
    laf_o[0] = la[:, :W_QK]
    lab_o[0] = la[:, W_QK:]

    def head_norm(t, ones_ref, g_ref):
        sh, sl = _split_bf16(t * t)
        msq = (_dot(sh, ones_ref[...]) + _dot(sl, ones_ref[...])) * (1.0 / HEAD_DIM)
        return t * lax.rsqrt(msq + RMS_EPS) * g_ref[...]

    qh = head_norm(qb, bq_ref, qn_ref)
    kh = head_norm(kb, bk_ref, kn_ref)
    kc_o[0] = kh
    vc_o[0] = vb
    if use_rope:
        qh = _rope_rotate(qh, cq_ref[...], sq_ref[...])
        kr = _rope_rotate(kh, ck_ref[...], sk_ref[...])
    else:
        kr = kh
    q_o[0] = (qh * (HEAD_DIM ** -0.5)).astype(BF16)
    k_o[0] = kr.astype(BF16)
    v_o[0] = vb.astype(BF16)


def _premix(x, mod, mod_shared, wp, rope):
    b, s, d = x.shape
    tm = min(PREMIX_TILE, s)
    use_rope = rope is not None
    const = lambda shape: pl.BlockSpec(shape, lambda i, j: (0,) * len(shape))
    tok = lambda w: pl.BlockSpec((1, tm, w), lambda i, j: (i, j, 0))
    mod_map = (lambda i, j: (0, 0, 0)) if mod_shared else (lambda i, j: (i, 0, 0))
    in_specs = [tok(d), pl.BlockSpec((1, 8, d), mod_map), const((1, d)),
                const((d, D_IN_PAD)), const((LANES, 2 * W_QK)), const((LANES, 2 * W_QK)),
                const((1, 2 * W_QK)), const((W_ATT, W_ATT)), const((W_KV, W_KV)),
                const((1, W_ATT)), const((1, W_KV))]
    args = [x, mod, wp["norm_mix"], wp["w_in"], wp["wg_hi"], wp["wg_lo"], wp["b_gate"],
            wp["ones_q"], wp["ones_k"], wp["q_norm"], wp["k_norm"]]
    if use_rope:
        seq = lambda w: pl.BlockSpec((tm, w), lambda i, j: (j, 0))
        in_specs += [seq(W_ATT), seq(W_ATT), seq(W_KV), seq(W_KV)]
        args += list(rope)
    widths = [(W_QK, F32), (W_QK, F32), (W_V, F32), (W_V, F32), (W_QK, F32), (W_QK, F32),
              (W_ATT, BF16), (W_KV, BF16), (W_KV, BF16), (W_KV, F32), (W_KV, F32)]
    return pl.pallas_call(
        functools.partial(_premix_kernel, use_rope=use_rope),
        grid=(b, s // tm),
        in_specs=in_specs,
        out_specs=[tok(w) for w, _ in widths],
        out_shape=[jax.ShapeDtypeStruct((b, s, w), dt) for w, dt in widths],
        compiler_params=pltpu.CompilerParams(
            dimension_semantics=("parallel", "parallel"), vmem_limit_bytes=VMEM_LIMIT),
        name="premix_rope" if use_rope else "premix",
    )(*args)


def _bcast_rows(x, n_par, p_rows, row):
    w = x.shape[-1]
    r = x.reshape(n_par, p_rows, w)[:, row:row + 1, :]
    return jnp.broadcast_to(r, (n_par, p_rows, w)).reshape(n_par * p_rows, w)


def _gla_kernel(q_ref, k_ref, v_ref, la_ref, s0_ref, o_ref, sf_ref, st_scr, *, rows, levels, reverse):
    blk = pl.program_id(1)

    @pl.when(blk == 0)
    def _():
        st_scr[...] = s0_ref[0]

    q = q_ref[0]
    k = k_ref[0]
    la = la_ref[0]
    vb = v_ref[0].astype(BF16)

    ri = lax.broadcasted_iota(jnp.int32, (rows, rows), 0)
    ci = lax.broadcasted_iota(jnp.int32, (rows, rows), 1)
    tri = jnp.where((ri <= ci) if reverse else (ri >= ci), 1.0, 0.0).astype(BF16)
    hi = la.astype(BF16)
    r1 = la - hi.astype(F32)
    mid = r1.astype(BF16)
    lo = (r1 - mid.astype(F32)).astype(BF16)
    cum = _dot(tri, hi) + _dot(tri, mid) + _dot(tri, lo)

    ridx = lax.broadcasted_iota(jnp.int32, (rows, W_QK), 0)
    s_acc = [None] * GLA_HEADS
    for (par, sub) in levels:
        groups = par // sub
        n_par = rows // par
        pi = (ridx % par) // sub
        edge = sub - 1 if reverse else 0
        if sub > 1:
            own = _bcast_rows(cum, rows // sub, sub, edge)
            qt = q * jnp.exp(cum - own)
        else:
            qt = q
        qps, kps = [], []
        for p in range(groups):
            if sub > 1 and p == (groups - 1 if reverse else 0):
                continue
            cp = _bcast_rows(cum, n_par, par, p * sub + edge)
            if reverse:
                km = (pi > p) if sub > 1 else (pi >= p)
            else:
                km = (pi < p) if sub > 1 else (pi <= p)
            kps.append(jnp.where(km, k * jnp.exp(cp - cum), 0.0).astype(BF16))
            qps.append(jnp.where(pi == p, qt, 0.0).astype(BF16))
        ng = len(qps)
        lane_c = lax.broadcasted_iota(jnp.int32, (rows, LANES * ng), 1)
        same_parent = (ri // par) == (ci // par)
        for hp in range(GLA_HEADS // 2):
            qc = jnp.concatenate([a[:, LANES * hp:LANES * (hp + 1)] for a in qps], axis=1)
            kc = jnp.concatenate([a[:, LANES * hp:LANES * (hp + 1)] for a in kps], axis=1)
            for hh in range(2):
                qh = jnp.where(((lane_c % LANES) // GLA_DK) == hh, qc, jnp.zeros_like(qc))
                sl = _dot_nt(qh, kc)
                if par < rows:
                    sl = jnp.where(same_parent, sl, 0.0)
                h = 2 * hp + hh
                s_acc[h] = sl if s_acc[h] is None else s_acc[h] + sl

    q0 = (q * jnp.exp(cum)).astype(BF16)
    far = 0 if reverse else rows - 1
    last = cum[far:far + 1, :]
    kd = (k * jnp.exp(last - cum)).astype(BF16)
    br = lax.broadcasted_iota(jnp.int32, (2 * GLA_DV, 2 * GLA_DK), 0)
    bc = lax.broadcasted_iota(jnp.int32, (2 * GLA_DV, 2 * GLA_DK), 1)
    blockdiag = (br // GLA_DV) == (bc // GLA_DK)
    for hp in range(GLA_HEADS // 2):
        st = st_scr[hp]
        o_inter = _dot_nt(q0[:, LANES * hp:LANES * (hp + 1)], st.astype(BF16))
        o_intra = jnp.concatenate(
            [_dot(s_acc[2 * hp + hh].astype(BF16),
                  vb[:, GLA_DV * (2 * hp + hh):GLA_DV * (2 * hp + hh + 1)]) for hh in range(2)],
            axis=1)
        o_ref[0, :, 2 * GLA_DV * hp:2 * GLA_DV * (hp + 1)] = o_inter + o_intra
        upd = _dot_tn(vb[:, 2 * GLA_DV * hp:2 * GLA_DV * (hp + 1)],
                      kd[:, LANES * hp:LANES * (hp + 1)])
        st_scr[hp] = (jnp.exp(last[:, LANES * hp:LANES * (hp + 1)]) * st
                      + jnp.where(blockdiag, upd, 0.0))

    @pl.when(blk == pl.num_programs(1) - 1)
    def _():
        sf_ref[0] = st_scr[...]


def _gla(q, k, v, la, s0t, reverse):
    n, l, _ = q.shape
    rows = min(GLA_BLOCK, l)
    nb = l // rows
    levels = tuple((min(p, rows), s) for p, s in GLA_LEVELS)
    order = (lambda j: nb - 1 - j) if reverse else (lambda j: j)
    tok = lambda w: pl.BlockSpec((1, rows, w), lambda i, j: (i, order(j), 0))
    st_spec = pl.BlockSpec((1, 2, 2 * GLA_DV, 2 * GLA_DK), lambda i, j: (i, 0, 0, 0))
    return pl.pallas_call(
        functools.partial(_gla_kernel, rows=rows, levels=levels, reverse=reverse),
        grid=(n, nb),
        in_specs=[tok(W_QK), tok(W_QK), tok(W_V), tok(W_QK), st_spec],
        out_specs=[tok(W_V), st_spec],
        out_shape=[jax.ShapeDtypeStruct((n, l, W_V), F32),
                   jax.ShapeDtypeStruct(s0t.shape, F32)],
        scratch_shapes=[pltpu.VMEM((2, 2 * GLA_DV, 2 * GLA_DK), F32)],
        compiler_params=pltpu.CompilerParams(
            dimension_semantics=("parallel", "arbitrary"), vmem_limit_bytes=VMEM_LIMIT),
        name="gla_bwd" if reverse else "gla_fwd",
    )(q, k, v, la, s0t)


def _state_to_blockdiag_t(s):
    n = s.shape[0]
    st = jnp.swapaxes(s, -1, -2).reshape(n, 2, 2, GLA_DV, GLA_DK)
    z = jnp.zeros_like(st[:, :, 0])
    top = jnp.concatenate([st[:, :, 0], z], axis=-1)
    bot = jnp.concatenate([z, st[:, :, 1]], axis=-1)
    return jnp.concatenate([top, bot], axis=-2)


def _blockdiag_t_to_state(sb):
    n = sb.shape[0]
    h0 = sb[:, :, :GLA_DV, :GLA_DK]
    h1 = sb[:, :, GLA_DV:, GLA_DK:]
    st = jnp.stack([h0, h1], axis=2).reshape(n, GLA_HEADS, GLA_DV, GLA_DK)
    return jnp.swapaxes(st, -1, -2)


def _attn_kernel(*refs, s_self, n_ctx):
    if n_ctx:
        q_ref, k_ref, v_ref, ck_ref, cv_ref, o_ref, kg_scr, v_scr = refs
    else:
        q_ref, k_ref, v_ref, o_ref, kg_scr, v_scr = refs

    @pl.when(pl.program_id(1) == 0)
    def _():
        def put(kk, vv, start, n):
            lane = lax.broadcasted_iota(jnp.int32, kk.shape, 1)
            kg_scr[0, start:start + n, :] = jnp.where(lane < HEAD_DIM, kk, jnp.zeros_like(kk))
            kg_scr[1, start:start + n, :] = jnp.where(lane >= HEAD_DIM, kk, jnp.zeros_like(kk))
            v_scr[start:start + n, :] = vv
        put(k_ref[0], v_ref[0], 0, s_self)
        if n_ctx:
            put(ck_ref[0].astype(BF16), cv_ref[0].astype(BF16), s_self, n_ctx)

    q = q_ref[0]
    vv = v_scr[...]
    for m in range(ATT_HEADS // KV_HEADS):
        qm = q[:, LANES * m:LANES * (m + 1)]
        og = []
        for g in range(KV_HEADS):
            s = _dot_nt(qm, kg_scr[g])
            mx = jnp.max(s, axis=-1, keepdims=True)
            p = jnp.exp(s - mx)
            l = jnp.sum(p, axis=-1, keepdims=True)
            og.append(_dot(p.astype(BF16), vv) / l)
        lane = lax.broadcasted_iota(jnp.int32, og[0].shape, 1)
        o_ref[0, :, LANES * m:LANES * (m + 1)] = jnp.where(lane < HEAD_DIM, og[0], og[1]).astype(BF16)


def _attention(q, k, v, cache_k=None, cache_v=None):
    b, s, _ = q.shape
    n_ctx = 0 if cache_k is None else cache_k.shape[1]
    tq = min(ATT_Q_TILE, s)
    sk = s + n_ctx
    full = lambda n, w: pl.BlockSpec((1, n, w), lambda i, j: (i, 0, 0))
    in_specs = [pl.BlockSpec((1, tq, W_ATT), lambda i, j: (i, j, 0)), full(s, W_KV), full(s, W_KV)]
    args = [q, k, v]
    if n_ctx:
        in_specs += [full(n_ctx, W_KV), full(n_ctx, W_KV)]
        args += [cache_k, cache_v]
    return pl.pallas_call(
        functools.partial(_attn_kernel, s_self=s, n_ctx=n_ctx),
        grid=(b, s // tq),
        in_specs=in_specs,
        out_specs=pl.BlockSpec((1, tq, W_ATT), lambda i, j: (i, j, 0)),
        out_shape=jax.ShapeDtypeStruct((b, s, W_ATT), BF16),
        scratch_shapes=[pltpu.VMEM((KV_HEADS, sk, W_KV), BF16), pltpu.VMEM((sk, W_KV), BF16)],
        compiler_params=pltpu.CompilerParams(
            dimension_semantics=("parallel", "arbitrary"), vmem_limit_bytes=VMEM_LIMIT),
        name="attention_ctx" if n_ctx else "attention",
    )(*args)


def _postmix_kernel(of_ref, or_ref, ga_ref, oa_ref, x_ref, mod_ref, gn_ref, wout_ref, nffn_ref,
                    wrh_ref, wrl_ref, br_ref, x1_o, h2_o, lp_o, tg_o, meta_o):
    mod = mod_ref[0]
    g1, sh2, sc2 = mod[2:3], mod[3:4], mod[4:5]
    og = of_ref[0] + or_ref[0]
    ga = ga_ref[0]
    parts = []
    for h in range(GLA_HEADS):
        blk = og[:, GLA_DV * h:GLA_DV * (h + 1)]
        ms = jnp.mean(blk * blk, axis=-1, keepdims=True)
        gh = ga[:, GLA_DV * h:GLA_DV * (h + 1)]
        parts.append((blk * lax.rsqrt(ms + RMS_EPS) * gn_ref[...] * (gh * jax.nn.sigmoid(gh))).astype(BF16))
    mix = jnp.concatenate(parts + [oa_ref[0]], axis=1)
    mo = _dot(mix, wout_ref[...])
    x1 = x_ref[0] + g1 * mo
    x1_o[0] = x1
    ms = jnp.mean(x1 * x1, axis=-1, keepdims=True)
    h2 = x1 * lax.rsqrt(ms + RMS_EPS) * nffn_ref[...]
    h2 = h2 * (1.0 + sc2) + sh2
    hh, hl = _split_bf16(h2)
    h2_o[0] = hh

    lg = _dot(hh, wrh_ref[...]) + _dot(hl, wrh_ref[...]) + _dot(hh, wrl_ref[...])
    lt = jnp.transpose(lg)[:N_EXPERTS] + br_ref[...]
    tm = lt.shape[1]
    eidx = lax.broadcasted_iota(jnp.int32, lt.shape, 0)
    vals, sels = [], []
    for _ in range(TOP_K):
        mx = jnp.max(lt, axis=0, keepdims=True)
        idx = jnp.min(jnp.where(lt == mx, eidx, N_EXPERTS), axis=0, keepdims=True)
        sel = eidx == idx
        lt = jnp.where(sel, -jnp.inf, lt)
        vals.append(mx)
        sels.append(sel)
    ws = [jnp.exp(vv - vals[0]) for vv in vals]
    tot = ws[0] + ws[1] + ws[2] + ws[3]
    tg_o[0] = jnp.concatenate([w / tot for w in ws] + [jnp.zeros((8 - TOP_K, tm), F32)], axis=0)

    onehot = sum(jnp.where(s, 1.0, 0.0) for s in sels)
    ti = lax.broadcasted_iota(jnp.int32, (tm, tm), 0)
    tj = lax.broadcasted_iota(jnp.int32, (tm, tm), 1)
    rank = _dot(onehot.astype(BF16), jnp.where(ti < tj, 1.0, 0.0).astype(BF16))
    cnt = jnp.sum(onehot, axis=1, keepdims=True)
    cnt = jnp.floor((cnt + 7.0) * 0.125) * 8.0
    ei = lax.broadcasted_iota(jnp.int32, (N_EXPERTS, N_EXPERTS), 0)
    ej = lax.broadcasted_iota(jnp.int32, (N_EXPERTS, N_EXPERTS), 1)
    cnt_b = jnp.broadcast_to(cnt, (N_EXPERTS, tm))
    seg = _dot(jnp.where(ej < ei, 1.0, 0.0).astype(BF16), cnt_b.astype(BF16))
    base = seg + rank
    lpos = [jnp.sum(jnp.where(s, base, 0.0), axis=0, keepdims=True).astype(jnp.int32) for s in sels]
    lp_o[0] = jnp.concatenate(lpos + [jnp.zeros((8 - TOP_K, tm), jnp.int32)], axis=0)
    meta_o[0] = jnp.concatenate([cnt_b[:, :LANES], seg[:, :LANES]], axis=0).astype(jnp.int32)


def _postmix(o_f, o_r, ga, o_att, x, mod, mod_shared, wp):
    b, s, d = x.shape
    tm = min(TOKEN_TILE, s)
    nt = s // tm
    const = lambda shape: pl.BlockSpec(shape, lambda i, j: (0,) * len(shape))
    tok = lambda w: pl.BlockSpec((1, tm, w), lambda i, j: (i, j, 0))
    mod_map = (lambda i, j: (0, 0, 0)) if mod_shared else (lambda i, j: (i, 0, 0))
    lane_tok = pl.BlockSpec((1, 8, tm), lambda i, j: (i * nt + j, 0, 0))
    return pl.pallas_call(
        _postmix_kernel,
        grid=(b, nt),
        in_specs=[tok(W_V), tok(W_V), tok(W_V), tok(W_ATT), tok(d), pl.BlockSpec((1, 8, d), mod_map),
                  const((1, GLA_DV)), const((d, d)), const((1, d)),
                  const((d, LANES)), const((d, LANES)), const((N_EXPERTS, 1))],
        out_specs=[tok(d), tok(d), lane_tok, lane_tok,
                   pl.BlockSpec((1, 2 * N_EXPERTS, LANES), lambda i, j: (i * nt + j, 0, 0))],
        out_shape=[jax.ShapeDtypeStruct((b, s, d), F32), jax.ShapeDtypeStruct((b, s, d), BF16),
                   jax.ShapeDtypeStruct((b * nt, 8, tm), jnp.int32),
                   jax.ShapeDtypeStruct((b * nt, 8, tm), F32),
                   jax.ShapeDtypeStruct((b * nt, 2 * N_EXPERTS, LANES), jnp.int32)],
        compiler_params=pltpu.CompilerParams(
            dimension_semantics=("parallel", "parallel"), vmem_limit_bytes=VMEM_LIMIT),
        name="postmix",
    )(o_f, o_r, ga, o_att, x, mod, wp["gla_norm"], wp["w_out"], wp["norm_ffn"],
      wp["wr_hi"], wp["wr_lo"], wp["b_router"])


SUBLANES = 8
SEG_PIECES = tuple(TOKEN_TILE >> b for b in range(6))
PAIR_ROWS = TOKEN_TILE * TOP_K
LOCAL_ROWS = PAIR_ROWS + N_EXPERTS * SUBLANES
TAB_REM = 3 * N_EXPERTS


def _pieces(n, emit):
    for size in SEG_PIECES:
        done = n & ~(2 * size - 1)
        pl.when((n & size) != 0)(functools.partial(emit, done, size))


def _segment_copies(tab_ref, copy):
    def body(e, carry):
        dst0 = tab_ref[0, 0, e]
        n = tab_ref[0, 0, N_EXPERTS + e]
        src0 = tab_ref[0, 0, 2 * N_EXPERTS + e]
        _pieces(n, lambda done, size: copy(pl.multiple_of(src0 + done, SUBLANES),
                                           pl.multiple_of(dst0 + done, SUBLANES), size))
        return carry
    lax.fori_loop(0, N_EXPERTS, body, 0)


def _wait_rows(rem, wait_piece):
    wait_piece(PAIR_ROWS)
    _pieces(rem, lambda done, size: wait_piece(size))


def _pair_onehot(lp_ref, weights=None):
    tm = lp_ref.shape[-1]
    rows = lax.broadcasted_iota(jnp.int32, (LOCAL_ROWS, tm), 0)
    acc = jnp.zeros((LOCAL_ROWS, tm), F32)
    for r in range(TOP_K):
        w = 1.0 if weights is None else weights[r]
        acc = jnp.where(rows == lp_ref[0, r:r + 1, :], w, acc)
    return acc


def _dispatch_kernel(tab_ref, pad_ref, h_ref, lp_ref, xs_hbm, xl, zbuf, rem_prev, sem, zsem):
    i = pl.program_id(0)
    slot = i % 2
    perm = _pair_onehot(lp_ref).astype(BF16)
    xl[slot] = _pack_bf16_pairs(_dot(perm, h_ref[...]))

    def copy(local, glob, size):
        pltpu.make_async_copy(xl.at[slot, pl.ds(local, size)], xs_hbm.at[pl.ds(glob, size)],
                              sem.at[slot]).start()
    _segment_copies(tab_ref, copy)

    def wait_slot(s, rem):
        _wait_rows(rem, lambda size: pltpu.make_async_copy(
            xl.at[s, pl.ds(0, size)], xs_hbm.at[pl.ds(0, size)], sem.at[s]).wait())

    @pl.when(i > 0)
    def _():
        wait_slot(1 - slot, rem_prev[0])
    rem_prev[0] = tab_ref[0, 0, TAB_REM]

    @pl.when(i == pl.num_programs(0) - 1)
    def _():
        wait_slot(slot, tab_ref[0, 0, TAB_REM])
        zbuf[...] = jnp.zeros_like(zbuf)

        def pads(wait):
            def body(e, carry):
                start = pad_ref[0, e]
                def emit(done, size):
                    cp = pltpu.make_async_copy(
                        zbuf.at[pl.ds(0, size)],
                        xs_hbm.at[pl.ds(pl.multiple_of(start + done, SUBLANES), size)], zsem)
                    cp.wait() if wait else cp.start()
                _pieces(pad_ref[0, N_EXPERTS + e], emit)
                return carry
            lax.fori_loop(0, N_EXPERTS, body, 0)
        n_tiles = xs_hbm.shape[0] // EXPERT_TILE

        def tail(wait):
            def body(t, carry):
                cp = pltpu.make_async_copy(
                    zbuf, xs_hbm.at[pl.ds(pl.multiple_of(t * EXPERT_TILE, EXPERT_TILE), EXPERT_TILE)], zsem)
                cp.wait() if wait else cp.start()
                return carry
            lax.fori_loop(pad_ref[0, 2 * N_EXPERTS], n_tiles, body, 0)
        pads(False)
        tail(False)
        pads(True)
        tail(True)


def _dispatch(h2, lpos, table, pad_table, n_rows):
    t, d = h2.shape
    tm = TOKEN_TILE
    nt = t // tm
    return pl.pallas_call(
        _dispatch_kernel,
        grid=(nt,),
        in_specs=[pl.BlockSpec((1, 1, LANES), lambda i: (i, 0, 0), memory_space=pltpu.SMEM),
                  pl.BlockSpec((1, LANES), lambda i: (0, 0), memory_space=pltpu.SMEM),
                  pl.BlockSpec((tm, d), lambda i: (i, 0)),
                  pl.BlockSpec((1, 8, tm), lambda i: (i, 0, 0))],
        out_specs=pl.BlockSpec(memory_space=pl.ANY),
        out_shape=jax.ShapeDtypeStruct((n_rows, d // 2), U32),
        scratch_shapes=[pltpu.VMEM((2, LOCAL_ROWS, d // 2), U32), pltpu.VMEM((EXPERT_TILE, d // 2), U32),
                        pltpu.SMEM((1,), jnp.int32),
                        pltpu.SemaphoreType.DMA((2,)), pltpu.SemaphoreType.DMA(())],
        compiler_params=pltpu.CompilerParams(
            dimension_semantics=("arbitrary",), vmem_limit_bytes=VMEM_LIMIT),
        name="dispatch",
    )(table, pad_table, h2, lpos)


def _experts_kernel(te_ref, nv_ref, nr_ref, xs_ref, wgu_ref, bgu_ref, wd_ref, bd_ref, o_ref, wgu_bf, wd_bf):
    i = pl.program_id(0)
    valid = i < nv_ref[0]
    half_rows = EXPERT_TILE // 2
    half = nr_ref[i] <= half_rows
    new_expert = jnp.logical_or(i == 0, te_ref[i] != te_ref[jnp.maximum(i - 1, 0)])

    @pl.when(jnp.logical_and(valid, new_expert))
    def _():
        wgu_bf[...] = wgu_ref[0].astype(BF16)
        wd_bf[...] = wd_ref[0].astype(BF16)

    def mlp(rows):
        x = _unpack_bf16_pairs(xs_ref[0:rows, :])
        gu = _dot(x, wgu_bf[...]) + bgu_ref[0]
        g = jnp.minimum(gu[:, :EXPERT_FF], SWIGLU_LIMIT)
        u = jnp.clip(gu[:, EXPERT_FF:], -SWIGLU_LIMIT, SWIGLU_LIMIT)
        act = (u + 1.0) * (g * jax.nn.sigmoid(SWIGLU_ALPHA * g))
        o_ref[0:rows, :] = _pack_bf16_pairs(_dot(act.astype(BF16), wd_bf[...]) + bd_ref[0])

    @pl.when(jnp.logical_and(valid, jnp.logical_not(half)))
    def _():
        mlp(EXPERT_TILE)

    @pl.when(jnp.logical_and(valid, half))
    def _():
        mlp(half_rows)
        o_ref[half_rows:, :] = jnp.zeros((EXPERT_TILE - half_rows, o_ref.shape[1]), o_ref.dtype)

    @pl.when(jnp.logical_not(valid))
    def _():
        o_ref[...] = jnp.zeros_like(o_ref)


def _experts(xs, tile_expert, n_valid, tile_rows, w_gu, b_gu, w_down, b_down):
    n_rows = xs.shape[0]
    tm = EXPERT_TILE
    nt = n_rows // tm
    ne, d, ff2 = w_gu.shape
    ff = ff2 // 2
    grid_spec = pltpu.PrefetchScalarGridSpec(
        num_scalar_prefetch=3,
        grid=(nt,),
        in_specs=[pl.BlockSpec((tm, d // 2), lambda i, te, nv, nr: (i, 0)),
                  pl.BlockSpec((1, d, ff2), lambda i, te, nv, nr: (te[i], 0, 0)),
                  pl.BlockSpec((1, 1, ff2), lambda i, te, nv, nr: (te[i], 0, 0)),
                  pl.BlockSpec((1, ff, d), lambda i, te, nv, nr: (te[i], 0, 0)),
                  pl.BlockSpec((1, 1, d), lambda i, te, nv, nr: (te[i], 0, 0))],
        out_specs=pl.BlockSpec((tm, d // 2), lambda i, te, nv, nr: (i, 0)),
        scratch_shapes=[pltpu.VMEM((d, ff2), BF16), pltpu.VMEM((ff, d), BF16)],
    )
    return pl.pallas_call(
        _experts_kernel,
        grid_spec=grid_spec,
        out_shape=jax.ShapeDtypeStruct((n_rows, d // 2), U32),
        compiler_params=pltpu.CompilerParams(
            dimension_semantics=("arbitrary",), vmem_limit_bytes=VMEM_LIMIT),
        name="experts",
    )(tile_expert, n_valid, tile_rows, xs, w_gu, b_gu.reshape(ne, 1, ff2), w_down, b_down.reshape(ne, 1, d))


def _combine_kernel(tab_ref, ys_hbm, lp_ref, tg_ref, x1_ref, mod_ref, fn_ref, y_ref, buf, sem):
    def copy(local, glob, size):
        pltpu.make_async_copy(ys_hbm.at[pl.ds(glob, size)], buf.at[pl.ds(local, size)], sem).start()
    _segment_copies(tab_ref, copy)
    tg = tg_ref[0]
    wh, wl = _split_bf16(_pair_onehot(lp_ref, [tg[r:r + 1, :] for r in range(TOP_K)]))
    _wait_rows(tab_ref[0, 0, TAB_REM], lambda size: pltpu.make_async_copy(
        ys_hbm.at[pl.ds(0, size)], buf.at[pl.ds(0, size)], sem).wait())
    nrow = PAIR_ROWS + tab_ref[0, 0, TAB_REM]
    rowi = lax.broadcasted_iota(jnp.int32, buf.shape, 0)
    yb = _unpack_bf16_pairs(jnp.where(rowi < nrow, buf[...], jnp.uint32(0)))
    y = _dot_tn(wh, yb) + _dot_tn(wl, yb)
    g2 = mod_ref[0][5:6]
    x2 = x1_ref[0] + g2 * y
    ms = jnp.mean(x2 * x2, axis=-1, keepdims=True)
    y_ref[0] = x2 * lax.rsqrt(ms + RMS_EPS) * fn_ref[...]


def _combine(ys, table, lpos, gates, x1, mod, mod_shared, final_norm):
    b, s, d = x1.shape
    tm = TOKEN_TILE
    nt = s // tm
    mod_map = (lambda i, j: (0, 0, 0)) if mod_shared else (lambda i, j: (i, 0, 0))
    tile = lambda shape: pl.BlockSpec(shape, lambda i, j: (i * nt + j, 0, 0))
    return pl.pallas_call(
        _combine_kernel,
        grid=(b, nt),
        in_specs=[pl.BlockSpec((1, 1, LANES), lambda i, j: (i * nt + j, 0, 0), memory_space=pltpu.SMEM),
                  pl.BlockSpec(memory_space=pl.ANY),
                  tile((1, 8, tm)), tile((1, 8, tm)),
                  pl.BlockSpec((1, tm, d), lambda i, j: (i, j, 0)),
                  pl.BlockSpec((1, 8, d), mod_map),
                  pl.BlockSpec((1, d), lambda i, j: (0, 0))],
        out_specs=pl.BlockSpec((1, tm, d), lambda i, j: (i, j, 0)),
        out_shape=jax.ShapeDtypeStruct((b, s, d), F32),
        scratch_shapes=[pltpu.VMEM((LOCAL_ROWS, d // 2), U32), pltpu.SemaphoreType.DMA(())],
        compiler_params=pltpu.CompilerParams(
            dimension_semantics=("arbitrary", "arbitrary"), vmem_limit_bytes=VMEM_LIMIT),
        name="combine",
    )(table, ys, lpos, gates, x1, mod, final_norm.reshape(1, d))


def _route_tables(meta, n_tiles_e):
    tm = EXPERT_TILE
    cnt = meta[:, :N_EXPERTS, 0]
    lstart = meta[:, N_EXPERTS:, 0]
    tot = jnp.sum(cnt, axis=0)
    tiles = (tot + tm - 1) // tm
    tile_end = jnp.cumsum(tiles)
    off = (tile_end - tiles) * tm
    dest = off[None, :] + jnp.cumsum(cnt, axis=0) - cnt
    rem = jnp.sum(cnt, axis=1, keepdims=True) - PAIR_ROWS
    table = jnp.concatenate([dest, cnt, lstart, jnp.broadcast_to(rem, cnt.shape)], axis=1).astype(jnp.int32)
    n_valid = tile_end[-1].astype(jnp.int32).reshape(1)
    pad_table = jnp.concatenate([off + tot, tiles * tm - tot, jnp.broadcast_to(n_valid, (2 * N_EXPERTS,))])
    pad_table = pad_table.astype(jnp.int32).reshape(1, LANES)
    tile_ids = jnp.minimum(jnp.arange(n_tiles_e), n_valid[0] - 1)
    tile_expert = jnp.sum((tile_end[None, :] <= tile_ids[:, None]).astype(jnp.int32), axis=1)
    first_tile = (tile_end - tiles)[tile_expert]
    tile_rows = jnp.clip(tot[tile_expert] - (tile_ids - first_tile) * tm, 0, tm).astype(jnp.int32)
    return table.reshape(-1, 1, LANES), pad_table, tile_expert, n_valid, tile_rows


def _rope_tables(n_tok):
    rows = n_tok // GRID_W
    r, col = jnp.meshgrid(jnp.arange(rows), jnp.arange(GRID_W), indexing="ij")
    r = r.reshape(-1).astype(F32)
    col = col.reshape(-1).astype(F32)
    inv = 1.0 / (ROPE_THETA ** (jnp.arange(ROPE_AXIS_PAIRS, dtype=F32) / ROPE_AXIS_PAIRS))
    ang = jnp.concatenate([r[:, None] * inv, col[:, None] * inv], axis=-1)
    c64 = jnp.repeat(jnp.cos(ang), 2, axis=-1)
    sign = jnp.tile(jnp.array([-1.0, 1.0], F32), HEAD_DIM // 2)
    s64 = jnp.repeat(jnp.sin(ang), 2, axis=-1) * sign
    return (jnp.tile(c64, (1, ATT_HEADS)), jnp.tile(s64, (1, ATT_HEADS)),
            jnp.tile(c64, (1, KV_HEADS)), jnp.tile(s64, (1, KV_HEADS)))


def _prep_weights(w_in, w_gate_f, b_gate_f, w_gate_b, b_gate_b, gla_norm, q_norm, k_norm, w_out,
                  norm_mix, norm_ffn, w_router, b_router):
    d = w_in.shape[0]
    o_lr = 2 * W_QK + 2 * W_V
    o_q = o_lr + 2 * GLA_GATE_RANK
    o_k = o_q + W_ATT
    head_order = np.array([0, 4, 1, 5, 2, 6, 3, 7])
    perm = (head_order[:, None] * HEAD_DIM + np.arange(HEAD_DIM)[None, :]).reshape(-1)
    w_q = w_in[:, o_q:o_k][:, perm]
    lr_pad = jnp.zeros((d, LANES - 2 * GLA_GATE_RANK), w_in.dtype)
    w_in_p = jnp.concatenate([w_in[:, :o_lr], w_q, w_in[:, o_k:], w_in[:, o_lr:o_q], lr_pad], axis=1)
    wg = jnp.zeros((LANES, 2 * W_QK), F32)
    wg = wg.at[:GLA_GATE_RANK, :W_QK].set(w_gate_f)
    wg = wg.at[GLA_GATE_RANK:2 * GLA_GATE_RANK, W_QK:].set(w_gate_b)
    wg_hi, wg_lo = _split_bf16(wg)
    ones_q = jnp.asarray(np.kron(np.eye(ATT_HEADS), np.ones((HEAD_DIM, HEAD_DIM))), BF16)
    ones_k = jnp.asarray(np.kron(np.eye(KV_HEADS), np.ones((HEAD_DIM, HEAD_DIM))), BF16)
    w_out_p = jnp.concatenate([w_out[:W_V], w_out[W_V:][perm]], axis=0).astype(BF16)
    wr_hi, wr_lo = _split_bf16(jnp.pad(w_router, ((0, 0), (0, LANES - N_EXPERTS))))
    return {
        "norm_mix": norm_mix.reshape(1, d), "w_in": w_in_p.astype(BF16),
        "wg_hi": wg_hi, "wg_lo": wg_lo,
        "b_gate": jnp.concatenate([b_gate_f, b_gate_b]).reshape(1, 2 * W_QK),
        "ones_q": ones_q, "ones_k": ones_k,
        "q_norm": jnp.tile(q_norm, ATT_HEADS).reshape(1, W_ATT),
        "k_norm": jnp.tile(k_norm, KV_HEADS).reshape(1, W_KV),
        "gla_norm": gla_norm.reshape(1, GLA_DV), "w_out": w_out_p,
        "norm_ffn": norm_ffn.reshape(1, d), "wr_hi": wr_hi, "wr_lo": wr_lo,
        "b_router": b_router.reshape(N_EXPERTS, 1),
    }


def _mixer(x, mod, mod_shared, wp, rope, cache_k, cache_v, s0_f, s0_b):
    b, s, _ = x.shape
    qa, ka, va, ga, la_f, la_b, q, k, v, kc, vc = _premix(x, mod, mod_shared, wp, rope)
    o_f, sf = _gla(qa, ka, va, la_f, _state_to_blockdiag_t(s0_f), False)
    o_r, sb = _gla(qa, ka, va, la_b, _state_to_blockdiag_t(s0_b), True)
    o_att = _attention(q, k, v, cache_k, cache_v)
    x1, h2, lpos, gates, meta = _postmix(o_f, o_r, ga, o_att, x, mod, mod_shared, wp)
    return x1, h2, lpos, gates, meta, kc, vc, _blockdiag_t_to_state(sf), _blockdiag_t_to_state(sb)


def kernel(x_prompt, x_sample, c, cache_k, cache_v, state_gla_fwd, state_gla_bwd, c_ctx, w_ada, b_ada,
           norm_mix, w_in, w_gate_f, b_gate_f, w_gate_b, b_gate_b, gla_norm, q_norm, k_norm, w_out,
           norm_ffn, w_router, b_router, w_gu, b_gu, w_down, b_down, final_norm):
    bp, sp, d = x_prompt.shape
    bs, ss, _ = x_sample.shape
    assert w_ada.shape[0] == 1, "single-layer trunk"
    wp = _prep_weights(w_in[0], w_gate_f[0], b_gate_f[0], w_gate_b[0], b_gate_b[0], gla_norm[0],
                       q_norm[0], k_norm[0], w_out[0], norm_mix[0], norm_ffn[0], w_router[0], b_router[0])

    n_cond = -(-(1 + bs) // 8) * 8
    conds = jnp.zeros((n_cond, d), F32).at[0].set(c_ctx).at[1:1 + bs].set(c)
    mod = _adaln(conds, w_ada[0], b_ada[0]).reshape(n_cond, 6, d)
    mod = jnp.concatenate([mod, jnp.zeros((n_cond, 2, d), F32)], axis=1)
    mod_p, mod_s = mod[0:1], mod[1:1 + bs]

    zero_state = jnp.zeros((bp, GLA_HEADS, GLA_DK, GLA_DV), F32)
    x1p, h2p, lpp, tgp, metap, kc, vc, sf, sb = _mixer(x_prompt, mod_p, True, wp, None, None, None,
                                                       zero_state, zero_state)
    n_ctx = cache_k.shape[2]
    x1s, h2s, lps, tgs, metas, _, _, _, _ = _mixer(
        x_sample, mod_s, False, wp, _rope_tables(ss),
        cache_k[:, 0].reshape(bs, n_ctx, W_KV), cache_v[:, 0].reshape(bs, n_ctx, W_KV),
        state_gla_fwd[:, 0].astype(F32), state_gla_bwd[:, 0].astype(F32))

    n_p, n_s = bp * sp, bs * ss
    n_tok = n_p + n_s
    ntp = n_p // TOKEN_TILE
    h2 = jnp.concatenate([h2p.reshape(n_p, d), h2s.reshape(n_s, d)], axis=0)
    lpos = jnp.concatenate([lpp, lps], axis=0)
    worst = n_tok * TOP_K + (n_tok // TOKEN_TILE) * N_EXPERTS * (SUBLANES - 1) + N_EXPERTS * (EXPERT_TILE - SUBLANES)
    n_rows = -(-worst // EXPERT_TILE) * EXPERT_TILE
    table, pad_table, tile_expert, n_valid, tile_rows = _route_tables(
        jnp.concatenate([metap, metas], axis=0), n_rows // EXPERT_TILE)
    xs = _dispatch(h2, lpos, table, pad_table, n_rows)
    ys = _experts(xs, tile_expert, n_valid, tile_rows, w_gu[0], b_gu[0], w_down[0], b_down[0])
    y_prompt = _combine(ys, table[:ntp], lpp, tgp, x1p, mod_p, True, final_norm)
    y_sample = _combine(ys, table[ntp:], lps, tgs, x1s, mod_s, False, final_norm)

    new_cache_k = kc.reshape(bp, 1, sp, KV_HEADS, HEAD_DIM)
    new_cache_v = vc.reshape(bp, 1, sp, KV_HEADS, HEAD_DIM)
    return (y_prompt, y_sample, new_cache_k, new_cache_v, sf[:, None], sb[:, None])
```

```python
import functools

import numpy as np
import jax
import jax.numpy as jnp
from jax import lax
from jax.experimental import pallas as pl
from jax.experimental.pallas import tpu as pltpu

F32 = jnp.float32
BF16 = jnp.bfloat16

D_MODEL = 1024
GRID_W = 64
GLA_HEADS = 4
GLA_DV = 128
GLA_DK = 64
GLA_GATE_RANK = 16
GLA_GATE_NORM = 16.0
HEAD_DIM = 64
ATT_HEADS = 8
KV_HEADS = 2
ROPE_THETA = 10000.0
ROPE_AXIS_PAIRS = HEAD_DIM // 4
N_EXPERTS = 32
TOP_K = 4
EXPERT_FF = 1024
SWIGLU_LIMIT = 7.0
SWIGLU_ALPHA = 1.702
RMS_EPS = 1e-6

W_QK = GLA_HEADS * GLA_DK
W_V = GLA_HEADS * GLA_DV
W_ATT = ATT_HEADS * HEAD_DIM
W_KV = KV_HEADS * HEAD_DIM
LANES = 128
D_IN_PAD = 2 * W_QK + 2 * W_V + W_ATT + 2 * W_KV + LANES

PREMIX_TILE = 512
TOKEN_TILE = 256
GLA_BLOCK = 256
GLA_LEVELS = ((256, 32), (32, 8), (8, 1))
ATT_Q_TILE = 512
EXPERT_TILE = 512
VMEM_LIMIT = 56 * 1024 * 1024


def _split_bf16(x):
    hi = x.astype(BF16)
    lo = (x - hi.astype(F32)).astype(BF16)
    return hi, lo


def _dot(a, b):
    return jnp.dot(a, b, preferred_element_type=F32)


def _dot_nt(a, b):
    return lax.dot_general(a, b, (((1,), (1,)), ((), ())), preferred_element_type=F32)


def _dot_tn(a, b):
    return lax.dot_general(a, b, (((0,), (0,)), ((), ())), preferred_element_type=F32)


U32 = jnp.uint32


def _pack_bf16_pairs(x):
    w = x.shape[1] // 2
    bits = lambda t: lax.bitcast_convert_type(t.astype(BF16).astype(F32), U32)
    return (bits(x[:, :w]) >> 16) | (bits(x[:, w:]) & jnp.uint32(0xFFFF0000))


def _unpack_bf16_pairs(w):
    lo = lax.bitcast_convert_type(w << 16, F32)
    hi = lax.bitcast_convert_type(w & jnp.uint32(0xFFFF0000), F32)
    return jnp.concatenate([lo, hi], axis=1).astype(BF16)


def _adaln_kernel(c_ref, w_ref, b_ref, o_ref):
    c = c_ref[...]
    a = c * jax.nn.sigmoid(c)
    ah, al = _split_bf16(a)
    wh, wl = _split_bf16(w_ref[...])
    o_ref[...] = _dot(ah, wh) + _dot(al, wh) + _dot(ah, wl) + b_ref[...]


def _adaln(conds, w_ada, b_ada):
    m, d = conds.shape
    n = w_ada.shape[1]
    tn = 512
    return pl.pallas_call(
        _adaln_kernel,
        grid=(n // tn,),
        in_specs=[pl.BlockSpec((m, d), lambda j: (0, 0)),
                  pl.BlockSpec((d, tn), lambda j: (0, j)),
                  pl.BlockSpec((1, tn), lambda j: (0, j))],
        out_specs=pl.BlockSpec((m, tn), lambda j: (0, j)),
        out_shape=jax.ShapeDtypeStruct((m, n), F32),
        name="adaln",
    )(conds, w_ada, b_ada.reshape(1, n))


def _rope_rotate(x, c, s):
    n = x.shape[-1]
    lane = lax.broadcasted_iota(jnp.int32, x.shape, 1)
    partner = jnp.where((lane & 1) == 0, pltpu.roll(x, n - 1, 1), pltpu.roll(x, 1, 1))
    return x * c + partner * s


def _premix_kernel(*refs, use_rope):
    (x_ref, mod_ref, nmix_ref, win_ref, wgh_ref, wgl_ref, bg_ref, bq_ref, bk_ref,
     qn_ref, kn_ref) = refs[:11]
    n_in = 11
    if use_rope:
        cq_ref, sq_ref, ck_ref, sk_ref = refs[11:15]
        n_in = 15
    (qa_o, ka_o, va_o, ga_o, laf_o, lab_o, q_o, k_o, v_o, kc_o, vc_o) = refs[n_in:]

    x = x_ref[0]
    mod = mod_ref[0]
    sh1 = mod[0:1]
    sc1 = mod[1:2]
    ms = jnp.mean(x * x, axis=-1, keepdims=True)
    h = x * lax.rsqrt(ms + RMS_EPS) * nmix_ref[...]
    h = h * (1.0 + sc1) + sh1
    proj = _dot(h.astype(BF16), win_ref[...])

    o = 0
    qa_o[0] = proj[:, o:o + W_QK] * (GLA_DK ** -0.5); o += W_QK
    ka_o[0] = proj[:, o:o + W_QK]; o += W_QK
    va_o[0] = proj[:, o:o + W_V]; o += W_V
    ga_o[0] = proj[:, o:o + W_V]; o += W_V
    qb = proj[:, o:o + W_ATT]; o += W_ATT
    kb = proj[:, o:o + W_KV]; o += W_KV
    vb = proj[:, o:o + W_KV]; o += W_KV
    lr = proj[:, o:o + LANES]

    lh, ll = _split_bf16(lr)
    xg = _dot(lh, wgh_ref[...]) + _dot(ll, wgh_ref[...]) + _dot(lh, wgl_ref[...]) + bg_ref[...]
    la = (jnp.minimum(xg, 0.0) - jnp.log1p(jnp.exp(-jnp.abs(xg)))) * (1.0 / GLA_GATE_NORM)
    laf_o[0] = la[:, :W_QK]
    lab_o[0] = la[:, W_QK:]

    def head_norm(t, ones_ref, g_ref):
        sh, sl = _split_bf16(t * t)
        msq = (_dot(sh, ones_ref[...]) + _dot(sl, ones_ref[...])) * (1.0 / HEAD_DIM)
        return t * lax.rsqrt(msq + RMS_EPS) * g_ref[...]

    qh = head_norm(qb, bq_ref, qn_ref)
    kh = head_norm(kb, bk_ref, kn_ref)
    kc_o[0] = kh
    vc_o[0] = vb
    if use_rope:
        qh = _rope_rotate(qh, cq_ref[...], sq_ref[...])
        kr = _rope_rotate(kh, ck_ref[...], sk_ref[...])
    else:
        kr = kh
    q_o[0] = (qh * (HEAD_DIM ** -0.5)).astype(BF16)
    k_o[0] = kr.astype(BF16)
    v_o[0] = vb.astype(BF16)


def _premix(x, mod, mod_shared, wp, rope):
    b, s, d = x.shape
    tm = min(PREMIX_TILE, s)
    use_rope = rope is not None
    const = lambda shape: pl.BlockSpec(shape, lambda i, j: (0,) * len(shape))
    tok = lambda w: pl.BlockSpec((1, tm, w), lambda i, j: (i, j, 0))
    mod_map = (lambda i, j: (0, 0, 0)) if mod_shared else (lambda i, j: (i, 0, 0))
    in_specs = [tok(d), pl.BlockSpec((1, 8, d), mod_map), const((1, d)),
                const((d, D_IN_PAD)), const((LANES, 2 * W_QK)), const((LANES, 2 * W_QK)),
                const((1, 2 * W_QK)), const((W_ATT, W_ATT)), const((W_KV, W_KV)),
                const((1, W_ATT)), const((1, W_KV))]
    args = [x, mod, wp["norm_mix"], wp["w_in"], wp["wg_hi"], wp["wg_lo"], wp["b_gate"],
            wp["ones_q"], wp["ones_k"], wp["q_norm"], wp["k_norm"]]
    if use_rope:
        seq = lambda w: pl.BlockSpec((tm, w), lambda i, j: (j, 0))
        in_specs += [seq(W_ATT), seq(W_ATT), seq(W_KV), seq(W_KV)]
        args += list(rope)
    widths = [(W_QK, F32), (W_QK, F32), (W_V, F32), (W_V, F32), (W_QK, F32), (W_QK, F32),
              (W_ATT, BF16), (W_KV, BF16), (W_KV, BF16), (W_KV, F32), (W_KV, F32)]
    return pl.pallas_call(
        functools.partial(_premix_kernel, use_rope=use_rope),
        grid=(b, s // tm),
        in_specs=in_specs,
        out_specs=[tok(w) for w, _ in widths],
        out_shape=[jax.ShapeDtypeStruct((b, s, w), dt) for w, dt in widths],
        compiler_params=pltpu.CompilerParams(
            dimension_semantics=("parallel", "parallel"), vmem_limit_bytes=VMEM_LIMIT),
        name="premix_rope" if use_rope else "premix",
    )(*args)


def _bcast_rows(x, n_par, p_rows, row):
    w = x.shape[-1]
    r = x.reshape(n_par, p_rows, w)[:, row:row + 1, :]
    return jnp.broadcast_to(r, (n_par, p_rows, w)).reshape(n_par * p_rows, w)


def _gla_kernel(q_ref, k_ref, v_ref, la_ref, s0_ref, o_ref, sf_ref, st_scr, *, rows, levels, reverse):
    blk = pl.program_id(1)

    @pl.when(blk == 0)
    def _():
        st_scr[...] = s0_ref[0]

    q = q_ref[0]
    k = k_ref[0]
    la = la_ref[0]
    vb = v_ref[0].astype(BF16)

    ri = lax.broadcasted_iota(jnp.int32, (rows, rows), 0)
    ci = lax.broadcasted_iota(jnp.int32, (rows, rows), 1)
    tri = jnp.where((ri <= ci) if reverse else (ri >= ci), 1.0, 0.0).astype(BF16)
    hi = la.astype(BF16)
    r1 = la - hi.astype(F32)
    mid = r1.astype(BF16)
    lo = (r1 - mid.astype(F32)).astype(BF16)
    cum = _dot(tri, hi) + _dot(tri, mid) + _dot(tri, lo)

    ridx = lax.broadcasted_iota(jnp.int32, (rows, W_QK), 0)
    s_acc = [None] * GLA_HEADS
    for (par, sub) in levels:
        groups = par // sub
        n_par = rows // par
        pi = (ridx % par) // sub
        edge = sub - 1 if reverse else 0
        if sub > 1:
            own = _bcast_rows(cum, rows // sub, sub, edge)
            qt = q * jnp.exp(cum - own)
        else:
            qt = q
        qps, kps = [], []
        for p in range(groups):
            if sub > 1 and p == (groups - 1 if reverse else 0):
                continue
            cp = _bcast_rows(cum, n_par, par, p * sub + edge)
            if reverse:
                km = (pi > p) if sub > 1 else (pi >= p)
            else:
                km = (pi < p) if sub > 1 else (pi <= p)
            kps.append(jnp.where(km, k * jnp.exp(cp - cum), 0.0).astype(BF16))
            qps.append(jnp.where(pi == p, qt, 0.0).astype(BF16))
        ng = len(qps)
        lane_c = lax.broadcasted_iota(jnp.int32, (rows, LANES * ng), 1)
        same_parent = (ri // par) == (ci // par)
        for hp in range(GLA_HEADS // 2):
            qc = jnp.concatenate([a[:, LANES * hp:LANES * (hp + 1)] for a in qps], axis=1)
            kc = jnp.concatenate([a[:, LANES * hp:LANES * (hp + 1)] for a in kps], axis=1)
            for hh in range(2):
                qh = jnp.where(((lane_c % LANES) // GLA_DK) == hh, qc, jnp.zeros_like(qc))
                sl = _dot_nt(qh, kc)
                if par < rows:
                    sl = jnp.where(same_parent, sl, 0.0)
                h = 2 * hp + hh
                s_acc[h] = sl if s_acc[h] is None else s_acc[h] + sl

    q0 = (q * jnp.exp(cum)).astype(BF16)
    far = 0 if reverse else rows - 1
    last = cum[far:far + 1, :]
    kd = (k * jnp.exp(last - cum)).astype(BF16)
    br = lax.broadcasted_iota(jnp.int32, (2 * GLA_DV, 2 * GLA_DK), 0)
    bc = lax.broadcasted_iota(jnp.int32, (2 * GLA_DV, 2 * GLA_DK), 1)
    blockdiag = (br // GLA_DV) == (bc // GLA_DK)
    for hp in range(GLA_HEADS // 2):
        st = st_scr[hp]
        o_inter = _dot_nt(q0[:, LANES * hp:LANES * (hp + 1)], st.astype(BF16))
        o_intra = jnp.concatenate(
            [_dot(s_acc[2 * hp + hh].astype(BF16),
                  vb[:, GLA_DV * (2 * hp + hh):GLA_DV * (2 * hp + hh + 1)]) for hh in range(2)],
            axis=1)
        o_ref[0, :, 2 * GLA_DV * hp:2 * GLA_DV * (hp + 1)] = o_inter + o_intra
        upd = _dot_tn(vb[:, 2 * GLA_DV * hp:2 * GLA_DV * (hp + 1)],
                      kd[:, LANES * hp:LANES * (hp + 1)])
        st_scr[hp] = (jnp.exp(last[:, LANES * hp:LANES * (hp + 1)]) * st
                      + jnp.where(blockdiag, upd, 0.0))

    @pl.when(blk == pl.num_programs(1) - 1)
    def _():
        sf_ref[0] = st_scr[...]


def _gla(q, k, v, la, s0t, reverse):
    n, l, _ = q.shape
    rows = min(GLA_BLOCK, l)
    nb = l // rows
    levels = tuple((min(p, rows), s) for p, s in GLA_LEVELS)
    order = (lambda j: nb - 1 - j) if reverse else (lambda j: j)
    tok = lambda w: pl.BlockSpec((1, rows, w), lambda i, j: (i, order(j), 0))
    st_spec = pl.BlockSpec((1, 2, 2 * GLA_DV, 2 * GLA_DK), lambda i, j: (i, 0, 0, 0))
    return pl.pallas_call(
        functools.partial(_gla_kernel, rows=rows, levels=levels, reverse=reverse),
        grid=(n, nb),
        in_specs=[tok(W_QK), tok(W_QK), tok(W_V), tok(W_QK), st_spec],
        out_specs=[tok(W_V), st_spec],
        out_shape=[jax.ShapeDtypeStruct((n, l, W_V), F32),
                   jax.ShapeDtypeStruct(s0t.shape, F32)],
        scratch_shapes=[pltpu.VMEM((2, 2 * GLA_DV, 2 * GLA_DK), F32)],
        compiler_params=pltpu.CompilerParams(
            dimension_semantics=("parallel", "arbitrary"), vmem_limit_bytes=VMEM_LIMIT),
        name="gla_bwd" if reverse else "gla_fwd",
    )(q, k, v, la, s0t)


def _state_to_blockdiag_t(s):
    n = s.shape[0]
    st = jnp.swapaxes(s, -1, -2).reshape(n, 2, 2, GLA_DV, GLA_DK)
    z = jnp.zeros_like(st[:, :, 0])
    top = jnp.concatenate([st[:, :, 0], z], axis=-1)
    bot = jnp.concatenate([z, st[:, :, 1]], axis=-1)
    return jnp.concatenate([top, bot], axis=-2)


def _blockdiag_t_to_state(sb):
    n = sb.shape[0]
    h0 = sb[:, :, :GLA_DV, :GLA_DK]
    h1 = sb[:, :, GLA_DV:, GLA_DK:]
    st = jnp.stack([h0, h1], axis=2).reshape(n, GLA_HEADS, GLA_DV, GLA_DK)
    return jnp.swapaxes(st, -1, -2)


def _attn_kernel(*refs, s_self, n_ctx):
    if n_ctx:
        q_ref, k_ref, v_ref, ck_ref, cv_ref, o_ref, kg_scr, v_scr = refs
    else:
        q_ref, k_ref, v_ref, o_ref, kg_scr, v_scr = refs

    @pl.when(pl.program_id(1) == 0)
    def _():
        def put(kk, vv, start, n):
            lane = lax.broadcasted_iota(jnp.int32, kk.shape, 1)
            kg_scr[0, start:start + n, :] = jnp.where(lane < HEAD_DIM, kk, jnp.zeros_like(kk))
            kg_scr[1, start:start + n, :] = jnp.where(lane >= HEAD_DIM, kk, jnp.zeros_like(kk))
            v_scr[start:start + n, :] = vv
        put(k_ref[0], v_ref[0], 0, s_self)
        if n_ctx:
            put(ck_ref[0].astype(BF16), cv_ref[0].astype(BF16), s_self, n_ctx)

    q = q_ref[0]
    vv = v_scr[...]
    for m in range(ATT_HEADS // KV_HEADS):
        qm = q[:, LANES * m:LANES * (m + 1)]
        og = []
        for g in range(KV_HEADS):
            s = _dot_nt(qm, kg_scr[g])
            mx = jnp.max(s, axis=-1, keepdims=True)
            p = jnp.exp(s - mx)
            l = jnp.sum(p, axis=-1, keepdims=True)
            og.append(_dot(p.astype(BF16), vv) / l)
        lane = lax.broadcasted_iota(jnp.int32, og[0].shape, 1)
        o_ref[0, :, LANES * m:LANES * (m + 1)] = jnp.where(lane < HEAD_DIM, og[0], og[1]).astype(BF16)


def _attention(q, k, v, cache_k=None, cache_v=None):
    b, s, _ = q.shape
    n_ctx = 0 if cache_k is None else cache_k.shape[1]
    tq = min(ATT_Q_TILE, s)
    sk = s + n_ctx
    full = lambda n, w: pl.BlockSpec((1, n, w), lambda i, j: (i, 0, 0))
    in_specs = [pl.BlockSpec((1, tq, W_ATT), lambda i, j: (i, j, 0)), full(s, W_KV), full(s, W_KV)]
    args = [q, k, v]
    if n_ctx:
        in_specs += [full(n_ctx, W_KV), full(n_ctx, W_KV)]
        args += [cache_k, cache_v]
    return pl.pallas_call(
        functools.partial(_attn_kernel, s_self=s, n_ctx=n_ctx),
        grid=(b, s // tq),
        in_specs=in_specs,
        out_specs=pl.BlockSpec((1, tq, W_ATT), lambda i, j: (i, j, 0)),
        out_shape=jax.ShapeDtypeStruct((b, s, W_ATT), BF16),
        scratch_shapes=[pltpu.VMEM((KV_HEADS, sk, W_KV), BF16), pltpu.VMEM((sk, W_KV), BF16)],
        compiler_params=pltpu.CompilerParams(
            dimension_semantics=("parallel", "arbitrary"), vmem_limit_bytes=VMEM_LIMIT),
        name="attention_ctx" if n_ctx else "attention",
    )(*args)


def _postmix_kernel(of_ref, or_ref, ga_ref, oa_ref, x_ref, mod_ref, gn_ref, wout_ref, nffn_ref,
                    wrh_ref, wrl_ref, br_ref, x1_o, h2_o, lp_o, tg_o, meta_o):
    mod = mod_ref[0]
    g1, sh2, sc2 = mod[2:3], mod[3:4], mod[4:5]
    og = of_ref[0] + or_ref[0]
    ga = ga_ref[0]
    parts = []
    for h in range(GLA_HEADS):
        blk = og[:, GLA_DV * h:GLA_DV * (h + 1)]
        ms = jnp.mean(blk * blk, axis=-1, keepdims=True)
        gh = ga[:, GLA_DV * h:GLA_DV * (h + 1)]
        parts.append((blk * lax.rsqrt(ms + RMS_EPS) * gn_ref[...] * (gh * jax.nn.sigmoid(gh))).astype(BF16))
    mix = jnp.concatenate(parts + [oa_ref[0]], axis=1)
    mo = _dot(mix, wout_ref[...])
    x1 = x_ref[0] + g1 * mo
    x1_o[0] = x1
    ms = jnp.mean(x1 * x1, axis=-1, keepdims=True)
    h2 = x1 * lax.rsqrt(ms + RMS_EPS) * nffn_ref[...]
    h2 = h2 * (1.0 + sc2) + sh2
    hh, hl = _split_bf16(h2)
    h2_o[0] = hh

    lg = _dot(hh, wrh_ref[...]) + _dot(hl, wrh_ref[...]) + _dot(hh, wrl_ref[...])
    lt = jnp.transpose(lg)[:N_EXPERTS] + br_ref[...]
    tm = lt.shape[1]
    eidx = lax.broadcasted_iota(jnp.int32, lt.shape, 0)
    vals, sels = [], []
    for _ in range(TOP_K):
        mx = jnp.max(lt, axis=0, keepdims=True)
        idx = jnp.min(jnp.where(lt == mx, eidx, N_EXPERTS), axis=0, keepdims=True)
        sel = eidx == idx
        lt = jnp.where(sel, -jnp.inf, lt)
        vals.append(mx)
        sels.append(sel)
    ws = [jnp.exp(vv - vals[0]) for vv in vals]
    tot = ws[0] + ws[1] + ws[2] + ws[3]
    tg_o[0] = jnp.concatenate([w / tot for w in ws] + [jnp.zeros((8 - TOP_K, tm), F32)], axis=0)

    onehot = sum(jnp.where(s, 1.0, 0.0) for s in sels)
    ti = lax.broadcasted_iota(jnp.int32, (tm, tm), 0)
    tj = lax.broadcasted_iota(jnp.int32, (tm, tm), 1)
    rank = _dot(onehot.astype(BF16), jnp.where(ti < tj, 1.0, 0.0).astype(BF16))
    cnt = jnp.sum(onehot, axis=1, keepdims=True)
    cnt = jnp.floor((cnt + 7.0) * 0.125) * 8.0
    ei = lax.broadcasted_iota(jnp.int32, (N_EXPERTS, N_EXPERTS), 0)
    ej = lax.broadcasted_iota(jnp.int32, (N_EXPERTS, N_EXPERTS), 1)
    cnt_b = jnp.broadcast_to(cnt, (N_EXPERTS, tm))
    seg = _dot(jnp.where(ej < ei, 1.0, 0.0).astype(BF16), cnt_b.astype(BF16))
    base = seg + rank
    lpos = [jnp.sum(jnp.where(s, base, 0.0), axis=0, keepdims=True).astype(jnp.int32) for s in sels]
    lp_o[0] = jnp.concatenate(lpos + [jnp.zeros((8 - TOP_K, tm), jnp.int32)], axis=0)
    meta_o[0] = jnp.concatenate([cnt_b[:, :LANES], seg[:, :LANES]], axis=0).astype(jnp.int32)


def _postmix(o_f, o_r, ga, o_att, x, mod, mod_shared, wp):
    b, s, d = x.shape
    tm = min(TOKEN_TILE, s)
    nt = s // tm
    const = lambda shape: pl.BlockSpec(shape, lambda i, j: (0,) * len(shape))
    tok = lambda w: pl.BlockSpec((1, tm, w), lambda i, j: (i, j, 0))
    mod_map = (lambda i, j: (0, 0, 0)) if mod_shared else (lambda i, j: (i, 0, 0))
    lane_tok = pl.BlockSpec((1, 8, tm), lambda i, j: (i * nt + j, 0, 0))
    return pl.pallas_call(
        _postmix_kernel,
        grid=(b, nt),
        in_specs=[tok(W_V), tok(W_V), tok(W_V), tok(W_ATT), tok(d), pl.BlockSpec((1, 8, d), mod_map),
                  const((1, GLA_DV)), const((d, d)), const((1, d)),
                  const((d, LANES)), const((d, LANES)), const((N_EXPERTS, 1))],
        out_specs=[tok(d), tok(d), lane_tok, lane_tok,
                   pl.BlockSpec((1, 2 * N_EXPERTS, LANES), lambda i, j: (i * nt + j, 0, 0))],
        out_shape=[jax.ShapeDtypeStruct((b, s, d), F32), jax.ShapeDtypeStruct((b, s, d), BF16),
                   jax.ShapeDtypeStruct((b * nt, 8, tm), jnp.int32),
                   jax.ShapeDtypeStruct((b * nt, 8, tm), F32),
                   jax.ShapeDtypeStruct((b * nt, 2 * N_EXPERTS, LANES), jnp.int32)],
        compiler_params=pltpu.CompilerParams(
            dimension_semantics=("parallel", "parallel"), vmem_limit_bytes=VMEM_LIMIT),
        name="postmix",
    )(o_f, o_r, ga, o_att, x, mod, wp["gla_norm"], wp["w_out"], wp["norm_ffn"],
      wp["wr_hi"], wp["wr_lo"], wp["b_router"])


SUBLANES = 8
SEG_PIECES = tuple(TOKEN_TILE >> b for b in range(6))
PAIR_ROWS = TOKEN_TILE * TOP_K
LOCAL_ROWS = PAIR_ROWS + N_EXPERTS * SUBLANES
TAB_REM = 3 * N_EXPERTS


def _pieces(n, emit):
    for size in SEG_PIECES:
        done = n & ~(2 * size - 1)
        pl.when((n & size) != 0)(functools.partial(emit, done, size))


def _segment_copies(tab_ref, copy):
    def body(e, carry):
        dst0 = tab_ref[0, 0, e]
        n = tab_ref[0, 0, N_EXPERTS + e]
        src0 = tab_ref[0, 0, 2 * N_EXPERTS + e]
        _pieces(n, lambda done, size: copy(pl.multiple_of(src0 + done, SUBLANES),
                                           pl.multiple_of(dst0 + done, SUBLANES), size))
        return carry
    lax.fori_loop(0, N_EXPERTS, body, 0)


def _wait_rows(rem, wait_piece):
    wait_piece(PAIR_ROWS)
    _pieces(rem, lambda done, size: wait_piece(size))


def _pair_onehot(lp_ref, weights=None):
    tm = lp_ref.shape[-1]
    rows = lax.broadcasted_iota(jnp.int32, (LOCAL_ROWS, tm), 0)
    acc = jnp.zeros((LOCAL_ROWS, tm), F32)
    for r in range(TOP_K):
        w = 1.0 if weights is None else weights[r]
        acc = jnp.where(rows == lp_ref[0, r:r + 1, :], w, acc)
    return acc


def _dispatch_kernel(tab_ref, pad_ref, h_ref, lp_ref, xs_hbm, xl, zbuf, rem_prev, sem, zsem):
    i = pl.program_id(0)
    slot = i % 2
    perm = _pair_onehot(lp_ref).astype(BF16)
    xl[slot] = _pack_bf16_pairs(_dot(perm, h_ref[...]))

    def copy(local, glob, size):
        pltpu.make_async_copy(xl.at[slot, pl.ds(local, size)], xs_hbm.at[pl.ds(glob, size)],
                              sem.at[slot]).start()
    _segment_copies(tab_ref, copy)

    def wait_slot(s, rem):
        _wait_rows(rem, lambda size: pltpu.make_async_copy(
            xl.at[s, pl.ds(0, size)], xs_hbm.at[pl.ds(0, size)], sem.at[s]).wait())

    @pl.when(i > 0)
    def _():
        wait_slot(1 - slot, rem_prev[0])
    rem_prev[0] = tab_ref[0, 0, TAB_REM]

    @pl.when(i == pl.num_programs(0) - 1)
    def _():
        wait_slot(slot, tab_ref[0, 0, TAB_REM])
        zbuf[...] = jnp.zeros_like(zbuf)

        def pads(wait):
            def body(e, carry):
                start = pad_ref[0, e]
                def emit(done, size):
                    cp = pltpu.make_async_copy(
                        zbuf.at[pl.ds(0, size)],
                        xs_hbm.at[pl.ds(pl.multiple_of(start + done, SUBLANES), size)], zsem)
                    cp.wait() if wait else cp.start()
                _pieces(pad_ref[0, N_EXPERTS + e], emit)
                return carry
            lax.fori_loop(0, N_EXPERTS, body, 0)
        n_tiles = xs_hbm.shape[0] // EXPERT_TILE

        def tail(wait):
            def body(t, carry):
                cp = pltpu.make_async_copy(
                    zbuf, xs_hbm.at[pl.ds(pl.multiple_of(t * EXPERT_TILE, EXPERT_TILE), EXPERT_TILE)], zsem)
                cp.wait() if wait else cp.start()
                return carry
            lax.fori_loop(pad_ref[0, 2 * N_EXPERTS], n_tiles, body, 0)
        pads(False)
        tail(False)
        pads(True)
        tail(True)


def _dispatch(h2, lpos, table, pad_table, n_rows):
    t, d = h2.shape
    tm = TOKEN_TILE
    nt = t // tm
    return pl.pallas_call(
        _dispatch_kernel,
        grid=(nt,),
        in_specs=[pl.BlockSpec((1, 1, LANES), lambda i: (i, 0, 0), memory_space=pltpu.SMEM),
                  pl.BlockSpec((1, LANES), lambda i: (0, 0), memory_space=pltpu.SMEM),
                  pl.BlockSpec((tm, d), lambda i: (i, 0)),
                  pl.BlockSpec((1, 8, tm), lambda i: (i, 0, 0))],
        out_specs=pl.BlockSpec(memory_space=pl.ANY),
        out_shape=jax.ShapeDtypeStruct((n_rows, d // 2), U32),
        scratch_shapes=[pltpu.VMEM((2, LOCAL_ROWS, d // 2), U32), pltpu.VMEM((EXPERT_TILE, d // 2), U32),
                        pltpu.SMEM((1,), jnp.int32),
                        pltpu.SemaphoreType.DMA((2,)), pltpu.SemaphoreType.DMA(())],
        compiler_params=pltpu.CompilerParams(
            dimension_semantics=("arbitrary",), vmem_limit_bytes=VMEM_LIMIT),
        name="dispatch",
    )(table, pad_table, h2, lpos)


def _experts_kernel(te_ref, nv_ref, xs_ref, wgu_ref, bgu_ref, wd_ref, bd_ref, o_ref, wgu_bf, wd_bf):
    i = pl.program_id(0)
    valid = i < nv_ref[0]
    new_expert = jnp.logical_or(i == 0, te_ref[i] != te_ref[jnp.maximum(i - 1, 0)])

    @pl.when(jnp.logical_and(valid, new_expert))
    def _():
        wgu_bf[...] = wgu_ref[0].astype(BF16)
        wd_bf[...] = wd_ref[0].astype(BF16)

    @pl.when(valid)
    def _():
        x = _unpack_bf16_pairs(xs_ref[...])
        gu = _dot(x, wgu_bf[...]) + bgu_ref[0]
        g = jnp.minimum(gu[:, :EXPERT_FF], SWIGLU_LIMIT)
        u = jnp.clip(gu[:, EXPERT_FF:], -SWIGLU_LIMIT, SWIGLU_LIMIT)
        act = (u + 1.0) * (g * jax.nn.sigmoid(SWIGLU_ALPHA * g))
        o_ref[...] = _pack_bf16_pairs(_dot(act.astype(BF16), wd_bf[...]) + bd_ref[0])

    @pl.when(jnp.logical_not(valid))
    def _():
        o_ref[...] = jnp.zeros_like(o_ref)


def _experts(xs, tile_expert, n_valid, w_gu, b_gu, w_down, b_down):
    n_rows = xs.shape[0]
    tm = EXPERT_TILE
    nt = n_rows // tm
    ne, d, ff2 = w_gu.shape
    ff = ff2 // 2
    grid_spec = pltpu.PrefetchScalarGridSpec(
        num_scalar_prefetch=2,
        grid=(nt,),
        in_specs=[pl.BlockSpec((tm, d // 2), lambda i, te, nv: (i, 0)),
                  pl.BlockSpec((1, d, ff2), lambda i, te, nv: (te[i], 0, 0)),
                  pl.BlockSpec((1, 1, ff2), lambda i, te, nv: (te[i], 0, 0)),
                  pl.BlockSpec((1, ff, d), lambda i, te, nv: (te[i], 0, 0)),
                  pl.BlockSpec((1, 1, d), lambda i, te, nv: (te[i], 0, 0))],
        out_specs=pl.BlockSpec((tm, d // 2), lambda i, te, nv: (i, 0)),
        scratch_shapes=[pltpu.VMEM((d, ff2), BF16), pltpu.VMEM((ff, d), BF16)],
    )
    return pl.pallas_call(
        _experts_kernel,
        grid_spec=grid_spec,
        out_shape=jax.ShapeDtypeStruct((n_rows, d // 2), U32),
        compiler_params=pltpu.CompilerParams(
            dimension_semantics=("arbitrary",), vmem_limit_bytes=VMEM_LIMIT),
        name="experts",
    )(tile_expert, n_valid, xs, w_gu, b_gu.reshape(ne, 1, ff2), w_down, b_down.reshape(ne, 1, d))


def _combine_kernel(tab_ref, ys_hbm, lp_ref, tg_ref, x1_ref, mod_ref, fn_ref, y_ref, buf, sem):
    def copy(local, glob, size):
        pltpu.make_async_copy(ys_hbm.at[pl.ds(glob, size)], buf.at[pl.ds(local, size)], sem).start()
    _segment_copies(tab_ref, copy)
    tg = tg_ref[0]
    wg = _pair_onehot(lp_ref, [tg[r:r + 1, :] for r in range(TOP_K)]).astype(BF16)
    _wait_rows(tab_ref[0, 0, TAB_REM], lambda size: pltpu.make_async_copy(
        ys_hbm.at[pl.ds(0, size)], buf.at[pl.ds(0, size)], sem).wait())
    nrow = PAIR_ROWS + tab_ref[0, 0, TAB_REM]
    rowi = lax.broadcasted_iota(jnp.int32, buf.shape, 0)
    yb = _unpack_bf16_pairs(jnp.where(rowi < nrow, buf[...], jnp.uint32(0)))
    y = _dot_tn(wg, yb)
    g2 = mod_ref[0][5:6]
    x2 = x1_ref[0] + g2 * y
    ms = jnp.mean(x2 * x2, axis=-1, keepdims=True)
    y_ref[0] = x2 * lax.rsqrt(ms + RMS_EPS) * fn_ref[...]


def _combine(ys, table, lpos, gates, x1, mod, mod_shared, final_norm):
    b, s, d = x1.shape
    tm = TOKEN_TILE
    nt = s // tm
    mod_map = (lambda i, j: (0, 0, 0)) if mod_shared else (lambda i, j: (i, 0, 0))
    tile = lambda shape: pl.BlockSpec(shape, lambda i, j: (i * nt + j, 0, 0))
    return pl.pallas_call(
        _combine_kernel,
        grid=(b, nt),
        in_specs=[pl.BlockSpec((1, 1, LANES), lambda i, j: (i * nt + j, 0, 0), memory_space=pltpu.SMEM),
                  pl.BlockSpec(memory_space=pl.ANY),
                  tile((1, 8, tm)), tile((1, 8, tm)),
                  pl.BlockSpec((1, tm, d), lambda i, j: (i, j, 0)),
                  pl.BlockSpec((1, 8, d), mod_map),
                  pl.BlockSpec((1, d), lambda i, j: (0, 0))],
        out_specs=pl.BlockSpec((1, tm, d), lambda i, j: (i, j, 0)),
        out_shape=jax.ShapeDtypeStruct((b, s, d), F32),
        scratch_shapes=[pltpu.VMEM((LOCAL_ROWS, d // 2), U32), pltpu.SemaphoreType.DMA(())],
        compiler_params=pltpu.CompilerParams(
            dimension_semantics=("arbitrary", "arbitrary"), vmem_limit_bytes=VMEM_LIMIT),
        name="combine",
    )(table, ys, lpos, gates, x1, mod, final_norm.reshape(1, d))


def _route_tables(meta, n_tiles_e):
    tm = EXPERT_TILE
    cnt = meta[:, :N_EXPERTS, 0]
    lstart = meta[:, N_EXPERTS:, 0]
    tot = jnp.sum(cnt, axis=0)
    tiles = (tot + tm - 1) // tm
    tile_end = jnp.cumsum(tiles)
    off = (tile_end - tiles) * tm
    dest = off[None, :] + jnp.cumsum(cnt, axis=0) - cnt
    rem = jnp.sum(cnt, axis=1, keepdims=True) - PAIR_ROWS
    table = jnp.concatenate([dest, cnt, lstart, jnp.broadcast_to(rem, cnt.shape)], axis=1).astype(jnp.int32)
    n_valid = tile_end[-1].astype(jnp.int32).reshape(1)
    pad_table = jnp.concatenate([off + tot, tiles * tm - tot, jnp.broadcast_to(n_valid, (2 * N_EXPERTS,))])
    pad_table = pad_table.astype(jnp.int32).reshape(1, LANES)
    tile_ids = jnp.minimum(jnp.arange(n_tiles_e), n_valid[0] - 1)
    tile_expert = jnp.sum((tile_end[None, :] <= tile_ids[:, None]).astype(jnp.int32), axis=1)
    return table.reshape(-1, 1, LANES), pad_table, tile_expert, n_valid


def _rope_tables(n_tok):
    rows = n_tok // GRID_W
    r, col = jnp.meshgrid(jnp.arange(rows), jnp.arange(GRID_W), indexing="ij")
    r = r.reshape(-1).astype(F32)
    col = col.reshape(-1).astype(F32)
    inv = 1.0 / (ROPE_THETA ** (jnp.arange(ROPE_AXIS_PAIRS, dtype=F32) / ROPE_AXIS_PAIRS))
    ang = jnp.concatenate([r[:, None] * inv, col[:, None] * inv], axis=-1)
    c64 = jnp.repeat(jnp.cos(ang), 2, axis=-1)
    sign = jnp.tile(jnp.array([-1.0, 1.0], F32), HEAD_DIM // 2)
    s64 = jnp.repeat(jnp.sin(ang), 2, axis=-1) * sign
    return (jnp.tile(c64, (1, ATT_HEADS)), jnp.tile(s64, (1, ATT_HEADS)),
            jnp.tile(c64, (1, KV_HEADS)), jnp.tile(s64, (1, KV_HEADS)))


def _prep_weights(w_in, w_gate_f, b_gate_f, w_gate_b, b_gate_b, gla_norm, q_norm, k_norm, w_out,
                  norm_mix, norm_ffn, w_router, b_router):
    d = w_in.shape[0]
    o_lr = 2 * W_QK + 2 * W_V
    o_q = o_lr + 2 * GLA_GATE_RANK
    o_k = o_q + W_ATT
    head_order = np.array([0, 4, 1, 5, 2, 6, 3, 7])
    perm = (head_order[:, None] * HEAD_DIM + np.arange(HEAD_DIM)[None, :]).reshape(-1)
    w_q = w_in[:, o_q:o_k][:, perm]
    lr_pad = jnp.zeros((d, LANES - 2 * GLA_GATE_RANK), w_in.dtype)
    w_in_p = jnp.concatenate([w_in[:, :o_lr], w_q, w_in[:, o_k:], w_in[:, o_lr:o_q], lr_pad], axis=1)
    wg = jnp.zeros((LANES, 2 * W_QK), F32)
    wg = wg.at[:GLA_GATE_RANK, :W_QK].set(w_gate_f)
    wg = wg.at[GLA_GATE_RANK:2 * GLA_GATE_RANK, W_QK:].set(w_gate_b)
    wg_hi, wg_lo = _split_bf16(wg)
    ones_q = jnp.asarray(np.kron(np.eye(ATT_HEADS), np.ones((HEAD_DIM, HEAD_DIM))), BF16)
    ones_k = jnp.asarray(np.kron(np.eye(KV_HEADS), np.ones((HEAD_DIM, HEAD_DIM))), BF16)
    w_out_p = jnp.concatenate([w_out[:W_V], w_out[W_V:][perm]], axis=0).astype(BF16)
    wr_hi, wr_lo = _split_bf16(jnp.pad(w_router, ((0, 0), (0, LANES - N_EXPERTS))))
    return {
        "norm_mix": norm_mix.reshape(1, d), "w_in": w_in_p.astype(BF16),
        "wg_hi": wg_hi, "wg_lo": wg_lo,
        "b_gate": jnp.concatenate([b_gate_f, b_gate_b]).reshape(1, 2 * W_QK),
        "ones_q": ones_q, "ones_k": ones_k,
        "q_norm": jnp.tile(q_norm, ATT_HEADS).reshape(1, W_ATT),
        "k_norm": jnp.tile(k_norm, KV_HEADS).reshape(1, W_KV),
        "gla_norm": gla_norm.reshape(1, GLA_DV), "w_out": w_out_p,
        "norm_ffn": norm_ffn.reshape(1, d), "wr_hi": wr_hi, "wr_lo": wr_lo,
        "b_router": b_router.reshape(N_EXPERTS, 1),
    }


def _mixer(x, mod, mod_shared, wp, rope, cache_k, cache_v, s0_f, s0_b):
    b, s, _ = x.shape
    qa, ka, va, ga, la_f, la_b, q, k, v, kc, vc = _premix(x, mod, mod_shared, wp, rope)
    o_f, sf = _gla(qa, ka, va, la_f, _state_to_blockdiag_t(s0_f), False)
    o_r, sb = _gla(qa, ka, va, la_b, _state_to_blockdiag_t(s0_b), True)
    o_att = _attention(q, k, v, cache_k, cache_v)
    x1, h2, lpos, gates, meta = _postmix(o_f, o_r, ga, o_att, x, mod, mod_shared, wp)
    return x1, h2, lpos, gates, meta, kc, vc, _blockdiag_t_to_state(sf), _blockdiag_t_to_state(sb)


def kernel(x_prompt, x_sample, c, cache_k, cache_v, state_gla_fwd, state_gla_bwd, c_ctx, w_ada, b_ada,
           norm_mix, w_in, w_gate_f, b_gate_f, w_gate_b, b_gate_b, gla_norm, q_norm, k_norm, w_out,
           norm_ffn, w_router, b_router, w_gu, b_gu, w_down, b_down, final_norm):
    bp, sp, d = x_prompt.shape
    bs, ss, _ = x_sample.shape
    assert w_ada.shape[0] == 1, "single-layer trunk"
    wp = _prep_weights(w_in[0], w_gate_f[0], b_gate_f[0], w_gate_b[0], b_gate_b[0], gla_norm[0],
                       q_norm[0], k_norm[0], w_out[0], norm_mix[0], norm_ffn[0], w_router[0], b_router[0])

    n_cond = -(-(1 + bs) // 8) * 8
    conds = jnp.zeros((n_cond, d), F32).at[0].set(c_ctx).at[1:1 + bs].set(c)
    mod = _adaln(conds, w_ada[0], b_ada[0]).reshape(n_cond, 6, d)
    mod = jnp.concatenate([mod, jnp.zeros((n_cond, 2, d), F32)], axis=1)
    mod_p, mod_s = mod[0:1], mod[1:1 + bs]

    zero_state = jnp.zeros((bp, GLA_HEADS, GLA_DK, GLA_DV), F32)
    x1p, h2p, lpp, tgp, metap, kc, vc, sf, sb = _mixer(x_prompt, mod_p, True, wp, None, None, None,
                                                       zero_state, zero_state)
    n_ctx = cache_k.shape[2]
    x1s, h2s, lps, tgs, metas, _, _, _, _ = _mixer(
        x_sample, mod_s, False, wp, _rope_tables(ss),
        cache_k[:, 0].reshape(bs, n_ctx, W_KV), cache_v[:, 0].reshape(bs, n_ctx, W_KV),
        state_gla_fwd[:, 0].astype(F32), state_gla_bwd[:, 0].astype(F32))

    n_p, n_s = bp * sp, bs * ss
    n_tok = n_p + n_s
    ntp = n_p // TOKEN_TILE
    h2 = jnp.concatenate([h2p.reshape(n_p, d), h2s.reshape(n_s, d)], axis=0)
    lpos = jnp.concatenate([lpp, lps], axis=0)
    worst = n_tok * TOP_K + (n_tok // TOKEN_TILE) * N_EXPERTS * (SUBLANES - 1) + N_EXPERTS * (EXPERT_TILE - SUBLANES)
    n_rows = -(-worst // EXPERT_TILE) * EXPERT_TILE
    table, pad_table, tile_expert, n_valid = _route_tables(
        jnp.concatenate([metap, metas], axis=0), n_rows // EXPERT_TILE)
    xs = _dispatch(h2, lpos, table, pad_table, n_rows)
    ys = _experts(xs, tile_expert, n_valid, w_gu[0], b_gu[0], w_down[0], b_down[0])
    y_prompt = _combine(ys, table[:ntp], lpp, tgp, x1p, mod_p, True, final_norm)
    y_sample = _combine(ys, table[ntp:], lps, tgs, x1s, mod_s, False, final_norm)

    new_cache_k = kc.reshape(bp, 1, sp, KV_HEADS, HEAD_DIM)
    new_cache_v = vc.reshape(bp, 1, sp, KV_HEADS, HEAD_DIM)
    return (y_prompt, y_sample, new_cache_k, new_cache_v, sf[:, None], sb[:, None])
```

```python
import functools

import numpy as np
import jax
import jax.numpy as jnp
from jax import lax
from jax.experimental import pallas as pl
from jax.experimental.pallas import tpu as pltpu

F32 = jnp.float32
BF16 = jnp.bfloat16

D_MODEL = 1024
GRID_W = 64
GLA_HEADS = 4
GLA_DV = 128
GLA_DK = 64
GLA_GATE_RANK = 16
GLA_GATE_NORM = 16.0
HEAD_DIM = 64
ATT_HEADS = 8
KV_HEADS = 2
ROPE_THETA = 10000.0
ROPE_AXIS_PAIRS = HEAD_DIM // 4
N_EXPERTS = 32
TOP_K = 4
EXPERT_FF = 1024
SWIGLU_LIMIT = 7.0
SWIGLU_ALPHA = 1.702
RMS_EPS = 1e-6

W_QK = GLA_HEADS * GLA_DK
W_V = GLA_HEADS * GLA_DV
W_ATT = ATT_HEADS * HEAD_DIM
W_KV = KV_HEADS * HEAD_DIM
LANES = 128
D_IN_PAD = 2 * W_QK + 2 * W_V + W_ATT + 2 * W_KV + LANES

PREMIX_TILE = 512
TOKEN_TILE = 256
GLA_BLOCK = 256
GLA_LEVELS = ((256, 32), (32, 8), (8, 1))
ATT_Q_TILE = 512
EXPERT_TILE = 512
VMEM_LIMIT = 56 * 1024 * 1024


def _split_bf16(x):
    hi = x.astype(BF16)
    lo = (x - hi.astype(F32)).astype(BF16)
    return hi, lo


def _dot(a, b):
    return jnp.dot(a, b, preferred_element_type=F32)


def _dot_nt(a, b):
    return lax.dot_general(a, b, (((1,), (1,)), ((), ())), preferred_element_type=F32)


def _dot_tn(a, b):
    return lax.dot_general(a, b, (((0,), (0,)), ((), ())), preferred_element_type=F32)


U32 = jnp.uint32


def _pack_bf16_pairs(x):
    w = x.shape[1] // 2
    bits = lambda t: lax.bitcast_convert_type(t.astype(BF16).astype(F32), U32)
    return (bits(x[:, :w]) >> 16) | (bits(x[:, w:]) & jnp.uint32(0xFFFF0000))


def _unpack_bf16_pairs(w):
    lo = lax.bitcast_convert_type(w << 16, F32)
    hi = lax.bitcast_convert_type(w & jnp.uint32(0xFFFF0000), F32)
    return jnp.concatenate([lo, hi], axis=1).astype(BF16)


def _adaln_kernel(c_ref, w_ref, b_ref, o_ref):
    c = c_ref[...]
    a = c * jax.nn.sigmoid(c)
    ah, al = _split_bf16(a)
    wh, wl = _split_bf16(w_ref[...])
    o_ref[...] = _dot(ah, wh) + _dot(al, wh) + _dot(ah, wl) + b_ref[...]


def _adaln(conds, w_ada, b_ada):
    m, d = conds.shape
    n = w_ada.shape[1]
    tn = 512
    return pl.pallas_call(
        _adaln_kernel,
        grid=(n // tn,),
        in_specs=[pl.BlockSpec((m, d), lambda j: (0, 0)),
                  pl.BlockSpec((d, tn), lambda j: (0, j)),
                  pl.BlockSpec((1, tn), lambda j: (0, j))],
        out_specs=pl.BlockSpec((m, tn), lambda j: (0, j)),
        out_shape=jax.ShapeDtypeStruct((m, n), F32),
        name="adaln",
    )(conds, w_ada, b_ada.reshape(1, n))


def _rope_rotate(x, c, s):
    n = x.shape[-1]
    lane = lax.broadcasted_iota(jnp.int32, x.shape, 1)
    partner = jnp.where((lane & 1) == 0, pltpu.roll(x, n - 1, 1), pltpu.roll(x, 1, 1))
    return x * c + partner * s


def _premix_kernel(*refs, use_rope):
    (x_ref, mod_ref, nmix_ref, win_ref, wgh_ref, wgl_ref, bg_ref, bq_ref, bk_ref,
     qn_ref, kn_ref) = refs[:11]
    n_in = 11
    if use_rope:
        cq_ref, sq_ref, ck_ref, sk_ref = refs[11:15]
        n_in = 15
    (qa_o, ka_o, va_o, ga_o, laf_o, lab_o, q_o, k_o, v_o, kc_o, vc_o) = refs[n_in:]

    x = x_ref[0]
    mod = mod_ref[0]
    sh1 = mod[0:1]
    sc1 = mod[1:2]
    ms = jnp.mean(x * x, axis=-1, keepdims=True)
    h = x * lax.rsqrt(ms + RMS_EPS) * nmix_ref[...]
    h = h * (1.0 + sc1) + sh1
    proj = _dot(h.astype(BF16), win_ref[...])

    o = 0
    qa_o[0] = proj[:, o:o + W_QK] * (GLA_DK ** -0.5); o += W_QK
    ka_o[0] = proj[:, o:o + W_QK]; o += W_QK
    va_o[0] = proj[:, o:o + W_V]; o += W_V
    ga_o[0] = proj[:, o:o + W_V]; o += W_V
    qb = proj[:, o:o + W_ATT]; o += W_ATT
    kb = proj[:, o:o + W_KV]; o += W_KV
    vb = proj[:, o:o + W_KV]; o += W_KV
    lr = proj[:, o:o + LANES]

    lh, ll = _split_bf16(lr)
    xg = _dot(lh, wgh_ref[...]) + _dot(ll, wgh_ref[...]) + _dot(lh, wgl_ref[...]) + bg_ref[...]
    la = (jnp.minimum(xg, 0.0) - jnp.log1p(jnp.exp(-jnp.abs(xg)))) * (1.0 / GLA_GATE_NORM)
    laf_o[0] = la[:, :W_QK]
    lab_o[0] = la[:, W_QK:]

    def head_norm(t, ones_ref, g_ref):
        sh, sl = _split_bf16(t * t)
        msq = (_dot(sh, ones_ref[...]) + _dot(sl, ones_ref[...])) * (1.0 / HEAD_DIM)
        return t * lax.rsqrt(msq + RMS_EPS) * g_ref[...]

    qh = head_norm(qb, bq_ref, qn_ref)
    kh = head_norm(kb, bk_ref, kn_ref)
    kc_o[0] = kh
    vc_o[0] = vb
    if use_rope:
        qh = _rope_rotate(qh, cq_ref[...], sq_ref[...])
        kr = _rope_rotate(kh, ck_ref[...], sk_ref[...])
    else:
        kr = kh
    q_o[0] = (qh * (HEAD_DIM ** -0.5)).astype(BF16)
    k_o[0] = kr.astype(BF16)
    v_o[0] = vb.astype(BF16)


def _premix(x, mod, mod_shared, wp, rope):
    b, s, d = x.shape
    tm = min(PREMIX_TILE, s)
    use_rope = rope is not None
    const = lambda shape: pl.BlockSpec(shape, lambda i, j: (0,) * len(shape))
    tok = lambda w: pl.BlockSpec((1, tm, w), lambda i, j: (i, j, 0))
    mod_map = (lambda i, j: (0, 0, 0)) if mod_shared else (lambda i, j: (i, 0, 0))
    in_specs = [tok(d), pl.BlockSpec((1, 8, d), mod_map), const((1, d)),
                const((d, D_IN_PAD)), const((LANES, 2 * W_QK)), const((LANES, 2 * W_QK)),
                const((1, 2 * W_QK)), const((W_ATT, W_ATT)), const((W_KV, W_KV)),
                const((1, W_ATT)), const((1, W_KV))]
    args = [x, mod, wp["norm_mix"], wp["w_in"], wp["wg_hi"], wp["wg_lo"], wp["b_gate"],
            wp["ones_q"], wp["ones_k"], wp["q_norm"], wp["k_norm"]]
    if use_rope:
        seq = lambda w: pl.BlockSpec((tm, w), lambda i, j: (j, 0))
        in_specs += [seq(W_ATT), seq(W_ATT), seq(W_KV), seq(W_KV)]
        args += list(rope)
    widths = [(W_QK, F32), (W_QK, F32), (W_V, F32), (W_V, F32), (W_QK, F32), (W_QK, F32),
              (W_ATT, BF16), (W_KV, BF16), (W_KV, BF16), (W_KV, F32), (W_KV, F32)]
    return pl.pallas_call(
        functools.partial(_premix_kernel, use_rope=use_rope),
        grid=(b, s // tm),
        in_specs=in_specs,
        out_specs=[tok(w) for w, _ in widths],
        out_shape=[jax.ShapeDtypeStruct((b, s, w), dt) for w, dt in widths],
        compiler_params=pltpu.CompilerParams(
            dimension_semantics=("parallel", "parallel"), vmem_limit_bytes=VMEM_LIMIT),
        name="premix_rope" if use_rope else "premix",
    )(*args)


def _bcast_rows(x, n_par, p_rows, row):
    w = x.shape[-1]
    r = x.reshape(n_par, p_rows, w)[:, row:row + 1, :]
    return jnp.broadcast_to(r, (n_par, p_rows, w)).reshape(n_par * p_rows, w)


def _gla_kernel(q_ref, k_ref, v_ref, la_ref, s0_ref, o_ref, sf_ref, st_scr, *, rows, levels, reverse):
    blk = pl.program_id(1)

    @pl.when(blk == 0)
    def _():
        st_scr[...] = s0_ref[0]

    q = q_ref[0]
    k = k_ref[0]
    la = la_ref[0]
    vb = v_ref[0].astype(BF16)

    ri = lax.broadcasted_iota(jnp.int32, (rows, rows), 0)
    ci = lax.broadcasted_iota(jnp.int32, (rows, rows), 1)
    tri = jnp.where((ri <= ci) if reverse else (ri >= ci), 1.0, 0.0).astype(BF16)
    hi = la.astype(BF16)
    r1 = la - hi.astype(F32)
    mid = r1.astype(BF16)
    lo = (r1 - mid.astype(F32)).astype(BF16)
    cum = _dot(tri, hi) + _dot(tri, mid) + _dot(tri, lo)

    ridx = lax.broadcasted_iota(jnp.int32, (rows, W_QK), 0)
    s_acc = [None] * GLA_HEADS
    for (par, sub) in levels:
        groups = par // sub
        n_par = rows // par
        pi = (ridx % par) // sub
        edge = sub - 1 if reverse else 0
        if sub > 1:
            own = _bcast_rows(cum, rows // sub, sub, edge)
            qt = q * jnp.exp(cum - own)
        else:
            qt = q
        qps, kps = [], []
        for p in range(groups):
            if sub > 1 and p == (groups - 1 if reverse else 0):
                continue
            cp = _bcast_rows(cum, n_par, par, p * sub + edge)
            if reverse:
                km = (pi > p) if sub > 1 else (pi >= p)
            else:
                km = (pi < p) if sub > 1 else (pi <= p)
            kps.append(jnp.where(km, k * jnp.exp(cp - cum), 0.0).astype(BF16))
            qps.append(jnp.where(pi == p, qt, 0.0).astype(BF16))
        ng = len(qps)
        lane_c = lax.broadcasted_iota(jnp.int32, (rows, LANES * ng), 1)
        same_parent = (ri // par) == (ci // par)
        for hp in range(GLA_HEADS // 2):
            qc = jnp.concatenate([a[:, LANES * hp:LANES * (hp + 1)] for a in qps], axis=1)
            kc = jnp.concatenate([a[:, LANES * hp:LANES * (hp + 1)] for a in kps], axis=1)
            for hh in range(2):
                qh = jnp.where(((lane_c % LANES) // GLA_DK) == hh, qc, jnp.zeros_like(qc))
                sl = _dot_nt(qh, kc)
                if par < rows:
                    sl = jnp.where(same_parent, sl, 0.0)
                h = 2 * hp + hh
                s_acc[h] = sl if s_acc[h] is None else s_acc[h] + sl

    q0 = (q * jnp.exp(cum)).astype(BF16)
    far = 0 if reverse else rows - 1
    last = cum[far:far + 1, :]
    kd = (k * jnp.exp(last - cum)).astype(BF16)
    br = lax.broadcasted_iota(jnp.int32, (2 * GLA_DV, 2 * GLA_DK), 0)
    bc = lax.broadcasted_iota(jnp.int32, (2 * GLA_DV, 2 * GLA_DK), 1)
    blockdiag = (br // GLA_DV) == (bc // GLA_DK)
    for hp in range(GLA_HEADS // 2):
        st = st_scr[hp]
        o_inter = _dot_nt(q0[:, LANES * hp:LANES * (hp + 1)], st.astype(BF16))
        o_intra = jnp.concatenate(
            [_dot(s_acc[2 * hp + hh].astype(BF16),
                  vb[:, GLA_DV * (2 * hp + hh):GLA_DV * (2 * hp + hh + 1)]) for hh in range(2)],
            axis=1)
        o_ref[0, :, 2 * GLA_DV * hp:2 * GLA_DV * (hp + 1)] = o_inter + o_intra
        upd = _dot_tn(vb[:, 2 * GLA_DV * hp:2 * GLA_DV * (hp + 1)],
                      kd[:, LANES * hp:LANES * (hp + 1)])
        st_scr[hp] = (jnp.exp(last[:, LANES * hp:LANES * (hp + 1)]) * st
                      + jnp.where(blockdiag, upd, 0.0))

    @pl.when(blk == pl.num_programs(1) - 1)
    def _():
        sf_ref[0] = st_scr[...]


def _gla(q, k, v, la, s0t, reverse):
    n, l, _ = q.shape
    rows = min(GLA_BLOCK, l)
    nb = l // rows
    levels = tuple((min(p, rows), s) for p, s in GLA_LEVELS)
    order = (lambda j: nb - 1 - j) if reverse else (lambda j: j)
    tok = lambda w: pl.BlockSpec((1, rows, w), lambda i, j: (i, order(j), 0))
    st_spec = pl.BlockSpec((1, 2, 2 * GLA_DV, 2 * GLA_DK), lambda i, j: (i, 0, 0, 0))
    return pl.pallas_call(
        functools.partial(_gla_kernel, rows=rows, levels=levels, reverse=reverse),
        grid=(n, nb),
        in_specs=[tok(W_QK), tok(W_QK), tok(W_V), tok(W_QK), st_spec],
        out_specs=[tok(W_V), st_spec],
        out_shape=[jax.ShapeDtypeStruct((n, l, W_V), F32),
                   jax.ShapeDtypeStruct(s0t.shape, F32)],
        scratch_shapes=[pltpu.VMEM((2, 2 * GLA_DV, 2 * GLA_DK), F32)],
        compiler_params=pltpu.CompilerParams(
            dimension_semantics=("parallel", "arbitrary"), vmem_limit_bytes=VMEM_LIMIT),
        name="gla_bwd" if reverse else "gla_fwd",
    )(q, k, v, la, s0t)


def _state_to_blockdiag_t(s):
    n = s.shape[0]
    st = jnp.swapaxes(s, -1, -2).reshape(n, 2, 2, GLA_DV, GLA_DK)
    z = jnp.zeros_like(st[:, :, 0])
    top = jnp.concatenate([st[:, :, 0], z], axis=-1)
    bot = jnp.concatenate([z, st[:, :, 1]], axis=-1)
    return jnp.concatenate([top, bot], axis=-2)


def _blockdiag_t_to_state(sb):
    n = sb.shape[0]
    h0 = sb[:, :, :GLA_DV, :GLA_DK]
    h1 = sb[:, :, GLA_DV:, GLA_DK:]
    st = jnp.stack([h0, h1], axis=2).reshape(n, GLA_HEADS, GLA_DV, GLA_DK)
    return jnp.swapaxes(st, -1, -2)


def _attn_kernel(*refs, s_self, n_ctx):
    if n_ctx:
        q_ref, k_ref, v_ref, ck_ref, cv_ref, o_ref, kg_scr, v_scr = refs
    else:
        q_ref, k_ref, v_ref, o_ref, kg_scr, v_scr = refs

    @pl.when(pl.program_id(1) == 0)
    def _():
        def put(kk, vv, start, n):
            lane = lax.broadcasted_iota(jnp.int32, kk.shape, 1)
            kg_scr[0, start:start + n, :] = jnp.where(lane < HEAD_DIM, kk, jnp.zeros_like(kk))
            kg_scr[1, start:start + n, :] = jnp.where(lane >= HEAD_DIM, kk, jnp.zeros_like(kk))
            v_scr[start:start + n, :] = vv
        put(k_ref[0], v_ref[0], 0, s_self)
        if n_ctx:
            put(ck_ref[0].astype(BF16), cv_ref[0].astype(BF16), s_self, n_ctx)

    q = q_ref[0]
    vv = v_scr[...]
    for m in range(ATT_HEADS // KV_HEADS):
        qm = q[:, LANES * m:LANES * (m + 1)]
        og = []
        for g in range(KV_HEADS):
            s = _dot_nt(qm, kg_scr[g])
            mx = jnp.max(s, axis=-1, keepdims=True)
            p = jnp.exp(s - mx)
            l = jnp.sum(p, axis=-1, keepdims=True)
            og.append(_dot(p.astype(BF16), vv) / l)
        lane = lax.broadcasted_iota(jnp.int32, og[0].shape, 1)
        o_ref[0, :, LANES * m:LANES * (m + 1)] = jnp.where(lane < HEAD_DIM, og[0], og[1]).astype(BF16)


def _attention(q, k, v, cache_k=None, cache_v=None):
    b, s, _ = q.shape
    n_ctx = 0 if cache_k is None else cache_k.shape[1]
    tq = min(ATT_Q_TILE, s)
    sk = s + n_ctx
    full = lambda n, w: pl.BlockSpec((1, n, w), lambda i, j: (i, 0, 0))
    in_specs = [pl.BlockSpec((1, tq, W_ATT), lambda i, j: (i, j, 0)), full(s, W_KV), full(s, W_KV)]
    args = [q, k, v]
    if n_ctx:
        in_specs += [full(n_ctx, W_KV), full(n_ctx, W_KV)]
        args += [cache_k, cache_v]
    return pl.pallas_call(
        functools.partial(_attn_kernel, s_self=s, n_ctx=n_ctx),
        grid=(b, s // tq),
        in_specs=in_specs,
        out_specs=pl.BlockSpec((1, tq, W_ATT), lambda i, j: (i, j, 0)),
        out_shape=jax.ShapeDtypeStruct((b, s, W_ATT), BF16),
        scratch_shapes=[pltpu.VMEM((KV_HEADS, sk, W_KV), BF16), pltpu.VMEM((sk, W_KV), BF16)],
        compiler_params=pltpu.CompilerParams(
            dimension_semantics=("parallel", "arbitrary"), vmem_limit_bytes=VMEM_LIMIT),
        name="attention_ctx" if n_ctx else "attention",
    )(*args)


def _postmix_kernel(of_ref, or_ref, ga_ref, oa_ref, x_ref, mod_ref, gn_ref, wout_ref, nffn_ref,
                    wrh_ref, wrl_ref, br_ref, x1_o, h2_o, lp_o, tg_o, meta_o):
    mod = mod_ref[0]
    g1, sh2, sc2 = mod[2:3], mod[3:4], mod[4:5]
    og = of_ref[0] + or_ref[0]
    ga = ga_ref[0]
    parts = []
    for h in range(GLA_HEADS):
        blk = og[:, GLA_DV * h:GLA_DV * (h + 1)]
        ms = jnp.mean(blk * blk, axis=-1, keepdims=True)
        gh = ga[:, GLA_DV * h:GLA_DV * (h + 1)]
        parts.append((blk * lax.rsqrt(ms + RMS_EPS) * gn_ref[...] * (gh * jax.nn.sigmoid(gh))).astype(BF16))
    mix = jnp.concatenate(parts + [oa_ref[0]], axis=1)
    mo = _dot(mix, wout_ref[...])
    x1 = x_ref[0] + g1 * mo
    x1_o[0] = x1
    ms = jnp.mean(x1 * x1, axis=-1, keepdims=True)
    h2 = x1 * lax.rsqrt(ms + RMS_EPS) * nffn_ref[...]
    h2 = h2 * (1.0 + sc2) + sh2
    hh, hl = _split_bf16(h2)
    h2_o[0] = hh

    lg = _dot(hh, wrh_ref[...]) + _dot(hl, wrh_ref[...]) + _dot(hh, wrl_ref[...])
    lt = jnp.transpose(lg)[:N_EXPERTS] + br_ref[...]
    tm = lt.shape[1]
    eidx = lax.broadcasted_iota(jnp.int32, lt.shape, 0)
    vals, sels = [], []
    for _ in range(TOP_K):
        mx = jnp.max(lt, axis=0, keepdims=True)
        idx = jnp.min(jnp.where(lt == mx, eidx, N_EXPERTS), axis=0, keepdims=True)
        sel = eidx == idx
        lt = jnp.where(sel, -jnp.inf, lt)
        vals.append(mx)
        sels.append(sel)
    ws = [jnp.exp(vv - vals[0]) for vv in vals]
    tot = ws[0] + ws[1] + ws[2] + ws[3]
    tg_o[0] = jnp.concatenate([w / tot for w in ws] + [jnp.zeros((8 - TOP_K, tm), F32)], axis=0)

    onehot = sum(jnp.where(s, 1.0, 0.0) for s in sels)
    ti = lax.broadcasted_iota(jnp.int32, (tm, tm), 0)
    tj = lax.broadcasted_iota(jnp.int32, (tm, tm), 1)
    rank = _dot(onehot.astype(BF16), jnp.where(ti < tj, 1.0, 0.0).astype(BF16))
    cnt = jnp.sum(onehot, axis=1, keepdims=True)
    cnt = jnp.floor((cnt + 7.0) * 0.125) * 8.0
    ei = lax.broadcasted_iota(jnp.int32, (N_EXPERTS, N_EXPERTS), 0)
    ej = lax.broadcasted_iota(jnp.int32, (N_EXPERTS, N_EXPERTS), 1)
    cnt_b = jnp.broadcast_to(cnt, (N_EXPERTS, tm))
    seg = _dot(jnp.where(ej < ei, 1.0, 0.0).astype(BF16), cnt_b.astype(BF16))
    base = seg + rank
    lpos = [jnp.sum(jnp.where(s, base, 0.0), axis=0, keepdims=True).astype(jnp.int32) for s in sels]
    lp_o[0] = jnp.concatenate(lpos + [jnp.zeros((8 - TOP_K, tm), jnp.int32)], axis=0)
    meta_o[0] = jnp.concatenate([cnt_b[:, :LANES], seg[:, :LANES]], axis=0).astype(jnp.int32)


def _postmix(o_f, o_r, ga, o_att, x, mod, mod_shared, wp):
    b, s, d = x.shape
    tm = min(TOKEN_TILE, s)
    nt = s // tm
    const = lambda shape: pl.BlockSpec(shape, lambda i, j: (0,) * len(shape))
    tok = lambda w: pl.BlockSpec((1, tm, w), lambda i, j: (i, j, 0))
    mod_map = (lambda i, j: (0, 0, 0)) if mod_shared else (lambda i, j: (i, 0, 0))
    lane_tok = pl.BlockSpec((1, 8, tm), lambda i, j: (i * nt + j, 0, 0))
    return pl.pallas_call(
        _postmix_kernel,
        grid=(b, nt),
        in_specs=[tok(W_V), tok(W_V), tok(W_V), tok(W_ATT), tok(d), pl.BlockSpec((1, 8, d), mod_map),
                  const((1, GLA_DV)), const((d, d)), const((1, d)),
                  const((d, LANES)), const((d, LANES)), const((N_EXPERTS, 1))],
        out_specs=[tok(d), tok(d), lane_tok, lane_tok,
                   pl.BlockSpec((1, 2 * N_EXPERTS, LANES), lambda i, j: (i * nt + j, 0, 0))],
        out_shape=[jax.ShapeDtypeStruct((b, s, d), F32), jax.ShapeDtypeStruct((b, s, d), BF16),
                   jax.ShapeDtypeStruct((b * nt, 8, tm), jnp.int32),
                   jax.ShapeDtypeStruct((b * nt, 8, tm), F32),
                   jax.ShapeDtypeStruct((b * nt, 2 * N_EXPERTS, LANES), jnp.int32)],
        compiler_params=pltpu.CompilerParams(
            dimension_semantics=("parallel", "parallel"), vmem_limit_bytes=VMEM_LIMIT),
        name="postmix",
    )(o_f, o_r, ga, o_att, x, mod, wp["gla_norm"], wp["w_out"], wp["norm_ffn"],
      wp["wr_hi"], wp["wr_lo"], wp["b_router"])


SUBLANES = 8
SEG_PIECES = tuple(TOKEN_TILE >> b for b in range(6))
PAIR_ROWS = TOKEN_TILE * TOP_K
LOCAL_ROWS = PAIR_ROWS + N_EXPERTS * SUBLANES
TAB_REM = 3 * N_EXPERTS


def _pieces(n, emit):
    for size in SEG_PIECES:
        done = n & ~(2 * size - 1)
        pl.when((n & size) != 0)(functools.partial(emit, done, size))


def _segment_copies(tab_ref, copy):
    def body(e, carry):
        dst0 = tab_ref[0, 0, e]
        n = tab_ref[0, 0, N_EXPERTS + e]
        src0 = tab_ref[0, 0, 2 * N_EXPERTS + e]
        _pieces(n, lambda done, size: copy(pl.multiple_of(src0 + done, SUBLANES),
                                           pl.multiple_of(dst0 + done, SUBLANES), size))
        return carry
    lax.fori_loop(0, N_EXPERTS, body, 0)


def _wait_rows(rem, wait_piece):
    wait_piece(PAIR_ROWS)
    _pieces(rem, lambda done, size: wait_piece(size))


def _pair_onehot(lp_ref, weights=None):
    tm = lp_ref.shape[-1]
    rows = lax.broadcasted_iota(jnp.int32, (LOCAL_ROWS, tm), 0)
    acc = jnp.zeros((LOCAL_ROWS, tm), F32)
    for r in range(TOP_K):
        w = 1.0 if weights is None else weights[r]
        acc = jnp.where(rows == lp_ref[0, r:r + 1, :], w, acc)
    return acc


def _dispatch_kernel(tab_ref, pad_ref, h_ref, lp_ref, xs_hbm, xl, zbuf, rem_prev, sem, zsem):
    i = pl.program_id(0)
    slot = i % 2
    perm = _pair_onehot(lp_ref).astype(BF16)
    xl[slot] = _pack_bf16_pairs(_dot(perm, h_ref[...]))

    def copy(local, glob, size):
        pltpu.make_async_copy(xl.at[slot, pl.ds(local, size)], xs_hbm.at[pl.ds(glob, size)],
                              sem.at[slot]).start()
    _segment_copies(tab_ref, copy)

    def wait_slot(s, rem):
        _wait_rows(rem, lambda size: pltpu.make_async_copy(
            xl.at[s, pl.ds(0, size)], xs_hbm.at[pl.ds(0, size)], sem.at[s]).wait())

    @pl.when(i > 0)
    def _():
        wait_slot(1 - slot, rem_prev[0])
    rem_prev[0] = tab_ref[0, 0, TAB_REM]

    @pl.when(i == pl.num_programs(0) - 1)
    def _():
        wait_slot(slot, tab_ref[0, 0, TAB_REM])
        zbuf[...] = jnp.zeros_like(zbuf)

        def pads(wait):
            def body(e, carry):
                start = pad_ref[0, e]
                def emit(done, size):
                    cp = pltpu.make_async_copy(
                        zbuf.at[pl.ds(0, size)],
                        xs_hbm.at[pl.ds(pl.multiple_of(start + done, SUBLANES), size)], zsem)
                    cp.wait() if wait else cp.start()
                _pieces(pad_ref[0, N_EXPERTS + e], emit)
                return carry
            lax.fori_loop(0, N_EXPERTS, body, 0)
        n_tiles = xs_hbm.shape[0] // EXPERT_TILE

        def tail(wait):
            def body(t, carry):
                cp = pltpu.make_async_copy(
                    zbuf, xs_hbm.at[pl.ds(pl.multiple_of(t * EXPERT_TILE, EXPERT_TILE), EXPERT_TILE)], zsem)
                cp.wait() if wait else cp.start()
                return carry
            lax.fori_loop(pad_ref[0, 2 * N_EXPERTS], n_tiles, body, 0)
        pads(False)
        tail(False)
        pads(True)
        tail(True)


def _dispatch(h2, lpos, table, pad_table, n_rows):
    t, d = h2.shape
    tm = TOKEN_TILE
    nt = t // tm
    return pl.pallas_call(
        _dispatch_kernel,
        grid=(nt,),
        in_specs=[pl.BlockSpec((1, 1, LANES), lambda i: (i, 0, 0), memory_space=pltpu.SMEM),
                  pl.BlockSpec((1, LANES), lambda i: (0, 0), memory_space=pltpu.SMEM),
                  pl.BlockSpec((tm, d), lambda i: (i, 0)),
                  pl.BlockSpec((1, 8, tm), lambda i: (i, 0, 0))],
        out_specs=pl.BlockSpec(memory_space=pl.ANY),
        out_shape=jax.ShapeDtypeStruct((n_rows, d // 2), U32),
        scratch_shapes=[pltpu.VMEM((2, LOCAL_ROWS, d // 2), U32), pltpu.VMEM((EXPERT_TILE, d // 2), U32),
                        pltpu.SMEM((1,), jnp.int32),
                        pltpu.SemaphoreType.DMA((2,)), pltpu.SemaphoreType.DMA(())],
        compiler_params=pltpu.CompilerParams(
            dimension_semantics=("arbitrary",), vmem_limit_bytes=VMEM_LIMIT),
        name="dispatch",
    )(table, pad_table, h2, lpos)


def _experts_kernel(te_ref, nv_ref, xs_ref, wgu_ref, bgu_ref, wd_ref, bd_ref, o_ref, wgu_bf, wd_bf):
    i = pl.program_id(0)
    valid = i < nv_ref[0]
    new_expert = jnp.logical_or(i == 0, te_ref[i] != te_ref[jnp.maximum(i - 1, 0)])

    @pl.when(jnp.logical_and(valid, new_expert))
    def _():
        wgu_bf[...] = wgu_ref[0].astype(BF16)
        wd_bf[...] = wd_ref[0].astype(BF16)

    @pl.when(valid)
    def _():
        x = _unpack_bf16_pairs(xs_ref[...])
        gu = _dot(x, wgu_bf[...]) + bgu_ref[0]
        g = jnp.minimum(gu[:, :EXPERT_FF], SWIGLU_LIMIT)
        u = jnp.clip(gu[:, EXPERT_FF:], -SWIGLU_LIMIT, SWIGLU_LIMIT)
        act = (u + 1.0) * (g * jax.nn.sigmoid(SWIGLU_ALPHA * g))
        o_ref[...] = _pack_bf16_pairs(_dot(act.astype(BF16), wd_bf[...]) + bd_ref[0])

    @pl.when(jnp.logical_not(valid))
    def _():
        o_ref[...] = jnp.zeros_like(o_ref)


def _experts(xs, tile_expert, n_valid, w_gu, b_gu, w_down, b_down):
    n_rows = xs.shape[0]
    tm = EXPERT_TILE
    nt = n_rows // tm
    ne, d, ff2 = w_gu.shape
    ff = ff2 // 2
    grid_spec = pltpu.PrefetchScalarGridSpec(
        num_scalar_prefetch=2,
        grid=(nt,),
        in_specs=[pl.BlockSpec((tm, d // 2), lambda i, te, nv: (i, 0)),
                  pl.BlockSpec((1, d, ff2), lambda i, te, nv: (te[i], 0, 0)),
                  pl.BlockSpec((1, 1, ff2), lambda i, te, nv: (te[i], 0, 0)),
                  pl.BlockSpec((1, ff, d), lambda i, te, nv: (te[i], 0, 0)),
                  pl.BlockSpec((1, 1, d), lambda i, te, nv: (te[i], 0, 0))],
        out_specs=pl.BlockSpec((tm, d // 2), lambda i, te, nv: (i, 0)),
        scratch_shapes=[pltpu.VMEM((d, ff2), BF16), pltpu.VMEM((ff, d), BF16)],
    )
    return pl.pallas_call(
        _experts_kernel,
        grid_spec=grid_spec,
        out_shape=jax.ShapeDtypeStruct((n_rows, d // 2), U32),
        compiler_params=pltpu.CompilerParams(
            dimension_semantics=("arbitrary",), vmem_limit_bytes=VMEM_LIMIT),
        name="experts",
    )(tile_expert, n_valid, xs, w_gu, b_gu.reshape(ne, 1, ff2), w_down, b_down.reshape(ne, 1, d))


def _combine_kernel(tab_ref, ys_hbm, lp_ref, tg_ref, x1_ref, mod_ref, fn_ref, y_ref, buf, sem):
    def copy(local, glob, size):
        pltpu.make_async_copy(ys_hbm.at[pl.ds(glob, size)], buf.at[pl.ds(local, size)], sem).start()
    _segment_copies(tab_ref, copy)
    tg = tg_ref[0]
    wg = _pair_onehot(lp_ref, [tg[r:r + 1, :] for r in range(TOP_K)]).astype(BF16)
    _wait_rows(tab_ref[0, 0, TAB_REM], lambda size: pltpu.make_async_copy(
        ys_hbm.at[pl.ds(0, size)], buf.at[pl.ds(0, size)], sem).wait())
    nrow = PAIR_ROWS + tab_ref[0, 0, TAB_REM]
    rowi = lax.broadcasted_iota(jnp.int32, buf.shape, 0)
    yb = _unpack_bf16_pairs(jnp.where(rowi < nrow, buf[...], jnp.uint32(0)))
    y = _dot_tn(wg, yb)
    g2 = mod_ref[0][5:6]
    x2 = x1_ref[0] + g2 * y
    ms = jnp.mean(x2 * x2, axis=-1, keepdims=True)
    y_ref[0] = x2 * lax.rsqrt(ms + RMS_EPS) * fn_ref[...]


def _combine(ys, table, lpos, gates, x1, mod, mod_shared, final_norm):
    b, s, d = x1.shape
    tm = TOKEN_TILE
    nt = s // tm
    mod_map = (lambda i, j: (0, 0, 0)) if mod_shared else (lambda i, j: (i, 0, 0))
    tile = lambda shape: pl.BlockSpec(shape, lambda i, j: (i * nt + j, 0, 0))
    return pl.pallas_call(
        _combine_kernel,
        grid=(b, nt),
        in_specs=[pl.BlockSpec((1, 1, LANES), lambda i, j: (i * nt + j, 0, 0), memory_space=pltpu.SMEM),
                  pl.BlockSpec(memory_space=pl.ANY),
                  tile((1, 8, tm)), tile((1, 8, tm)),
                  pl.BlockSpec((1, tm, d), lambda i, j: (i, j, 0)),
                  pl.BlockSpec((1, 8, d), mod_map),
                  pl.BlockSpec((1, d), lambda i, j: (0, 0))],
        out_specs=pl.BlockSpec((1, tm, d), lambda i, j: (i, j, 0)),
        out_shape=jax.ShapeDtypeStruct((b, s, d), F32),
        scratch_shapes=[pltpu.VMEM((LOCAL_ROWS, d // 2), U32), pltpu.SemaphoreType.DMA(())],
        compiler_params=pltpu.CompilerParams(
            dimension_semantics=("arbitrary", "arbitrary"), vmem_limit_bytes=VMEM_LIMIT),
        name="combine",
    )(table, ys, lpos, gates, x1, mod, final_norm.reshape(1, d))


def _route_tables(meta, n_tiles_e):
    tm = EXPERT_TILE
    cnt = meta[:, :N_EXPERTS, 0]
    lstart = meta[:, N_EXPERTS:, 0]
    tot = jnp.sum(cnt, axis=0)
    tiles = (tot + tm - 1) // tm
    tile_end = jnp.cumsum(tiles)
    off = (tile_end - tiles) * tm
    dest = off[None, :] + jnp.cumsum(cnt, axis=0) - cnt
    rem = jnp.sum(cnt, axis=1, keepdims=True) - PAIR_ROWS
    table = jnp.concatenate([dest, cnt, lstart, jnp.broadcast_to(rem, cnt.shape)], axis=1).astype(jnp.int32)
    n_valid = tile_end[-1].astype(jnp.int32).reshape(1)
    pad_table = jnp.concatenate([off + tot, tiles * tm - tot, jnp.broadcast_to(n_valid, (2 * N_EXPERTS,))])
    pad_table = pad_table.astype(jnp.int32).reshape(1, LANES)
    tile_ids = jnp.minimum(jnp.arange(n_tiles_e), n_valid[0] - 1)
    tile_expert = jnp.sum((tile_end[None, :] <= tile_ids[:, None]).astype(jnp.int32), axis=1)
    return table.reshape(-1, 1, LANES), pad_table, tile_expert, n_valid


def _rope_tables(n_tok):
    rows = n_tok // GRID_W
    r, col = jnp.meshgrid(jnp.arange(rows), jnp.arange(GRID_W), indexing="ij")
    r = r.reshape(-1).astype(F32)
    col = col.reshape(-1).astype(F32)
    inv = 1.0 / (ROPE_THETA ** (jnp.arange(ROPE_AXIS_PAIRS, dtype=F32) / ROPE_AXIS_PAIRS))
    ang = jnp.concatenate([r[:, None] * inv, col[:, None] * inv], axis=-1)
    c64 = jnp.repeat(jnp.cos(ang), 2, axis=-1)
    sign = jnp.tile(jnp.array([-1.0, 1.0], F32), HEAD_DIM // 2)
    s64 = jnp.repeat(jnp.sin(ang), 2, axis=-1) * sign
    return (jnp.tile(c64, (1, ATT_HEADS)), jnp.tile(s64, (1, ATT_HEADS)),
            jnp.tile(c64, (1, KV_HEADS)), jnp.tile(s64, (1, KV_HEADS)))


def _prep_weights(w_in, w_gate_f, b_gate_f, w_gate_b, b_gate_b, gla_norm, q_norm, k_norm, w_out,
                  norm_mix, norm_ffn, w_router, b_router):
    d = w_in.shape[0]
    o_lr = 2 * W_QK + 2 * W_V
    o_q = o_lr + 2 * GLA_GATE_RANK
    o_k = o_q + W_ATT
    head_order = (0, 4, 1, 5, 2, 6, 3, 7)
    w_q = [w_in[:, o_q + h * HEAD_DIM:o_q + (h + 1) * HEAD_DIM] for h in head_order]
    lr_pad = jnp.zeros((d, LANES - 2 * GLA_GATE_RANK), w_in.dtype)
    w_in_p = jnp.concatenate([w_in[:, :o_lr]] + w_q + [w_in[:, o_k:], w_in[:, o_lr:o_q], lr_pad], axis=1)
    wg = jnp.zeros((LANES, 2 * W_QK), F32)
    wg = wg.at[:GLA_GATE_RANK, :W_QK].set(w_gate_f)
    wg = wg.at[GLA_GATE_RANK:2 * GLA_GATE_RANK, W_QK:].set(w_gate_b)
    wg_hi, wg_lo = _split_bf16(wg)
    ones_q = jnp.asarray(np.kron(np.eye(ATT_HEADS), np.ones((HEAD_DIM, HEAD_DIM))), BF16)
    ones_k = jnp.asarray(np.kron(np.eye(KV_HEADS), np.ones((HEAD_DIM, HEAD_DIM))), BF16)
    w_out_p = jnp.concatenate(
        [w_out[:W_V]] + [w_out[W_V + h * HEAD_DIM:W_V + (h + 1) * HEAD_DIM] for h in head_order],
        axis=0).astype(BF16)
    wr_hi, wr_lo = _split_bf16(jnp.pad(w_router, ((0, 0), (0, LANES - N_EXPERTS))))
    return {
        "norm_mix": norm_mix.reshape(1, d), "w_in": w_in_p.astype(BF16),
        "wg_hi": wg_hi, "wg_lo": wg_lo,
        "b_gate": jnp.concatenate([b_gate_f, b_gate_b]).reshape(1, 2 * W_QK),
        "ones_q": ones_q, "ones_k": ones_k,
        "q_norm": jnp.tile(q_norm, ATT_HEADS).reshape(1, W_ATT),
        "k_norm": jnp.tile(k_norm, KV_HEADS).reshape(1, W_KV),
        "gla_norm": gla_norm.reshape(1, GLA_DV), "w_out": w_out_p,
        "norm_ffn": norm_ffn.reshape(1, d), "wr_hi": wr_hi, "wr_lo": wr_lo,
        "b_router": b_router.reshape(N_EXPERTS, 1),
    }


def _mixer(x, mod, mod_shared, wp, rope, cache_k, cache_v, s0_f, s0_b):
    b, s, _ = x.shape
    qa, ka, va, ga, la_f, la_b, q, k, v, kc, vc = _premix(x, mod, mod_shared, wp, rope)
    o_f, sf = _gla(qa, ka, va, la_f, _state_to_blockdiag_t(s0_f), False)
    o_r, sb = _gla(qa, ka, va, la_b, _state_to_blockdiag_t(s0_b), True)
    o_att = _attention(q, k, v, cache_k, cache_v)
    x1, h2, lpos, gates, meta = _postmix(o_f, o_r, ga, o_att, x, mod, mod_shared, wp)
    return x1, h2, lpos, gates, meta, kc, vc, _blockdiag_t_to_state(sf), _blockdiag_t_to_state(sb)


def kernel(x_prompt, x_sample, c, cache_k, cache_v, state_gla_fwd, state_gla_bwd, c_ctx, w_ada, b_ada,
           norm_mix, w_in, w_gate_f, b_gate_f, w_gate_b, b_gate_b, gla_norm, q_norm, k_norm, w_out,
           norm_ffn, w_router, b_router, w_gu, b_gu, w_down, b_down, final_norm):
    bp, sp, d = x_prompt.shape
    bs, ss, _ = x_sample.shape
    assert w_ada.shape[0] == 1, "single-layer trunk"
    wp = _prep_weights(w_in[0], w_gate_f[0], b_gate_f[0], w_gate_b[0], b_gate_b[0], gla_norm[0],
                       q_norm[0], k_norm[0], w_out[0], norm_mix[0], norm_ffn[0], w_router[0], b_router[0])

    n_cond = -(-(1 + bs) // 8) * 8
    conds = jnp.zeros((n_cond, d), F32).at[0].set(c_ctx).at[1:1 + bs].set(c)
    mod = _adaln(conds, w_ada[0], b_ada[0]).reshape(n_cond, 6, d)
    mod = jnp.concatenate([mod, jnp.zeros((n_cond, 2, d), F32)], axis=1)
    mod_p, mod_s = mod[0:1], mod[1:1 + bs]

    zero_state = jnp.zeros((bp, GLA_HEADS, GLA_DK, GLA_DV), F32)
    x1p, h2p, lpp, tgp, metap, kc, vc, sf, sb = _mixer(x_prompt, mod_p, True, wp, None, None, None,
                                                       zero_state, zero_state)
    n_ctx = cache_k.shape[2]
    x1s, h2s, lps, tgs, metas, _, _, _, _ = _mixer(
        x_sample, mod_s, False, wp, _rope_tables(ss),
        cache_k[:, 0].reshape(bs, n_ctx, W_KV), cache_v[:, 0].reshape(bs, n_ctx, W_KV),
        state_gla_fwd[:, 0].astype(F32), state_gla_bwd[:, 0].astype(F32))

    n_p, n_s = bp * sp, bs * ss
    n_tok = n_p + n_s
    ntp = n_p // TOKEN_TILE
    h2 = jnp.concatenate([h2p.reshape(n_p, d), h2s.reshape(n_s, d)], axis=0)
    lpos = jnp.concatenate([lpp, lps], axis=0)
    worst = n_tok * TOP_K + (n_tok // TOKEN_TILE) * N_EXPERTS * (SUBLANES - 1) + N_EXPERTS * (EXPERT_TILE - SUBLANES)
    n_rows = -(-worst // EXPERT_TILE) * EXPERT_TILE
    table, pad_table, tile_expert, n_valid = _route_tables(
        jnp.concatenate([metap, metas], axis=0), n_rows // EXPERT_TILE)
    xs = _dispatch(h2, lpos, table, pad_table, n_rows)
    ys = _experts(xs, tile_expert, n_valid, w_gu[0], b_gu[0], w_down[0], b_down[0])
    y_prompt = _combine(ys, table[:ntp], lpp, tgp, x1p, mod_p, True, final_norm)
    y_sample = _combine(ys, table[ntp:], lps, tgs, x1s, mod_s, False, final_norm)

    new_cache_k = kc.reshape(bp, 1, sp, KV_HEADS, HEAD_DIM)
    new_cache_v = vc.reshape(bp, 1, sp, KV_HEADS, HEAD_DIM)
    return (y_prompt, y_sample, new_cache_k, new_cache_v, sf[:, None], sb[:, None])
```

```python
import functools

import numpy as np
import jax
import jax.numpy as jnp
from jax import lax
from jax.experimental import pallas as pl
from jax.experimental.pallas import tpu as pltpu

F32 = jnp.float32
BF16 = jnp.bfloat16

D_MODEL = 1024
GRID_W = 64
GLA_HEADS = 4
GLA_DV = 128
GLA_DK = 64
GLA_GATE_RANK = 16
GLA_GATE_NORM = 16.0
HEAD_DIM = 64
ATT_HEADS = 8
KV_HEADS = 2
ROPE_THETA = 10000.0
ROPE_AXIS_PAIRS = HEAD_DIM // 4
N_EXPERTS = 32
TOP_K = 4
EXPERT_FF = 1024
SWIGLU_LIMIT = 7.0
SWIGLU_ALPHA = 1.702
RMS_EPS = 1e-6

W_QK = GLA_HEADS * GLA_DK
W_V = GLA_HEADS * GLA_DV
W_ATT = ATT_HEADS * HEAD_DIM
W_KV = KV_HEADS * HEAD_DIM
LANES = 128
D_IN_PAD = 2 * W_QK + 2 * W_V + W_ATT + 2 * W_KV + LANES

PREMIX_TILE = 512
TOKEN_TILE = 256
GLA_BLOCK = 256
GLA_LEVELS = ((256, 32), (32, 8), (8, 1))
ATT_Q_TILE = 512
EXPERT_TILE = 512
VMEM_LIMIT = 56 * 1024 * 1024


def _split_bf16(x):
    hi = x.astype(BF16)
    lo = (x - hi.astype(F32)).astype(BF16)
    return hi, lo


def _dot(a, b):
    return jnp.dot(a, b, preferred_element_type=F32)


def _dot_nt(a, b):
    return lax.dot_general(a, b, (((1,), (1,)), ((), ())), preferred_element_type=F32)


def _dot_tn(a, b):
    return lax.dot_general(a, b, (((0,), (0,)), ((), ())), preferred_element_type=F32)


U32 = jnp.uint32


def _pack_bf16_pairs(x):
    w = x.shape[1] // 2
    bits = lambda t: lax.bitcast_convert_type(t.astype(BF16).astype(F32), U32)
    return (bits(x[:, :w]) >> 16) | (bits(x[:, w:]) & jnp.uint32(0xFFFF0000))


def _unpack_bf16_pairs(w):
    lo = lax.bitcast_convert_type(w << 16, F32)
    hi = lax.bitcast_convert_type(w & jnp.uint32(0xFFFF0000), F32)
    return jnp.concatenate([lo, hi], axis=1).astype(BF16)


def _adaln_kernel(c_ref, w_ref, b_ref, o_ref):
    c = c_ref[...]
    a = c * jax.nn.sigmoid(c)
    ah, al = _split_bf16(a)
    wh, wl = _split_bf16(w_ref[...])
    o_ref[...] = _dot(ah, wh) + _dot(al, wh) + _dot(ah, wl) + b_ref[...]


def _adaln(conds, w_ada, b_ada):
    m, d = conds.shape
    n = w_ada.shape[1]
    tn = 512
    return pl.pallas_call(
        _adaln_kernel,
        grid=(n // tn,),
        in_specs=[pl.BlockSpec((m, d), lambda j: (0, 0)),
                  pl.BlockSpec((d, tn), lambda j: (0, j)),
                  pl.BlockSpec((1, tn), lambda j: (0, j))],
        out_specs=pl.BlockSpec((m, tn), lambda j: (0, j)),
        out_shape=jax.ShapeDtypeStruct((m, n), F32),
        name="adaln",
    )(conds, w_ada, b_ada.reshape(1, n))


def _rope_rotate(x, c, s):
    n = x.shape[-1]
    lane = lax.broadcasted_iota(jnp.int32, x.shape, 1)
    partner = jnp.where((lane & 1) == 0, pltpu.roll(x, n - 1, 1), pltpu.roll(x, 1, 1))
    return x * c + partner * s


def _premix_kernel(*refs, use_rope):
    (x_ref, mod_ref, nmix_ref, win_ref, wgh_ref, wgl_ref, bg_ref, bq_ref, bk_ref,
     qn_ref, kn_ref) = refs[:11]
    n_in = 11
    if use_rope:
        cq_ref, sq_ref, ck_ref, sk_ref = refs[11:15]
        n_in = 15
    (qa_o, ka_o, va_o, ga_o, laf_o, lab_o, q_o, k_o, v_o, kc_o, vc_o) = refs[n_in:]

    x = x_ref[0]
    mod = mod_ref[0]
    sh1 = mod[0:1]
    sc1 = mod[1:2]
    ms = jnp.mean(x * x, axis=-1, keepdims=True)
    h = x * lax.rsqrt(ms + RMS_EPS) * nmix_ref[...]
    h = h * (1.0 + sc1) + sh1
    proj = _dot(h.astype(BF16), win_ref[...])

    o = 0
    qa_o[0] = proj[:, o:o + W_QK] * (GLA_DK ** -0.5); o += W_QK
    ka_o[0] = proj[:, o:o + W_QK]; o += W_QK
    va_o[0] = proj[:, o:o + W_V]; o += W_V
    ga_o[0] = proj[:, o:o + W_V]; o += W_V
    qb = proj[:, o:o + W_ATT]; o += W_ATT
    kb = proj[:, o:o + W_KV]; o += W_KV
    vb = proj[:, o:o + W_KV]; o += W_KV
    lr = proj[:, o:o + LANES]

    lh, ll = _split_bf16(lr)
    xg = _dot(lh, wgh_ref[...]) + _dot(ll, wgh_ref[...]) + _dot(lh, wgl_ref[...]) + bg_ref[...]
    la = (jnp.minimum(xg, 0.0) - jnp.log1p(jnp.exp(-jnp.abs(xg)))) * (1.0 / GLA_GATE_NORM)
    laf_o[0] = la[:, :W_QK]
    lab_o[0] = la[:, W_QK:]

    def head_norm(t, ones_ref, g_ref):
        sh, sl = _split_bf16(t * t)
        msq = (_dot(sh, ones_ref[...]) + _dot(sl, ones_ref[...])) * (1.0 / HEAD_DIM)
        return t * lax.rsqrt(msq + RMS_EPS) * g_ref[...]

    qh = head_norm(qb, bq_ref, qn_ref)
    kh = head_norm(kb, bk_ref, kn_ref)
    kc_o[0] = kh
    vc_o[0] = vb
    if use_rope:
        qh = _rope_rotate(qh, cq_ref[...], sq_ref[...])
        kr = _rope_rotate(kh, ck_ref[...], sk_ref[...])
    else:
        kr = kh
    q_o[0] = (qh * (HEAD_DIM ** -0.5)).astype(BF16)
    k_o[0] = kr.astype(BF16)
    v_o[0] = vb.astype(BF16)


def _premix(x, mod, mod_shared, wp, rope):
    b, s, d = x.shape
    tm = min(PREMIX_TILE, s)
    use_rope = rope is not None
    const = lambda shape: pl.BlockSpec(shape, lambda i, j: (0,) * len(shape))
    tok = lambda w: pl.BlockSpec((1, tm, w), lambda i, j: (i, j, 0))
    mod_map = (lambda i, j: (0, 0, 0)) if mod_shared else (lambda i, j: (i, 0, 0))
    in_specs = [tok(d), pl.BlockSpec((1, 8, d), mod_map), const((1, d)),
                const((d, D_IN_PAD)), const((LANES, 2 * W_QK)), const((LANES, 2 * W_QK)),
                const((1, 2 * W_QK)), const((W_ATT, W_ATT)), const((W_KV, W_KV)),
                const((1, W_ATT)), const((1, W_KV))]
    args = [x, mod, wp["norm_mix"], wp["w_in"], wp["wg_hi"], wp["wg_lo"], wp["b_gate"],
            wp["ones_q"], wp["ones_k"], wp["q_norm"], wp["k_norm"]]
    if use_rope:
        seq = lambda w: pl.BlockSpec((tm, w), lambda i, j: (j, 0))
        in_specs += [seq(W_ATT), seq(W_ATT), seq(W_KV), seq(W_KV)]
        args += list(rope)
    widths = [(W_QK, F32), (W_QK, F32), (W_V, F32), (W_V, F32), (W_QK, F32), (W_QK, F32),
              (W_ATT, BF16), (W_KV, BF16), (W_KV, BF16), (W_KV, F32), (W_KV, F32)]
    return pl.pallas_call(
        functools.partial(_premix_kernel, use_rope=use_rope),
        grid=(b, s // tm),
        in_specs=in_specs,
        out_specs=[tok(w) for w, _ in widths],
        out_shape=[jax.ShapeDtypeStruct((b, s, w), dt) for w, dt in widths],
        compiler_params=pltpu.CompilerParams(
            dimension_semantics=("parallel", "parallel"), vmem_limit_bytes=VMEM_LIMIT),
        name="premix_rope" if use_rope else "premix",
    )(*args)


def _bcast_rows(x, n_par, p_rows, row):
    w = x.shape[-1]
    r = x.reshape(n_par, p_rows, w)[:, row:row + 1, :]
    return jnp.broadcast_to(r, (n_par, p_rows, w)).reshape(n_par * p_rows, w)


def _gla_kernel(q_ref, k_ref, v_ref, la_ref, s0_ref, o_ref, sf_ref, st_scr, *, rows, levels, reverse):
    blk = pl.program_id(1)

    @pl.when(blk == 0)
    def _():
        st_scr[...] = s0_ref[0]

    q = q_ref[0]
    k = k_ref[0]
    la = la_ref[0]
    vb = v_ref[0].astype(BF16)

    ri = lax.broadcasted_iota(jnp.int32, (rows, rows), 0)
    ci = lax.broadcasted_iota(jnp.int32, (rows, rows), 1)
    tri = jnp.where((ri <= ci) if reverse else (ri >= ci), 1.0, 0.0).astype(BF16)
    hi = la.astype(BF16)
    r1 = la - hi.astype(F32)
    mid = r1.astype(BF16)
    lo = (r1 - mid.astype(F32)).astype(BF16)
    cum = _dot(tri, hi) + _dot(tri, mid) + _dot(tri, lo)

    ridx = lax.broadcasted_iota(jnp.int32, (rows, W_QK), 0)
    s_acc = [None] * GLA_HEADS
    for (par, sub) in levels:
        groups = par // sub
        n_par = rows // par
        pi = (ridx % par) // sub
        edge = sub - 1 if reverse else 0
        if sub > 1:
            own = _bcast_rows(cum, rows // sub, sub, edge)
            qt = q * jnp.exp(cum - own)
        else:
            qt = q
        qps, kps = [], []
        for p in range(groups):
            if sub > 1 and p == (groups - 1 if reverse else 0):
                continue
            cp = _bcast_rows(cum, n_par, par, p * sub + edge)
            if reverse:
                km = (pi > p) if sub > 1 else (pi >= p)
            else:
                km = (pi < p) if sub > 1 else (pi <= p)
            kps.append(jnp.where(km, k * jnp.exp(cp - cum), 0.0).astype(BF16))
            qps.append(jnp.where(pi == p, qt, 0.0).astype(BF16))
        ng = len(qps)
        lane_c = lax.broadcasted_iota(jnp.int32, (rows, LANES * ng), 1)
        same_parent = (ri // par) == (ci // par)
        for hp in range(GLA_HEADS // 2):
            qc = jnp.concatenate([a[:, LANES * hp:LANES * (hp + 1)] for a in qps], axis=1)
            kc = jnp.concatenate([a[:, LANES * hp:LANES * (hp + 1)] for a in kps], axis=1)
            for hh in range(2):
                qh = jnp.where(((lane_c % LANES) // GLA_DK) == hh, qc, jnp.zeros_like(qc))
                sl = _dot_nt(qh, kc)
                if par < rows:
                    sl = jnp.where(same_parent, sl, 0.0)
                h = 2 * hp + hh
                s_acc[h] = sl if s_acc[h] is None else s_acc[h] + sl

    q0 = (q * jnp.exp(cum)).astype(BF16)
    far = 0 if reverse else rows - 1
    last = cum[far:far + 1, :]
    kd = (k * jnp.exp(last - cum)).astype(BF16)
    br = lax.broadcasted_iota(jnp.int32, (2 * GLA_DV, 2 * GLA_DK), 0)
    bc = lax.broadcasted_iota(jnp.int32, (2 * GLA_DV, 2 * GLA_DK), 1)
    blockdiag = (br // GLA_DV) == (bc // GLA_DK)
    for hp in range(GLA_HEADS // 2):
        st = st_scr[hp]
        o_inter = _dot_nt(q0[:, LANES * hp:LANES * (hp + 1)], st.astype(BF16))
        o_intra = jnp.concatenate(
            [_dot(s_acc[2 * hp + hh].astype(BF16),
                  vb[:, GLA_DV * (2 * hp + hh):GLA_DV * (2 * hp + hh + 1)]) for hh in range(2)],
            axis=1)
        o_ref[0, :, 2 * GLA_DV * hp:2 * GLA_DV * (hp + 1)] = o_inter + o_intra
        upd = _dot_tn(vb[:, 2 * GLA_DV * hp:2 * GLA_DV * (hp + 1)],
                      kd[:, LANES * hp:LANES * (hp + 1)])
        st_scr[hp] = (jnp.exp(last[:, LANES * hp:LANES * (hp + 1)]) * st
                      + jnp.where(blockdiag, upd, 0.0))

    @pl.when(blk == pl.num_programs(1) - 1)
    def _():
        sf_ref[0] = st_scr[...]


def _gla(q, k, v, la, s0t, reverse):
    n, l, _ = q.shape
    rows = min(GLA_BLOCK, l)
    nb = l // rows
    levels = tuple((min(p, rows), s) for p, s in GLA_LEVELS)
    order = (lambda j: nb - 1 - j) if reverse else (lambda j: j)
    tok = lambda w: pl.BlockSpec((1, rows, w), lambda i, j: (i, order(j), 0))
    st_spec = pl.BlockSpec((1, 2, 2 * GLA_DV, 2 * GLA_DK), lambda i, j: (i, 0, 0, 0))
    return pl.pallas_call(
        functools.partial(_gla_kernel, rows=rows, levels=levels, reverse=reverse),
        grid=(n, nb),
        in_specs=[tok(W_QK), tok(W_QK), tok(W_V), tok(W_QK), st_spec],
        out_specs=[tok(W_V), st_spec],
        out_shape=[jax.ShapeDtypeStruct((n, l, W_V), F32),
                   jax.ShapeDtypeStruct(s0t.shape, F32)],
        scratch_shapes=[pltpu.VMEM((2, 2 * GLA_DV, 2 * GLA_DK), F32)],
        compiler_params=pltpu.CompilerParams(
            dimension_semantics=("parallel", "arbitrary"), vmem_limit_bytes=VMEM_LIMIT),
        name="gla_bwd" if reverse else "gla_fwd",
    )(q, k, v, la, s0t)


def _state_to_blockdiag_t(s):
    n = s.shape[0]
    st = jnp.swapaxes(s, -1, -2).reshape(n, 2, 2, GLA_DV, GLA_DK)
    z = jnp.zeros_like(st[:, :, 0])
    top = jnp.concatenate([st[:, :, 0], z], axis=-1)
    bot = jnp.concatenate([z, st[:, :, 1]], axis=-1)
    return jnp.concatenate([top, bot], axis=-2)


def _blockdiag_t_to_state(sb):
    n = sb.shape[0]
    h0 = sb[:, :, :GLA_DV, :GLA_DK]
    h1 = sb[:, :, GLA_DV:, GLA_DK:]
    st = jnp.stack([h0, h1], axis=2).reshape(n, GLA_HEADS, GLA_DV, GLA_DK)
    return jnp.swapaxes(st, -1, -2)


def _attn_kernel(*refs, s_self, n_ctx):
    if n_ctx:
        q_ref, k_ref, v_ref, ck_ref, cv_ref, o_ref, kg_scr, v_scr = refs
    else:
        q_ref, k_ref, v_ref, o_ref, kg_scr, v_scr = refs

    @pl.when(pl.program_id(1) == 0)
    def _():
        def put(kk, vv, start, n):
            lane = lax.broadcasted_iota(jnp.int32, kk.shape, 1)
            kg_scr[0, start:start + n, :] = jnp.where(lane < HEAD_DIM, kk, jnp.zeros_like(kk))
            kg_scr[1, start:start + n, :] = jnp.where(lane >= HEAD_DIM, kk, jnp.zeros_like(kk))
            v_scr[start:start + n, :] = vv
        put(k_ref[0], v_ref[0], 0, s_self)
        if n_ctx:
            put(ck_ref[0].astype(BF16), cv_ref[0].astype(BF16), s_self, n_ctx)

    q = q_ref[0]
    vv = v_scr[...]
    for m in range(ATT_HEADS // KV_HEADS):
        qm = q[:, LANES * m:LANES * (m + 1)]
        og = []
        for g in range(KV_HEADS):
            s = _dot_nt(qm, kg_scr[g])
            mx = jnp.max(s, axis=-1, keepdims=True)
            p = jnp.exp(s - mx)
            l = jnp.sum(p, axis=-1, keepdims=True)
            og.append(_dot(p.astype(BF16), vv) / l)
        lane = lax.broadcasted_iota(jnp.int32, og[0].shape, 1)
        o_ref[0, :, LANES * m:LANES * (m + 1)] = jnp.where(lane < HEAD_DIM, og[0], og[1]).astype(BF16)


def _attention(q, k, v, cache_k=None, cache_v=None):
    b, s, _ = q.shape
    n_ctx = 0 if cache_k is None else cache_k.shape[1]
    tq = min(ATT_Q_TILE, s)
    sk = s + n_ctx
    full = lambda n, w: pl.BlockSpec((1, n, w), lambda i, j: (i, 0, 0))
    in_specs = [pl.BlockSpec((1, tq, W_ATT), lambda i, j: (i, j, 0)), full(s, W_KV), full(s, W_KV)]
    args = [q, k, v]
    if n_ctx:
        in_specs += [full(n_ctx, W_KV), full(n_ctx, W_KV)]
        args += [cache_k, cache_v]
    return pl.pallas_call(
        functools.partial(_attn_kernel, s_self=s, n_ctx=n_ctx),
        grid=(b, s // tq),
        in_specs=in_specs,
        out_specs=pl.BlockSpec((1, tq, W_ATT), lambda i, j: (i, j, 0)),
        out_shape=jax.ShapeDtypeStruct((b, s, W_ATT), BF16),
        scratch_shapes=[pltpu.VMEM((KV_HEADS, sk, W_KV), BF16), pltpu.VMEM((sk, W_KV), BF16)],
        compiler_params=pltpu.CompilerParams(
            dimension_semantics=("parallel", "arbitrary"), vmem_limit_bytes=VMEM_LIMIT),
        name="attention_ctx" if n_ctx else "attention",
    )(*args)


def _postmix_kernel(of_ref, or_ref, ga_ref, oa_ref, x_ref, mod_ref, gn_ref, wout_ref, nffn_ref,
                    wrh_ref, wrl_ref, br_ref, x1_o, h2_o, lp_o, tg_o, meta_o):
    mod = mod_ref[0]
    g1, sh2, sc2 = mod[2:3], mod[3:4], mod[4:5]
    og = of_ref[0] + or_ref[0]
    ga = ga_ref[0]
    parts = []
    for h in range(GLA_HEADS):
        blk = og[:, GLA_DV * h:GLA_DV * (h + 1)]
        ms = jnp.mean(blk * blk, axis=-1, keepdims=True)
        gh = ga[:, GLA_DV * h:GLA_DV * (h + 1)]
        parts.append((blk * lax.rsqrt(ms + RMS_EPS) * gn_ref[...] * (gh * jax.nn.sigmoid(gh))).astype(BF16))
    mix = jnp.concatenate(parts + [oa_ref[0]], axis=1)
    mo = _dot(mix, wout_ref[...])
    x1 = x_ref[0] + g1 * mo
    x1_o[0] = x1
    ms = jnp.mean(x1 * x1, axis=-1, keepdims=True)
    h2 = x1 * lax.rsqrt(ms + RMS_EPS) * nffn_ref[...]
    h2 = h2 * (1.0 + sc2) + sh2
    hh, hl = _split_bf16(h2)
    h2_o[0] = hh

    lg = _dot(hh, wrh_ref[...]) + _dot(hl, wrh_ref[...]) + _dot(hh, wrl_ref[...])
    lt = jnp.transpose(lg)[:N_EXPERTS] + br_ref[...]
    tm = lt.shape[1]
    eidx = lax.broadcasted_iota(jnp.int32, lt.shape, 0)
    vals, sels = [], []
    for _ in range(TOP_K):
        mx = jnp.max(lt, axis=0, keepdims=True)
        idx = jnp.min(jnp.where(lt == mx, eidx, N_EXPERTS), axis=0, keepdims=True)
        sel = eidx == idx
        lt = jnp.where(sel, -jnp.inf, lt)
        vals.append(mx)
        sels.append(sel)
    ws = [jnp.exp(vv - vals[0]) for vv in vals]
    tot = ws[0] + ws[1] + ws[2] + ws[3]
    tg_o[0] = jnp.concatenate([w / tot for w in ws] + [jnp.zeros((8 - TOP_K, tm), F32)], axis=0)

    onehot = sum(jnp.where(s, 1.0, 0.0) for s in sels)
    ti = lax.broadcasted_iota(jnp.int32, (tm, tm), 0)
    tj = lax.broadcasted_iota(jnp.int32, (tm, tm), 1)
    rank = _dot(onehot.astype(BF16), jnp.where(ti < tj, 1.0, 0.0).astype(BF16))
    cnt = jnp.sum(onehot, axis=1, keepdims=True)
    cnt = jnp.floor((cnt + 7.0) * 0.125) * 8.0
    ei = lax.broadcasted_iota(jnp.int32, (N_EXPERTS, N_EXPERTS), 0)
    ej = lax.broadcasted_iota(jnp.int32, (N_EXPERTS, N_EXPERTS), 1)
    cnt_b = jnp.broadcast_to(cnt, (N_EXPERTS, tm))
    seg = _dot(jnp.where(ej < ei, 1.0, 0.0).astype(BF16), cnt_b.astype(BF16))
    base = seg + rank
    lpos = [jnp.sum(jnp.where(s, base, 0.0), axis=0, keepdims=True).astype(jnp.int32) for s in sels]
    lp_o[0] = jnp.concatenate(lpos + [jnp.zeros((8 - TOP_K, tm), jnp.int32)], axis=0)
    meta_o[0] = jnp.concatenate([cnt_b[:, :LANES], seg[:, :LANES]], axis=0).astype(jnp.int32)


def _postmix(o_f, o_r, ga, o_att, x, mod, mod_shared, wp):
    b, s, d = x.shape
    tm = min(TOKEN_TILE, s)
    nt = s // tm
    const = lambda shape: pl.BlockSpec(shape, lambda i, j: (0,) * len(shape))
    tok = lambda w: pl.BlockSpec((1, tm, w), lambda i, j: (i, j, 0))
    mod_map = (lambda i, j: (0, 0, 0)) if mod_shared else (lambda i, j: (i, 0, 0))
    lane_tok = pl.BlockSpec((1, 8, tm), lambda i, j: (i * nt + j, 0, 0))
    return pl.pallas_call(
        _postmix_kernel,
        grid=(b, nt),
        in_specs=[tok(W_V), tok(W_V), tok(W_V), tok(W_ATT), tok(d), pl.BlockSpec((1, 8, d), mod_map),
                  const((1, GLA_DV)), const((d, d)), const((1, d)),
                  const((d, LANES)), const((d, LANES)), const((N_EXPERTS, 1))],
        out_specs=[tok(d), tok(d), lane_tok, lane_tok,
                   pl.BlockSpec((1, 2 * N_EXPERTS, LANES), lambda i, j: (i * nt + j, 0, 0))],
        out_shape=[jax.ShapeDtypeStruct((b, s, d), F32), jax.ShapeDtypeStruct((b, s, d), BF16),
                   jax.ShapeDtypeStruct((b * nt, 8, tm), jnp.int32),
                   jax.ShapeDtypeStruct((b * nt, 8, tm), F32),
                   jax.ShapeDtypeStruct((b * nt, 2 * N_EXPERTS, LANES), jnp.int32)],
        compiler_params=pltpu.CompilerParams(
            dimension_semantics=("parallel", "parallel"), vmem_limit_bytes=VMEM_LIMIT),
        name="postmix",
    )(o_f, o_r, ga, o_att, x, mod, wp["gla_norm"], wp["w_out"], wp["norm_ffn"],
      wp["wr_hi"], wp["wr_lo"], wp["b_router"])


SUBLANES = 8
SEG_PIECES = tuple(TOKEN_TILE >> b for b in range(6))
PAIR_ROWS = TOKEN_TILE * TOP_K
LOCAL_ROWS = PAIR_ROWS + N_EXPERTS * SUBLANES
TAB_REM = 3 * N_EXPERTS


def _pieces(n, emit):
    for size in SEG_PIECES:
        done = n & ~(2 * size - 1)
        pl.when((n & size) != 0)(functools.partial(emit, done, size))


def _segment_copies(tab_ref, copy):
    def body(e, carry):
        dst0 = tab_ref[0, 0, e]
        n = tab_ref[0, 0, N_EXPERTS + e]
        src0 = tab_ref[0, 0, 2 * N_EXPERTS + e]
        _pieces(n, lambda done, size: copy(pl.multiple_of(src0 + done, SUBLANES),
                                           pl.multiple_of(dst0 + done, SUBLANES), size))
        return carry
    lax.fori_loop(0, N_EXPERTS, body, 0)


def _wait_rows(rem, wait_piece):
    wait_piece(PAIR_ROWS)
    _pieces(rem, lambda done, size: wait_piece(size))


def _pair_onehot(lp_ref, weights=None):
    tm = lp_ref.shape[-1]
    rows = lax.broadcasted_iota(jnp.int32, (LOCAL_ROWS, tm), 0)
    acc = jnp.zeros((LOCAL_ROWS, tm), F32)
    for r in range(TOP_K):
        w = 1.0 if weights is None else weights[r]
        acc = jnp.where(rows == lp_ref[0, r:r + 1, :], w, acc)
    return acc


def _dispatch_kernel(tab_ref, pad_ref, ha_ref, hb_ref, lp_ref, xs_hbm, xl, zbuf, hbuf, rem_prev, sem, zsem, *,
                     tiles_a):
    i = pl.program_id(0)
    slot = i % 2
    perm = _pair_onehot(lp_ref).astype(BF16)

    @pl.when(i < tiles_a)
    def _():
        hbuf[...] = ha_ref[...]

    @pl.when(i >= tiles_a)
    def _():
        hbuf[...] = hb_ref[...]

    xl[slot] = _pack_bf16_pairs(_dot(perm, hbuf[...]))

    def copy(local, glob, size):
        pltpu.make_async_copy(xl.at[slot, pl.ds(local, size)], xs_hbm.at[pl.ds(glob, size)],
                              sem.at[slot]).start()
    _segment_copies(tab_ref, copy)

    def wait_slot(s, rem):
        _wait_rows(rem, lambda size: pltpu.make_async_copy(
            xl.at[s, pl.ds(0, size)], xs_hbm.at[pl.ds(0, size)], sem.at[s]).wait())

    @pl.when(i > 0)
    def _():
        wait_slot(1 - slot, rem_prev[0])
    rem_prev[0] = tab_ref[0, 0, TAB_REM]

    @pl.when(i == pl.num_programs(0) - 1)
    def _():
        wait_slot(slot, tab_ref[0, 0, TAB_REM])
        zbuf[...] = jnp.zeros_like(zbuf)

        def pads(wait):
            def body(e, carry):
                start = pad_ref[0, e]
                def emit(done, size):
                    cp = pltpu.make_async_copy(
                        zbuf.at[pl.ds(0, size)],
                        xs_hbm.at[pl.ds(pl.multiple_of(start + done, SUBLANES), size)], zsem)
                    cp.wait() if wait else cp.start()
                _pieces(pad_ref[0, N_EXPERTS + e], emit)
                return carry
            lax.fori_loop(0, N_EXPERTS, body, 0)
        n_tiles = xs_hbm.shape[0] // EXPERT_TILE

        def tail(wait):
            def body(t, carry):
                cp = pltpu.make_async_copy(
                    zbuf, xs_hbm.at[pl.ds(pl.multiple_of(t * EXPERT_TILE, EXPERT_TILE), EXPERT_TILE)], zsem)
                cp.wait() if wait else cp.start()
                return carry
            lax.fori_loop(pad_ref[0, 2 * N_EXPERTS], n_tiles, body, 0)
        pads(False)
        tail(False)
        pads(True)
        tail(True)


def _dispatch(h_a, h_b, lpos, table, pad_table, n_rows):
    d = h_a.shape[1]
    tm = TOKEN_TILE
    ta, tb = h_a.shape[0] // tm, h_b.shape[0] // tm
    return pl.pallas_call(
        functools.partial(_dispatch_kernel, tiles_a=ta),
        grid=(ta + tb,),
        in_specs=[pl.BlockSpec((1, 1, LANES), lambda i: (i, 0, 0), memory_space=pltpu.SMEM),
                  pl.BlockSpec((1, LANES), lambda i: (0, 0), memory_space=pltpu.SMEM),
                  pl.BlockSpec((tm, d), lambda i: (jnp.minimum(i, ta - 1), 0)),
                  pl.BlockSpec((tm, d), lambda i: (jnp.maximum(i - ta, 0), 0)),
                  pl.BlockSpec((1, 8, tm), lambda i: (i, 0, 0))],
        out_specs=pl.BlockSpec(memory_space=pl.ANY),
        out_shape=jax.ShapeDtypeStruct((n_rows, d // 2), U32),
        scratch_shapes=[pltpu.VMEM((2, LOCAL_ROWS, d // 2), U32), pltpu.VMEM((EXPERT_TILE, d // 2), U32),
                        pltpu.VMEM((tm, d), BF16), pltpu.SMEM((1,), jnp.int32),
                        pltpu.SemaphoreType.DMA((2,)), pltpu.SemaphoreType.DMA(())],
        compiler_params=pltpu.CompilerParams(
            dimension_semantics=("arbitrary",), vmem_limit_bytes=VMEM_LIMIT),
        name="dispatch",
    )(table, pad_table, h_a, h_b, lpos)


def _experts_kernel(te_ref, nv_ref, xs_ref, wgu_ref, bgu_ref, wd_ref, bd_ref, o_ref, wgu_bf, wd_bf):
    i = pl.program_id(0)
    valid = i < nv_ref[0]
    new_expert = jnp.logical_or(i == 0, te_ref[i] != te_ref[jnp.maximum(i - 1, 0)])

    @pl.when(jnp.logical_and(valid, new_expert))
    def _():
        wgu_bf[...] = wgu_ref[0].astype(BF16)
        wd_bf[...] = wd_ref[0].astype(BF16)

    @pl.when(valid)
    def _():
        x = _unpack_bf16_pairs(xs_ref[...])
        gu = _dot(x, wgu_bf[...]) + bgu_ref[0]
        g = jnp.minimum(gu[:, :EXPERT_FF], SWIGLU_LIMIT)
        u = jnp.clip(gu[:, EXPERT_FF:], -SWIGLU_LIMIT, SWIGLU_LIMIT)
        act = (u + 1.0) * (g * jax.nn.sigmoid(SWIGLU_ALPHA * g))
        o_ref[...] = _pack_bf16_pairs(_dot(act.astype(BF16), wd_bf[...]) + bd_ref[0])

    @pl.when(jnp.logical_not(valid))
    def _():
        o_ref[...] = jnp.zeros_like(o_ref)


def _experts(xs, tile_expert, n_valid, w_gu, b_gu, w_down, b_down):
    n_rows = xs.shape[0]
    tm = EXPERT_TILE
    nt = n_rows // tm
    ne, d, ff2 = w_gu.shape
    ff = ff2 // 2
    grid_spec = pltpu.PrefetchScalarGridSpec(
        num_scalar_prefetch=2,
        grid=(nt,),
        in_specs=[pl.BlockSpec((tm, d // 2), lambda i, te, nv: (i, 0)),
                  pl.BlockSpec((1, d, ff2), lambda i, te, nv: (te[i], 0, 0)),
                  pl.BlockSpec((1, 1, ff2), lambda i, te, nv: (te[i], 0, 0)),
                  pl.BlockSpec((1, ff, d), lambda i, te, nv: (te[i], 0, 0)),
                  pl.BlockSpec((1, 1, d), lambda i, te, nv: (te[i], 0, 0))],
        out_specs=pl.BlockSpec((tm, d // 2), lambda i, te, nv: (i, 0)),
        scratch_shapes=[pltpu.VMEM((d, ff2), BF16), pltpu.VMEM((ff, d), BF16)],
    )
    return pl.pallas_call(
        _experts_kernel,
        grid_spec=grid_spec,
        out_shape=jax.ShapeDtypeStruct((n_rows, d // 2), U32),
        compiler_params=pltpu.CompilerParams(
            dimension_semantics=("arbitrary",), vmem_limit_bytes=VMEM_LIMIT),
        name="experts",
    )(tile_expert, n_valid, xs, w_gu, b_gu.reshape(ne, 1, ff2), w_down, b_down.reshape(ne, 1, d))


def _combine_kernel(tab_ref, ys_hbm, lp_ref, tg_ref, x1_ref, mod_ref, fn_ref, y_ref, buf, sem):
    @pl.when(jnp.logical_and(pl.program_id(0) == 0, pl.program_id(1) == 0))
    def _():
        buf[...] = jnp.zeros_like(buf)

    def copy(local, glob, size):
        pltpu.make_async_copy(ys_hbm.at[pl.ds(glob, size)], buf.at[pl.ds(local, size)], sem).start()
    _segment_copies(tab_ref, copy)
    tg = tg_ref[0]
    wg = _pair_onehot(lp_ref, [tg[r:r + 1, :] for r in range(TOP_K)]).astype(BF16)
    _wait_rows(tab_ref[0, 0, TAB_REM], lambda size: pltpu.make_async_copy(
        ys_hbm.at[pl.ds(0, size)], buf.at[pl.ds(0, size)], sem).wait())
    nrow = PAIR_ROWS + tab_ref[0, 0, TAB_REM]
    rowi = lax.broadcasted_iota(jnp.int32, buf.shape, 0)
    yb = _unpack_bf16_pairs(jnp.where(rowi < nrow, buf[...], jnp.uint32(0)))
    y = _dot_tn(wg, yb)
    g2 = mod_ref[0][5:6]
    x2 = x1_ref[0] + g2 * y
    ms = jnp.mean(x2 * x2, axis=-1, keepdims=True)
    y_ref[0] = x2 * lax.rsqrt(ms + RMS_EPS) * fn_ref[...]


def _combine(ys, table, lpos, gates, x1, mod, mod_shared, final_norm):
    b, s, d = x1.shape
    tm = TOKEN_TILE
    nt = s // tm
    mod_map = (lambda i, j: (0, 0, 0)) if mod_shared else (lambda i, j: (i, 0, 0))
    tile = lambda shape: pl.BlockSpec(shape, lambda i, j: (i * nt + j, 0, 0))
    return pl.pallas_call(
        _combine_kernel,
        grid=(b, nt),
        in_specs=[pl.BlockSpec((1, 1, LANES), lambda i, j: (i * nt + j, 0, 0), memory_space=pltpu.SMEM),
                  pl.BlockSpec(memory_space=pl.ANY),
                  tile((1, 8, tm)), tile((1, 8, tm)),
                  pl.BlockSpec((1, tm, d), lambda i, j: (i, j, 0)),
                  pl.BlockSpec((1, 8, d), mod_map),
                  pl.BlockSpec((1, d), lambda i, j: (0, 0))],
        out_specs=pl.BlockSpec((1, tm, d), lambda i, j: (i, j, 0)),
        out_shape=jax.ShapeDtypeStruct((b, s, d), F32),
        scratch_shapes=[pltpu.VMEM((LOCAL_ROWS, d // 2), U32), pltpu.SemaphoreType.DMA(())],
        compiler_params=pltpu.CompilerParams(
            dimension_semantics=("arbitrary", "arbitrary"), vmem_limit_bytes=VMEM_LIMIT),
        name="combine",
    )(table, ys, lpos, gates, x1, mod, final_norm.reshape(1, d))


def _route_tables(meta, n_tiles_e):
    tm = EXPERT_TILE
    cnt = meta[:, :N_EXPERTS, 0]
    lstart = meta[:, N_EXPERTS:, 0]
    tot = jnp.sum(cnt, axis=0)
    tiles = (tot + tm - 1) // tm
    tile_end = jnp.cumsum(tiles)
    off = (tile_end - tiles) * tm
    dest = off[None, :] + jnp.cumsum(cnt, axis=0) - cnt
    rem = jnp.sum(cnt, axis=1, keepdims=True) - PAIR_ROWS
    table = jnp.concatenate([dest, cnt, lstart, jnp.broadcast_to(rem, cnt.shape)], axis=1).astype(jnp.int32)
    n_valid = tile_end[-1].astype(jnp.int32).reshape(1)
    pad_table = jnp.concatenate([off + tot, tiles * tm - tot, jnp.broadcast_to(n_valid, (2 * N_EXPERTS,))])
    pad_table = pad_table.astype(jnp.int32).reshape(1, LANES)
    tile_ids = jnp.minimum(jnp.arange(n_tiles_e), n_valid[0] - 1)
    tile_expert = jnp.sum((tile_end[None, :] <= tile_ids[:, None]).astype(jnp.int32), axis=1)
    return table.reshape(-1, 1, LANES), pad_table, tile_expert, n_valid


def _rope_tables(n_tok):
    rows = n_tok // GRID_W
    r, col = jnp.meshgrid(jnp.arange(rows), jnp.arange(GRID_W), indexing="ij")
    r = r.reshape(-1).astype(F32)
    col = col.reshape(-1).astype(F32)
    inv = 1.0 / (ROPE_THETA ** (jnp.arange(ROPE_AXIS_PAIRS, dtype=F32) / ROPE_AXIS_PAIRS))
    ang = jnp.concatenate([r[:, None] * inv, col[:, None] * inv], axis=-1)
    c64 = jnp.repeat(jnp.cos(ang), 2, axis=-1)
    sign = jnp.tile(jnp.array([-1.0, 1.0], F32), HEAD_DIM // 2)
    s64 = jnp.repeat(jnp.sin(ang), 2, axis=-1) * sign
    return (jnp.tile(c64, (1, ATT_HEADS)), jnp.tile(s64, (1, ATT_HEADS)),
            jnp.tile(c64, (1, KV_HEADS)), jnp.tile(s64, (1, KV_HEADS)))


def _prep_weights(w_in, w_gate_f, b_gate_f, w_gate_b, b_gate_b, gla_norm, q_norm, k_norm, w_out,
                  norm_mix, norm_ffn, w_router, b_router):
    d = w_in.shape[0]
    o_lr = 2 * W_QK + 2 * W_V
    o_q = o_lr + 2 * GLA_GATE_RANK
    o_k = o_q + W_ATT
    head_order = (0, 4, 1, 5, 2, 6, 3, 7)
    w_q = [w_in[:, o_q + h * HEAD_DIM:o_q + (h + 1) * HEAD_DIM] for h in head_order]
    lr_pad = jnp.zeros((d, LANES - 2 * GLA_GATE_RANK), w_in.dtype)
    w_in_p = jnp.concatenate([w_in[:, :o_lr]] + w_q + [w_in[:, o_k:], w_in[:, o_lr:o_q], lr_pad], axis=1)
    wg = jnp.zeros((LANES, 2 * W_QK), F32)
    wg = wg.at[:GLA_GATE_RANK, :W_QK].set(w_gate_f)
    wg = wg.at[GLA_GATE_RANK:2 * GLA_GATE_RANK, W_QK:].set(w_gate_b)
    wg_hi, wg_lo = _split_bf16(wg)
    ones_q = jnp.asarray(np.kron(np.eye(ATT_HEADS), np.ones((HEAD_DIM, HEAD_DIM))), BF16)
    ones_k = jnp.asarray(np.kron(np.eye(KV_HEADS), np.ones((HEAD_DIM, HEAD_DIM))), BF16)
    w_out_p = jnp.concatenate(
        [w_out[:W_V]] + [w_out[W_V + h * HEAD_DIM:W_V + (h + 1) * HEAD_DIM] for h in head_order],
        axis=0).astype(BF16)
    wr_hi, wr_lo = _split_bf16(jnp.pad(w_router, ((0, 0), (0, LANES - N_EXPERTS))))
    return {
        "norm_mix": norm_mix.reshape(1, d), "w_in": w_in_p.astype(BF16),
        "wg_hi": wg_hi, "wg_lo": wg_lo,
        "b_gate": jnp.concatenate([b_gate_f, b_gate_b]).reshape(1, 2 * W_QK),
        "ones_q": ones_q, "ones_k": ones_k,
        "q_norm": jnp.tile(q_norm, ATT_HEADS).reshape(1, W_ATT),
        "k_norm": jnp.tile(k_norm, KV_HEADS).reshape(1, W_KV),
        "gla_norm": gla_norm.reshape(1, GLA_DV), "w_out": w_out_p,
        "norm_ffn": norm_ffn.reshape(1, d), "wr_hi": wr_hi, "wr_lo": wr_lo,
        "b_router": b_router.reshape(N_EXPERTS, 1),
    }


def _mixer(x, mod, mod_shared, wp, rope, cache_k, cache_v, s0_f, s0_b):
    b, s, _ = x.shape
    qa, ka, va, ga, la_f, la_b, q, k, v, kc, vc = _premix(x, mod, mod_shared, wp, rope)
    o_f, sf = _gla(qa, ka, va, la_f, _state_to_blockdiag_t(s0_f), False)
    o_r, sb = _gla(qa, ka, va, la_b, _state_to_blockdiag_t(s0_b), True)
    o_att = _attention(q, k, v, cache_k, cache_v)
    x1, h2, lpos, gates, meta = _postmix(o_f, o_r, ga, o_att, x, mod, mod_shared, wp)
    return x1, h2, lpos, gates, meta, kc, vc, _blockdiag_t_to_state(sf), _blockdiag_t_to_state(sb)


def kernel(x_prompt, x_sample, c, cache_k, cache_v, state_gla_fwd, state_gla_bwd, c_ctx, w_ada, b_ada,
           norm_mix, w_in, w_gate_f, b_gate_f, w_gate_b, b_gate_b, gla_norm, q_norm, k_norm, w_out,
           norm_ffn, w_router, b_router, w_gu, b_gu, w_down, b_down, final_norm):
    bp, sp, d = x_prompt.shape
    bs, ss, _ = x_sample.shape
    assert w_ada.shape[0] == 1, "single-layer trunk"
    wp = _prep_weights(w_in[0], w_gate_f[0], b_gate_f[0], w_gate_b[0], b_gate_b[0], gla_norm[0],
                       q_norm[0], k_norm[0], w_out[0], norm_mix[0], norm_ffn[0], w_router[0], b_router[0])

    n_cond = -(-(1 + bs) // 8) * 8
    conds = jnp.zeros((n_cond, d), F32).at[0].set(c_ctx).at[1:1 + bs].set(c)
    mod = _adaln(conds, w_ada[0], b_ada[0]).reshape(n_cond, 6, d)
    mod = jnp.concatenate([mod, jnp.zeros((n_cond, 2, d), F32)], axis=1)
    mod_p, mod_s = mod[0:1], mod[1:1 + bs]

    zero_state = jnp.zeros((bp, GLA_HEADS, GLA_DK, GLA_DV), F32)
    x1p, h2p, lpp, tgp, metap, kc, vc, sf, sb = _mixer(x_prompt, mod_p, True, wp, None, None, None,
                                                       zero_state, zero_state)
    n_ctx = cache_k.shape[2]
    x1s, h2s, lps, tgs, metas, _, _, _, _ = _mixer(
        x_sample, mod_s, False, wp, _rope_tables(ss),
        cache_k[:, 0].reshape(bs, n_ctx, W_KV), cache_v[:, 0].reshape(bs, n_ctx, W_KV),
        state_gla_fwd[:, 0].astype(F32), state_gla_bwd[:, 0].astype(F32))

    n_p, n_s = bp * sp, bs * ss
    n_tok = n_p + n_s
    ntp = n_p // TOKEN_TILE
    lpos = jnp.concatenate([lpp, lps], axis=0)
    worst = n_tok * TOP_K + (n_tok // TOKEN_TILE) * N_EXPERTS * (SUBLANES - 1) + N_EXPERTS * (EXPERT_TILE - SUBLANES)
    n_rows = -(-worst // EXPERT_TILE) * EXPERT_TILE
    table, pad_table, tile_expert, n_valid = _route_tables(
        jnp.concatenate([metap, metas], axis=0), n_rows // EXPERT_TILE)
    xs = _dispatch(h2p.reshape(n_p, d), h2s.reshape(n_s, d), lpos, table, pad_table, n_rows)
    ys = _experts(xs, tile_expert, n_valid, w_gu[0], b_gu[0], w_down[0], b_down[0])
    y_prompt = _combine(ys, table[:ntp], lpp, tgp, x1p, mod_p, True, final_norm)
    y_sample = _combine(ys, table[ntp:], lps, tgs, x1s, mod_s, False, final_norm)

    new_cache_k = kc.reshape(bp, 1, sp, KV_HEADS, HEAD_DIM)
    new_cache_v = vc.reshape(bp, 1, sp, KV_HEADS, HEAD_DIM)
    return (y_prompt, y_sample, new_cache_k, new_cache_v, sf[:, None], sb[:, None])
```

```python
import functools

import numpy as np
import jax
import jax.numpy as jnp
from jax import lax
from jax.experimental import pallas as pl
from jax.experimental.pallas import tpu as pltpu

F32 = jnp.float32
BF16 = jnp.bfloat16

D_MODEL = 1024
GRID_W = 64
GLA_HEADS = 4
GLA_DV = 128
GLA_DK = 64
GLA_GATE_RANK = 16
GLA_GATE_NORM = 16.0
HEAD_DIM = 64
ATT_HEADS = 8
KV_HEADS = 2
ROPE_THETA = 10000.0
ROPE_AXIS_PAIRS = HEAD_DIM // 4
N_EXPERTS = 32
TOP_K = 4
EXPERT_FF = 1024
SWIGLU_LIMIT = 7.0
SWIGLU_ALPHA = 1.702
RMS_EPS = 1e-6

W_QK = GLA_HEADS * GLA_DK
W_V = GLA_HEADS * GLA_DV
W_ATT = ATT_HEADS * HEAD_DIM
W_KV = KV_HEADS * HEAD_DIM
LANES = 128
D_IN_PAD = 2 * W_QK + 2 * W_V + W_ATT + 2 * W_KV + LANES

PREMIX_TILE = 512
TOKEN_TILE = 256
GLA_BLOCK = 256
GLA_LEVELS = ((256, 32), (32, 8), (8, 1))
ATT_Q_TILE = 512
EXPERT_TILE = 512
VMEM_LIMIT = 56 * 1024 * 1024


def _split_bf16(x):
    hi = x.astype(BF16)
    lo = (x - hi.astype(F32)).astype(BF16)
    return hi, lo


def _dot(a, b):
    return jnp.dot(a, b, preferred_element_type=F32)


def _dot_nt(a, b):
    return lax.dot_general(a, b, (((1,), (1,)), ((), ())), preferred_element_type=F32)


def _dot_tn(a, b):
    return lax.dot_general(a, b, (((0,), (0,)), ((), ())), preferred_element_type=F32)


U32 = jnp.uint32


def _pack_bf16_pairs(x):
    w = x.shape[1] // 2
    bits = lambda t: lax.bitcast_convert_type(t.astype(BF16).astype(F32), U32)
    return (bits(x[:, :w]) >> 16) | (bits(x[:, w:]) & jnp.uint32(0xFFFF0000))


def _unpack_bf16_pairs(w):
    lo = lax.bitcast_convert_type(w << 16, F32)
    hi = lax.bitcast_convert_type(w & jnp.uint32(0xFFFF0000), F32)
    return jnp.concatenate([lo, hi], axis=1).astype(BF16)


def _adaln_kernel(c_ref, w_ref, b_ref, o_ref):
    c = c_ref[...]
    a = c * jax.nn.sigmoid(c)
    ah, al = _split_bf16(a)
    wh, wl = _split_bf16(w_ref[...])
    o_ref[...] = _dot(ah, wh) + _dot(al, wh) + _dot(ah, wl) + b_ref[...]


def _adaln(conds, w_ada, b_ada):
    m, d = conds.shape
    n = w_ada.shape[1]
    tn = 512
    return pl.pallas_call(
        _adaln_kernel,
        grid=(n // tn,),
        in_specs=[pl.BlockSpec((m, d), lambda j: (0, 0)),
                  pl.BlockSpec((d, tn), lambda j: (0, j)),
                  pl.BlockSpec((1, tn), lambda j: (0, j))],
        out_specs=pl.BlockSpec((m, tn), lambda j: (0, j)),
        out_shape=jax.ShapeDtypeStruct((m, n), F32),
        name="adaln",
    )(conds, w_ada, b_ada.reshape(1, n))


def _rope_rotate(x, c, s):
    n = x.shape[-1]
    lane = lax.broadcasted_iota(jnp.int32, x.shape, 1)
    partner = jnp.where((lane & 1) == 0, pltpu.roll(x, n - 1, 1), pltpu.roll(x, 1, 1))
    return x * c + partner * s


def _premix_kernel(*refs, use_rope):
    (x_ref, mod_ref, nmix_ref, win_ref, wgh_ref, wgl_ref, bg_ref, bq_ref, bk_ref,
     qn_ref, kn_ref) = refs[:11]
    n_in = 11
    if use_rope:
        cq_ref, sq_ref, ck_ref, sk_ref = refs[11:15]
        n_in = 15
    (qa_o, ka_o, va_o, ga_o, laf_o, lab_o, q_o, k_o, v_o, kc_o, vc_o) = refs[n_in:]

    x = x_ref[0]
    mod = mod_ref[0]
    sh1 = mod[0:1]
    sc1 = mod[1:2]
    ms = jnp.mean(x * x, axis=-1, keepdims=True)
    h = x * lax.rsqrt(ms + RMS_EPS) * nmix_ref[...]
    h = h * (1.0 + sc1) + sh1
    proj = _dot(h.astype(BF16), win_ref[...])

    o = 0
    qa_o[0] = proj[:, o:o + W_QK] * (GLA_DK ** -0.5); o += W_QK
    ka_o[0] = proj[:, o:o + W_QK]; o += W_QK
    va_o[0] = proj[:, o:o + W_V]; o += W_V
    ga_o[0] = proj[:, o:o + W_V]; o += W_V
    qb = proj[:, o:o + W_ATT]; o += W_ATT
    kb = proj[:, o:o + W_KV]; o += W_KV
    vb = proj[:, o:o + W_KV]; o += W_KV
    lr = proj[:, o:o + LANES]

    lh, ll = _split_bf16(lr)
    xg = _dot(lh, wgh_ref[...]) + _dot(ll, wgh_ref[...]) + _dot(lh, wgl_ref[...]) + bg_ref[...]
    la = (jnp.minimum(xg, 0.0) - jnp.log1p(jnp.exp(-jnp.abs(xg)))) * (1.0 / GLA_GATE_NORM)
    laf_o[0] = la[:, :W_QK]
    lab_o[0] = la[:, W_QK:]

    def head_norm(t, ones_ref, g_ref):
        sh, sl = _split_bf16(t * t)
        msq = (_dot(sh, ones_ref[...]) + _dot(sl, ones_ref[...])) * (1.0 / HEAD_DIM)
        return t * lax.rsqrt(msq + RMS_EPS) * g_ref[...]

    qh = head_norm(qb, bq_ref, qn_ref)
    kh = head_norm(kb, bk_ref, kn_ref)
    kc_o[0] = kh
    vc_o[0] = vb
    if use_rope:
        qh = _rope_rotate(qh, cq_ref[...], sq_ref[...])
        kr = _rope_rotate(kh, ck_ref[...], sk_ref[...])
    else:
        kr = kh
    q_o[0] = (qh * (HEAD_DIM ** -0.5)).astype(BF16)
    k_o[0] = kr.astype(BF16)
    v_o[0] = vb.astype(BF16)


def _premix(x, mod, mod_shared, wp, rope):
    b, s, d = x.shape
    tm = min(PREMIX_TILE, s)
    use_rope = rope is not None
    const = lambda shape: pl.BlockSpec(shape, lambda i, j: (0,) * len(shape))
    tok = lambda w: pl.BlockSpec((1, tm, w), lambda i, j: (i, j, 0))
    mod_map = (lambda i, j: (0, 0, 0)) if mod_shared else (lambda i, j: (i, 0, 0))
    in_specs = [tok(d), pl.BlockSpec((1, 8, d), mod_map), const((1, d)),
                const((d, D_IN_PAD)), const((LANES, 2 * W_QK)), const((LANES, 2 * W_QK)),
                const((1, 2 * W_QK)), const((W_ATT, W_ATT)), const((W_KV, W_KV)),
                const((1, W_ATT)), const((1, W_KV))]
    args = [x, mod, wp["norm_mix"], wp["w_in"], wp["wg_hi"], wp["wg_lo"], wp["b_gate"],
            wp["ones_q"], wp["ones_k"], wp["q_norm"], wp["k_norm"]]
    if use_rope:
        seq = lambda w: pl.BlockSpec((tm, w), lambda i, j: (j, 0))
        in_specs += [seq(W_ATT), seq(W_ATT), seq(W_KV), seq(W_KV)]
        args += list(rope)
    widths = [(W_QK, F32), (W_QK, F32), (W_V, F32), (W_V, F32), (W_QK, F32), (W_QK, F32),
              (W_ATT, BF16), (W_KV, BF16), (W_KV, BF16), (W_KV, F32), (W_KV, F32)]
    return pl.pallas_call(
        functools.partial(_premix_kernel, use_rope=use_rope),
        grid=(b, s // tm),
        in_specs=in_specs,
        out_specs=[tok(w) for w, _ in widths],
        out_shape=[jax.ShapeDtypeStruct((b, s, w), dt) for w, dt in widths],
        compiler_params=pltpu.CompilerParams(
            dimension_semantics=("parallel", "parallel"), vmem_limit_bytes=VMEM_LIMIT),
        name="premix_rope" if use_rope else "premix",
    )(*args)


def _bcast_rows(x, n_par, p_rows, row):
    w = x.shape[-1]
    r = x.reshape(n_par, p_rows, w)[:, row:row + 1, :]
    return jnp.broadcast_to(r, (n_par, p_rows, w)).reshape(n_par * p_rows, w)


def _gla_kernel(q_ref, k_ref, v_ref, la_ref, s0_ref, o_ref, sf_ref, st_scr, *, rows, levels, reverse):
    blk = pl.program_id(1)

    @pl.when(blk == 0)
    def _():
        st_scr[...] = s0_ref[0]

    q = q_ref[0]
    k = k_ref[0]
    la = la_ref[0]
    vb = v_ref[0].astype(BF16)

    ri = lax.broadcasted_iota(jnp.int32, (rows, rows), 0)
    ci = lax.broadcasted_iota(jnp.int32, (rows, rows), 1)
    tri = jnp.where((ri <= ci) if reverse else (ri >= ci), 1.0, 0.0).astype(BF16)
    hi = la.astype(BF16)
    r1 = la - hi.astype(F32)
    mid = r1.astype(BF16)
    lo = (r1 - mid.astype(F32)).astype(BF16)
    cum = _dot(tri, hi) + _dot(tri, mid) + _dot(tri, lo)

    ridx = lax.broadcasted_iota(jnp.int32, (rows, W_QK), 0)
    s_acc = [None] * GLA_HEADS
    for (par, sub) in levels:
        groups = par // sub
        n_par = rows // par
        pi = (ridx % par) // sub
        edge = sub - 1 if reverse else 0
        if sub > 1:
            own = _bcast_rows(cum, rows // sub, sub, edge)
            qt = q * jnp.exp(cum - own)
        else:
            qt = q
        qps, kps = [], []
        for p in range(groups):
            if sub > 1 and p == (groups - 1 if reverse else 0):
                continue
            cp = _bcast_rows(cum, n_par, par, p * sub + edge)
            if reverse:
                km = (pi > p) if sub > 1 else (pi >= p)
            else:
                km = (pi < p) if sub > 1 else (pi <= p)
            kps.append(jnp.where(km, k * jnp.exp(cp - cum), 0.0).astype(BF16))
            qps.append(jnp.where(pi == p, qt, 0.0).astype(BF16))
        ng = len(qps)
        lane_c = lax.broadcasted_iota(jnp.int32, (rows, LANES * ng), 1)
        same_parent = (ri // par) == (ci // par)
        for hp in range(GLA_HEADS // 2):
            qc = jnp.concatenate([a[:, LANES * hp:LANES * (hp + 1)] for a in qps], axis=1)
            kc = jnp.concatenate([a[:, LANES * hp:LANES * (hp + 1)] for a in kps], axis=1)
            for hh in range(2):
                qh = jnp.where(((lane_c % LANES) // GLA_DK) == hh, qc, jnp.zeros_like(qc))
                sl = _dot_nt(qh, kc)
                if par < rows:
                    sl = jnp.where(same_parent, sl, 0.0)
                h = 2 * hp + hh
                s_acc[h] = sl if s_acc[h] is None else s_acc[h] + sl

    q0 = (q * jnp.exp(cum)).astype(BF16)
    far = 0 if reverse else rows - 1
    last = cum[far:far + 1, :]
    kd = (k * jnp.exp(last - cum)).astype(BF16)
    br = lax.broadcasted_iota(jnp.int32, (2 * GLA_DV, 2 * GLA_DK), 0)
    bc = lax.broadcasted_iota(jnp.int32, (2 * GLA_DV, 2 * GLA_DK), 1)
    blockdiag = (br // GLA_DV) == (bc // GLA_DK)
    for hp in range(GLA_HEADS // 2):
        st = st_scr[hp]
        o_inter = _dot_nt(q0[:, LANES * hp:LANES * (hp + 1)], st.astype(BF16))
        o_intra = jnp.concatenate(
            [_dot(s_acc[2 * hp + hh].astype(BF16),
                  vb[:, GLA_DV * (2 * hp + hh):GLA_DV * (2 * hp + hh + 1)]) for hh in range(2)],
            axis=1)
        o_ref[0, :, 2 * GLA_DV * hp:2 * GLA_DV * (hp + 1)] = o_inter + o_intra
        upd = _dot_tn(vb[:, 2 * GLA_DV * hp:2 * GLA_DV * (hp + 1)],
                      kd[:, LANES * hp:LANES * (hp + 1)])
        st_scr[hp] = (jnp.exp(last[:, LANES * hp:LANES * (hp + 1)]) * st
                      + jnp.where(blockdiag, upd, 0.0))

    @pl.when(blk == pl.num_programs(1) - 1)
    def _():
        sf_ref[0] = st_scr[...]


def _gla(q, k, v, la, s0t, reverse):
    n, l, _ = q.shape
    rows = min(GLA_BLOCK, l)
    nb = l // rows
    levels = tuple((min(p, rows), s) for p, s in GLA_LEVELS)
    order = (lambda j: nb - 1 - j) if reverse else (lambda j: j)
    tok = lambda w: pl.BlockSpec((1, rows, w), lambda i, j: (i, order(j), 0))
    st_spec = pl.BlockSpec((1, 2, 2 * GLA_DV, 2 * GLA_DK), lambda i, j: (i, 0, 0, 0))
    return pl.pallas_call(
        functools.partial(_gla_kernel, rows=rows, levels=levels, reverse=reverse),
        grid=(n, nb),
        in_specs=[tok(W_QK), tok(W_QK), tok(W_V), tok(W_QK), st_spec],
        out_specs=[tok(W_V), st_spec],
        out_shape=[jax.ShapeDtypeStruct((n, l, W_V), F32),
                   jax.ShapeDtypeStruct(s0t.shape, F32)],
        scratch_shapes=[pltpu.VMEM((2, 2 * GLA_DV, 2 * GLA_DK), F32)],
        compiler_params=pltpu.CompilerParams(
            dimension_semantics=("parallel", "arbitrary"), vmem_limit_bytes=VMEM_LIMIT),
        name="gla_bwd" if reverse else "gla_fwd",
    )(q, k, v, la, s0t)


def _state_to_blockdiag_t(s):
    n = s.shape[0]
    st = jnp.swapaxes(s, -1, -2).reshape(n, 2, 2, GLA_DV, GLA_DK)
    z = jnp.zeros_like(st[:, :, 0])
    top = jnp.concatenate([st[:, :, 0], z], axis=-1)
    bot = jnp.concatenate([z, st[:, :, 1]], axis=-1)
    return jnp.concatenate([top, bot], axis=-2)


def _blockdiag_t_to_state(sb):
    n = sb.shape[0]
    h0 = sb[:, :, :GLA_DV, :GLA_DK]
    h1 = sb[:, :, GLA_DV:, GLA_DK:]
    st = jnp.stack([h0, h1], axis=2).reshape(n, GLA_HEADS, GLA_DV, GLA_DK)
    return jnp.swapaxes(st, -1, -2)


def _attn_kernel(*refs, s_self, n_ctx):
    if n_ctx:
        q_ref, k_ref, v_ref, ck_ref, cv_ref, o_ref, kg_scr, v_scr = refs
    else:
        q_ref, k_ref, v_ref, o_ref, kg_scr, v_scr = refs

    @pl.when(pl.program_id(1) == 0)
    def _():
        def put(kk, vv, start, n):
            lane = lax.broadcasted_iota(jnp.int32, kk.shape, 1)
            kg_scr[0, start:start + n, :] = jnp.where(lane < HEAD_DIM, kk, jnp.zeros_like(kk))
            kg_scr[1, start:start + n, :] = jnp.where(lane >= HEAD_DIM, kk, jnp.zeros_like(kk))
            v_scr[start:start + n, :] = vv
        put(k_ref[0], v_ref[0], 0, s_self)
        if n_ctx:
            put(ck_ref[0].astype(BF16), cv_ref[0].astype(BF16), s_self, n_ctx)

    q = q_ref[0]
    vv = v_scr[...]
    for m in range(ATT_HEADS // KV_HEADS):
        qm = q[:, LANES * m:LANES * (m + 1)]
        og = []
        for g in range(KV_HEADS):
            s = _dot_nt(qm, kg_scr[g])
            mx = jnp.max(s, axis=-1, keepdims=True)
            p = jnp.exp(s - mx)
            l = jnp.sum(p, axis=-1, keepdims=True)
            og.append(_dot(p.astype(BF16), vv) / l)
        lane = lax.broadcasted_iota(jnp.int32, og[0].shape, 1)
        o_ref[0, :, LANES * m:LANES * (m + 1)] = jnp.where(lane < HEAD_DIM, og[0], og[1]).astype(BF16)


def _attention(q, k, v, cache_k=None, cache_v=None):
    b, s, _ = q.shape
    n_ctx = 0 if cache_k is None else cache_k.shape[1]
    tq = min(ATT_Q_TILE, s)
    sk = s + n_ctx
    full = lambda n, w: pl.BlockSpec((1, n, w), lambda i, j: (i, 0, 0))
    in_specs = [pl.BlockSpec((1, tq, W_ATT), lambda i, j: (i, j, 0)), full(s, W_KV), full(s, W_KV)]
    args = [q, k, v]
    if n_ctx:
        in_specs += [full(n_ctx, W_KV), full(n_ctx, W_KV)]
        args += [cache_k, cache_v]
    return pl.pallas_call(
        functools.partial(_attn_kernel, s_self=s, n_ctx=n_ctx),
        grid=(b, s // tq),
        in_specs=in_specs,
        out_specs=pl.BlockSpec((1, tq, W_ATT), lambda i, j: (i, j, 0)),
        out_shape=jax.ShapeDtypeStruct((b, s, W_ATT), BF16),
        scratch_shapes=[pltpu.VMEM((KV_HEADS, sk, W_KV), BF16), pltpu.VMEM((sk, W_KV), BF16)],
        compiler_params=pltpu.CompilerParams(
            dimension_semantics=("parallel", "arbitrary"), vmem_limit_bytes=VMEM_LIMIT),
        name="attention_ctx" if n_ctx else "attention",
    )(*args)


def _postmix_kernel(of_ref, or_ref, ga_ref, oa_ref, x_ref, mod_ref, gn_ref, wout_ref, nffn_ref,
                    wrh_ref, wrl_ref, br_ref, x1_o, h2_o, lp_o, tg_o, meta_o):
    mod = mod_ref[0]
    g1, sh2, sc2 = mod[2:3], mod[3:4], mod[4:5]
    og = of_ref[0] + or_ref[0]
    ga = ga_ref[0]
    parts = []
    for h in range(GLA_HEADS):
        blk = og[:, GLA_DV * h:GLA_DV * (h + 1)]
        ms = jnp.mean(blk * blk, axis=-1, keepdims=True)
        gh = ga[:, GLA_DV * h:GLA_DV * (h + 1)]
        parts.append((blk * lax.rsqrt(ms + RMS_EPS) * gn_ref[...] * (gh * jax.nn.sigmoid(gh))).astype(BF16))
    mix = jnp.concatenate(parts + [oa_ref[0]], axis=1)
    mo = _dot(mix, wout_ref[...])
    x1 = x_ref[0] + g1 * mo
    x1_o[0] = x1
    ms = jnp.mean(x1 * x1, axis=-1, keepdims=True)
    h2 = x1 * lax.rsqrt(ms + RMS_EPS) * nffn_ref[...]
    h2 = h2 * (1.0 + sc2) + sh2
    hh, hl = _split_bf16(h2)
    h2_o[0] = hh

    lg = _dot(hh, wrh_ref[...]) + _dot(hl, wrh_ref[...]) + _dot(hh, wrl_ref[...])
    lt = jnp.transpose(lg)[:N_EXPERTS] + br_ref[...]
    tm = lt.shape[1]
    eidx = lax.broadcasted_iota(jnp.int32, lt.shape, 0)
    vals, sels = [], []
    for _ in range(TOP_K):
        mx = jnp.max(lt, axis=0, keepdims=True)
        idx = jnp.min(jnp.where(lt == mx, eidx, N_EXPERTS), axis=0, keepdims=True)
        sel = eidx == idx
        lt = jnp.where(sel, -jnp.inf, lt)
        vals.append(mx)
        sels.append(sel)
    ws = [jnp.exp(vv - vals[0]) for vv in vals]
    tot = ws[0] + ws[1] + ws[2] + ws[3]
    tg_o[0] = jnp.concatenate([w / tot for w in ws] + [jnp.zeros((8 - TOP_K, tm), F32)], axis=0)

    onehot = sum(jnp.where(s, 1.0, 0.0) for s in sels)
    ti = lax.broadcasted_iota(jnp.int32, (tm, tm), 0)
    tj = lax.broadcasted_iota(jnp.int32, (tm, tm), 1)
    rank = _dot(onehot.astype(BF16), jnp.where(ti < tj, 1.0, 0.0).astype(BF16))
    cnt = jnp.sum(onehot, axis=1, keepdims=True)
    cnt = jnp.floor((cnt + 7.0) * 0.125) * 8.0
    ei = lax.broadcasted_iota(jnp.int32, (N_EXPERTS, N_EXPERTS), 0)
    ej = lax.broadcasted_iota(jnp.int32, (N_EXPERTS, N_EXPERTS), 1)
    cnt_b = jnp.broadcast_to(cnt, (N_EXPERTS, tm))
    seg = _dot(jnp.where(ej < ei, 1.0, 0.0).astype(BF16), cnt_b.astype(BF16))
    base = seg + rank
    lpos = [jnp.sum(jnp.where(s, base, 0.0), axis=0, keepdims=True).astype(jnp.int32) for s in sels]
    lp_o[0] = jnp.concatenate(lpos + [jnp.zeros((8 - TOP_K, tm), jnp.int32)], axis=0)
    meta_o[0] = jnp.concatenate([cnt_b[:, :LANES], seg[:, :LANES]], axis=0).astype(jnp.int32)


def _postmix(o_f, o_r, ga, o_att, x, mod, mod_shared, wp):
    b, s, d = x.shape
    tm = min(TOKEN_TILE, s)
    nt = s // tm
    const = lambda shape: pl.BlockSpec(shape, lambda i, j: (0,) * len(shape))
    tok = lambda w: pl.BlockSpec((1, tm, w), lambda i, j: (i, j, 0))
    mod_map = (lambda i, j: (0, 0, 0)) if mod_shared else (lambda i, j: (i, 0, 0))
    lane_tok = pl.BlockSpec((1, 8, tm), lambda i, j: (i * nt + j, 0, 0))
    return pl.pallas_call(
        _postmix_kernel,
        grid=(b, nt),
        in_specs=[tok(W_V), tok(W_V), tok(W_V), tok(W_ATT), tok(d), pl.BlockSpec((1, 8, d), mod_map),
                  const((1, GLA_DV)), const((d, d)), const((1, d)),
                  const((d, LANES)), const((d, LANES)), const((N_EXPERTS, 1))],
        out_specs=[tok(d), tok(d), lane_tok, lane_tok,
                   pl.BlockSpec((1, 2 * N_EXPERTS, LANES), lambda i, j: (i * nt + j, 0, 0))],
        out_shape=[jax.ShapeDtypeStruct((b, s, d), F32), jax.ShapeDtypeStruct((b, s, d), BF16),
                   jax.ShapeDtypeStruct((b * nt, 8, tm), jnp.int32),
                   jax.ShapeDtypeStruct((b * nt, 8, tm), F32),
                   jax.ShapeDtypeStruct((b * nt, 2 * N_EXPERTS, LANES), jnp.int32)],
        compiler_params=pltpu.CompilerParams(
            dimension_semantics=("parallel", "parallel"), vmem_limit_bytes=VMEM_LIMIT),
        name="postmix",
    )(o_f, o_r, ga, o_att, x, mod, wp["gla_norm"], wp["w_out"], wp["norm_ffn"],
      wp["wr_hi"], wp["wr_lo"], wp["b_router"])


SUBLANES = 8
SEG_PIECES = tuple(TOKEN_TILE >> b for b in range(6))
PAIR_ROWS = TOKEN_TILE * TOP_K
LOCAL_ROWS = PAIR_ROWS + N_EXPERTS * SUBLANES
TAB_REM = 3 * N_EXPERTS


def _piece_priority(size):
    return SEG_PIECES.index(size) % 2


def _pieces(n, emit):
    for size in SEG_PIECES:
        done = n & ~(2 * size - 1)
        pl.when((n & size) != 0)(functools.partial(emit, done, size))


def _segment_copies(tab_ref, copy):
    def body(e, carry):
        dst0 = tab_ref[0, 0, e]
        n = tab_ref[0, 0, N_EXPERTS + e]
        src0 = tab_ref[0, 0, 2 * N_EXPERTS + e]
        _pieces(n, lambda done, size: copy(pl.multiple_of(src0 + done, SUBLANES),
                                           pl.multiple_of(dst0 + done, SUBLANES), size))
        return carry
    lax.fori_loop(0, N_EXPERTS, body, 0)


def _wait_rows(rem, wait_piece):
    wait_piece(PAIR_ROWS)
    _pieces(rem, lambda done, size: wait_piece(size))


def _pair_onehot(lp_ref, weights=None):
    tm = lp_ref.shape[-1]
    rows = lax.broadcasted_iota(jnp.int32, (LOCAL_ROWS, tm), 0)
    acc = jnp.zeros((LOCAL_ROWS, tm), F32)
    for r in range(TOP_K):
        w = 1.0 if weights is None else weights[r]
        acc = jnp.where(rows == lp_ref[0, r:r + 1, :], w, acc)
    return acc


def _dispatch_kernel(tab_ref, pad_ref, ha_ref, hb_ref, lp_ref, xs_hbm, xl, zbuf, hbuf, rem_prev, sem, zsem, *,
                     tiles_a):
    i = pl.program_id(0)
    slot = i % 2
    perm = _pair_onehot(lp_ref).astype(BF16)

    @pl.when(i < tiles_a)
    def _():
        hbuf[...] = ha_ref[...]

    @pl.when(i >= tiles_a)
    def _():
        hbuf[...] = hb_ref[...]

    xl[slot] = _pack_bf16_pairs(_dot(perm, hbuf[...]))

    def copy(local, glob, size):
        pltpu.make_async_copy(xl.at[slot, pl.ds(local, size)], xs_hbm.at[pl.ds(glob, size)],
                              sem.at[slot]).start(priority=_piece_priority(size))
    _segment_copies(tab_ref, copy)

    def wait_slot(s, rem):
        _wait_rows(rem, lambda size: pltpu.make_async_copy(
            xl.at[s, pl.ds(0, size)], xs_hbm.at[pl.ds(0, size)], sem.at[s]).wait())

    @pl.when(i > 0)
    def _():
        wait_slot(1 - slot, rem_prev[0])
    rem_prev[0] = tab_ref[0, 0, TAB_REM]

    @pl.when(i == pl.num_programs(0) - 1)
    def _():
        wait_slot(slot, tab_ref[0, 0, TAB_REM])
        zbuf[...] = jnp.zeros_like(zbuf)

        def pads(wait):
            def body(e, carry):
                start = pad_ref[0, e]
                def emit(done, size):
                    cp = pltpu.make_async_copy(
                        zbuf.at[pl.ds(0, size)],
                        xs_hbm.at[pl.ds(pl.multiple_of(start + done, SUBLANES), size)], zsem)
                    cp.wait() if wait else cp.start()
                _pieces(pad_ref[0, N_EXPERTS + e], emit)
                return carry
            lax.fori_loop(0, N_EXPERTS, body, 0)
        n_tiles = xs_hbm.shape[0] // EXPERT_TILE

        def tail(wait):
            def body(t, carry):
                cp = pltpu.make_async_copy(
                    zbuf, xs_hbm.at[pl.ds(pl.multiple_of(t * EXPERT_TILE, EXPERT_TILE), EXPERT_TILE)], zsem)
                cp.wait() if wait else cp.start()
                return carry
            lax.fori_loop(pad_ref[0, 2 * N_EXPERTS], n_tiles, body, 0)
        pads(False)
        tail(False)
        pads(True)
        tail(True)


def _dispatch(h_a, h_b, lpos, table, pad_table, n_rows):
    d = h_a.shape[1]
    tm = TOKEN_TILE
    ta, tb = h_a.shape[0] // tm, h_b.shape[0] // tm
    return pl.pallas_call(
        functools.partial(_dispatch_kernel, tiles_a=ta),
        grid=(ta + tb,),
        in_specs=[pl.BlockSpec((1, 1, LANES), lambda i: (i, 0, 0), memory_space=pltpu.SMEM),
                  pl.BlockSpec((1, LANES), lambda i: (0, 0), memory_space=pltpu.SMEM),
                  pl.BlockSpec((tm, d), lambda i: (jnp.minimum(i, ta - 1), 0)),
                  pl.BlockSpec((tm, d), lambda i: (jnp.maximum(i - ta, 0), 0)),
                  pl.BlockSpec((1, 8, tm), lambda i: (i, 0, 0))],
        out_specs=pl.BlockSpec(memory_space=pl.ANY),
        out_shape=jax.ShapeDtypeStruct((n_rows, d // 2), U32),
        scratch_shapes=[pltpu.VMEM((2, LOCAL_ROWS, d // 2), U32), pltpu.VMEM((EXPERT_TILE, d // 2), U32),
                        pltpu.VMEM((tm, d), BF16), pltpu.SMEM((1,), jnp.int32),
                        pltpu.SemaphoreType.DMA((2,)), pltpu.SemaphoreType.DMA(())],
        compiler_params=pltpu.CompilerParams(
            dimension_semantics=("arbitrary",), vmem_limit_bytes=VMEM_LIMIT),
        name="dispatch",
    )(table, pad_table, h_a, h_b, lpos)


def _experts_kernel(te_ref, nv_ref, xs_ref, wgu_ref, bgu_ref, wd_ref, bd_ref, o_ref, wgu_bf, wd_bf):
    i = pl.program_id(0)
    valid = i < nv_ref[0]
    new_expert = jnp.logical_or(i == 0, te_ref[i] != te_ref[jnp.maximum(i - 1, 0)])

    @pl.when(jnp.logical_and(valid, new_expert))
    def _():
        wgu_bf[...] = wgu_ref[0].astype(BF16)
        wd_bf[...] = wd_ref[0].astype(BF16)

    @pl.when(valid)
    def _():
        x = _unpack_bf16_pairs(xs_ref[...])
        gu = _dot(x, wgu_bf[...]) + bgu_ref[0]
        g = jnp.minimum(gu[:, :EXPERT_FF], SWIGLU_LIMIT)
        u = jnp.clip(gu[:, EXPERT_FF:], -SWIGLU_LIMIT, SWIGLU_LIMIT)
        act = (u + 1.0) * (g * jax.nn.sigmoid(SWIGLU_ALPHA * g))
        o_ref[...] = _pack_bf16_pairs(_dot(act.astype(BF16), wd_bf[...]) + bd_ref[0])

    @pl.when(jnp.logical_not(valid))
    def _():
        o_ref[...] = jnp.zeros_like(o_ref)


def _experts(xs, tile_expert, n_valid, w_gu, b_gu, w_down, b_down):
    n_rows = xs.shape[0]
    tm = EXPERT_TILE
    nt = n_rows // tm
    ne, d, ff2 = w_gu.shape
    ff = ff2 // 2
    grid_spec = pltpu.PrefetchScalarGridSpec(
        num_scalar_prefetch=2,
        grid=(nt,),
        in_specs=[pl.BlockSpec((tm, d // 2), lambda i, te, nv: (i, 0)),
                  pl.BlockSpec((1, d, ff2), lambda i, te, nv: (te[i], 0, 0)),
                  pl.BlockSpec((1, 1, ff2), lambda i, te, nv: (te[i], 0, 0)),
                  pl.BlockSpec((1, ff, d), lambda i, te, nv: (te[i], 0, 0)),
                  pl.BlockSpec((1, 1, d), lambda i, te, nv: (te[i], 0, 0))],
        out_specs=pl.BlockSpec((tm, d // 2), lambda i, te, nv: (i, 0)),
        scratch_shapes=[pltpu.VMEM((d, ff2), BF16), pltpu.VMEM((ff, d), BF16)],
    )
    return pl.pallas_call(
        _experts_kernel,
        grid_spec=grid_spec,
        out_shape=jax.ShapeDtypeStruct((n_rows, d // 2), U32),
        compiler_params=pltpu.CompilerParams(
            dimension_semantics=("arbitrary",), vmem_limit_bytes=VMEM_LIMIT),
        name="experts",
    )(tile_expert, n_valid, xs, w_gu, b_gu.reshape(ne, 1, ff2), w_down, b_down.reshape(ne, 1, d))


def _combine_kernel(tab_ref, ys_hbm, lp_ref, tg_ref, x1_ref, mod_ref, fn_ref, y_ref, buf, sem):
    @pl.when(jnp.logical_and(pl.program_id(0) == 0, pl.program_id(1) == 0))
    def _():
        buf[...] = jnp.zeros_like(buf)

    def copy(local, glob, size):
        pltpu.make_async_copy(ys_hbm.at[pl.ds(glob, size)], buf.at[pl.ds(local, size)],
                              sem).start(priority=_piece_priority(size))
    _segment_copies(tab_ref, copy)
    tg = tg_ref[0]
    wg = _pair_onehot(lp_ref, [tg[r:r + 1, :] for r in range(TOP_K)]).astype(BF16)
    _wait_rows(tab_ref[0, 0, TAB_REM], lambda size: pltpu.make_async_copy(
        ys_hbm.at[pl.ds(0, size)], buf.at[pl.ds(0, size)], sem).wait())
    nrow = PAIR_ROWS + tab_ref[0, 0, TAB_REM]
    rowi = lax.broadcasted_iota(jnp.int32, buf.shape, 0)
    yb = _unpack_bf16_pairs(jnp.where(rowi < nrow, buf[...], jnp.uint32(0)))
    y = _dot_tn(wg, yb)
    g2 = mod_ref[0][5:6]
    x2 = x1_ref[0] + g2 * y
    ms = jnp.mean(x2 * x2, axis=-1, keepdims=True)
    y_ref[0] = x2 * lax.rsqrt(ms + RMS_EPS) * fn_ref[...]


def _combine(ys, table, lpos, gates, x1, mod, mod_shared, final_norm):
    b, s, d = x1.shape
    tm = TOKEN_TILE
    nt = s // tm
    mod_map = (lambda i, j: (0, 0, 0)) if mod_shared else (lambda i, j: (i, 0, 0))
    tile = lambda shape: pl.BlockSpec(shape, lambda i, j: (i * nt + j, 0, 0))
    return pl.pallas_call(
        _combine_kernel,
        grid=(b, nt),
        in_specs=[pl.BlockSpec((1, 1, LANES), lambda i, j: (i * nt + j, 0, 0), memory_space=pltpu.SMEM),
                  pl.BlockSpec(memory_space=pl.ANY),
                  tile((1, 8, tm)), tile((1, 8, tm)),
                  pl.BlockSpec((1, tm, d), lambda i, j: (i, j, 0)),
                  pl.BlockSpec((1, 8, d), mod_map),
                  pl.BlockSpec((1, d), lambda i, j: (0, 0))],
        out_specs=pl.BlockSpec((1, tm, d), lambda i, j: (i, j, 0)),
        out_shape=jax.ShapeDtypeStruct((b, s, d), F32),
        scratch_shapes=[pltpu.VMEM((LOCAL_ROWS, d // 2), U32), pltpu.SemaphoreType.DMA(())],
        compiler_params=pltpu.CompilerParams(
            dimension_semantics=("arbitrary", "arbitrary"), vmem_limit_bytes=VMEM_LIMIT),
        name="combine",
    )(table, ys, lpos, gates, x1, mod, final_norm.reshape(1, d))


def _route_tables(meta, n_tiles_e):
    tm = EXPERT_TILE
    cnt = meta[:, :N_EXPERTS, 0]
    lstart = meta[:, N_EXPERTS:, 0]
    tot = jnp.sum(cnt, axis=0)
    tiles = (tot + tm - 1) // tm
    tile_end = jnp.cumsum(tiles)
    off = (tile_end - tiles) * tm
    dest = off[None, :] + jnp.cumsum(cnt, axis=0) - cnt
    rem = jnp.sum(cnt, axis=1, keepdims=True) - PAIR_ROWS
    table = jnp.concatenate([dest, cnt, lstart, jnp.broadcast_to(rem, cnt.shape)], axis=1).astype(jnp.int32)
    n_valid = tile_end[-1].astype(jnp.int32).reshape(1)
    pad_table = jnp.concatenate([off + tot, tiles * tm - tot, jnp.broadcast_to(n_valid, (2 * N_EXPERTS,))])
    pad_table = pad_table.astype(jnp.int32).reshape(1, LANES)
    tile_ids = jnp.minimum(jnp.arange(n_tiles_e), n_valid[0] - 1)
    tile_expert = jnp.sum((tile_end[None, :] <= tile_ids[:, None]).astype(jnp.int32), axis=1)
    return table.reshape(-1, 1, LANES), pad_table, tile_expert, n_valid


def _rope_tables(n_tok):
    rows = n_tok // GRID_W
    r, col = jnp.meshgrid(jnp.arange(rows), jnp.arange(GRID_W), indexing="ij")
    r = r.reshape(-1).astype(F32)
    col = col.reshape(-1).astype(F32)
    inv = 1.0 / (ROPE_THETA ** (jnp.arange(ROPE_AXIS_PAIRS, dtype=F32) / ROPE_AXIS_PAIRS))
    ang = jnp.concatenate([r[:, None] * inv, col[:, None] * inv], axis=-1)
    c64 = jnp.repeat(jnp.cos(ang), 2, axis=-1)
    sign = jnp.tile(jnp.array([-1.0, 1.0], F32), HEAD_DIM // 2)
    s64 = jnp.repeat(jnp.sin(ang), 2, axis=-1) * sign
    return (jnp.tile(c64, (1, ATT_HEADS)), jnp.tile(s64, (1, ATT_HEADS)),
            jnp.tile(c64, (1, KV_HEADS)), jnp.tile(s64, (1, KV_HEADS)))


def _prep_weights(w_in, w_gate_f, b_gate_f, w_gate_b, b_gate_b, gla_norm, q_norm, k_norm, w_out,
                  norm_mix, norm_ffn, w_router, b_router):
    d = w_in.shape[0]
    o_lr = 2 * W_QK + 2 * W_V
    o_q = o_lr + 2 * GLA_GATE_RANK
    o_k = o_q + W_ATT
    head_order = (0, 4, 1, 5, 2, 6, 3, 7)
    w_q = [w_in[:, o_q + h * HEAD_DIM:o_q + (h + 1) * HEAD_DIM] for h in head_order]
    lr_pad = jnp.zeros((d, LANES - 2 * GLA_GATE_RANK), w_in.dtype)
    w_in_p = jnp.concatenate([w_in[:, :o_lr]] + w_q + [w_in[:, o_k:], w_in[:, o_lr:o_q], lr_pad], axis=1)
    wg = jnp.zeros((LANES, 2 * W_QK), F32)
    wg = wg.at[:GLA_GATE_RANK, :W_QK].set(w_gate_f)
    wg = wg.at[GLA_GATE_RANK:2 * GLA_GATE_RANK, W_QK:].set(w_gate_b)
    wg_hi, wg_lo = _split_bf16(wg)
    ones_q = jnp.asarray(np.kron(np.eye(ATT_HEADS), np.ones((HEAD_DIM, HEAD_DIM))), BF16)
    ones_k = jnp.asarray(np.kron(np.eye(KV_HEADS), np.ones((HEAD_DIM, HEAD_DIM))), BF16)
    w_out_p = jnp.concatenate(
        [w_out[:W_V]] + [w_out[W_V + h * HEAD_DIM:W_V + (h + 1) * HEAD_DIM] for h in head_order],
        axis=0).astype(BF16)
    wr_hi, wr_lo = _split_bf16(jnp.pad(w_router, ((0, 0), (0, LANES - N_EXPERTS))))
    return {
        "norm_mix": norm_mix.reshape(1, d), "w_in": w_in_p.astype(BF16),
        "wg_hi": wg_hi, "wg_lo": wg_lo,
        "b_gate": jnp.concatenate([b_gate_f, b_gate_b]).reshape(1, 2 * W_QK),
        "ones_q": ones_q, "ones_k": ones_k,
        "q_norm": jnp.tile(q_norm, ATT_HEADS).reshape(1, W_ATT),
        "k_norm": jnp.tile(k_norm, KV_HEADS).reshape(1, W_KV),
        "gla_norm": gla_norm.reshape(1, GLA_DV), "w_out": w_out_p,
        "norm_ffn": norm_ffn.reshape(1, d), "wr_hi": wr_hi, "wr_lo": wr_lo,
        "b_router": b_router.reshape(N_EXPERTS, 1),
    }


def _mixer(x, mod, mod_shared, wp, rope, cache_k, cache_v, s0_f, s0_b):
    b, s, _ = x.shape
    qa, ka, va, ga, la_f, la_b, q, k, v, kc, vc = _premix(x, mod, mod_shared, wp, rope)
    o_f, sf = _gla(qa, ka, va, la_f, _state_to_blockdiag_t(s0_f), False)
    o_r, sb = _gla(qa, ka, va, la_b, _state_to_blockdiag_t(s0_b), True)
    o_att = _attention(q, k, v, cache_k, cache_v)
    x1, h2, lpos, gates, meta = _postmix(o_f, o_r, ga, o_att, x, mod, mod_shared, wp)
    return x1, h2, lpos, gates, meta, kc, vc, _blockdiag_t_to_state(sf), _blockdiag_t_to_state(sb)


def kernel(x_prompt, x_sample, c, cache_k, cache_v, state_gla_fwd, state_gla_bwd, c_ctx, w_ada, b_ada,
           norm_mix, w_in, w_gate_f, b_gate_f, w_gate_b, b_gate_b, gla_norm, q_norm, k_norm, w_out,
           norm_ffn, w_router, b_router, w_gu, b_gu, w_down, b_down, final_norm):
    bp, sp, d = x_prompt.shape
    bs, ss, _ = x_sample.shape
    assert w_ada.shape[0] == 1, "single-layer trunk"
    wp = _prep_weights(w_in[0], w_gate_f[0], b_gate_f[0], w_gate_b[0], b_gate_b[0], gla_norm[0],
                       q_norm[0], k_norm[0], w_out[0], norm_mix[0], norm_ffn[0], w_router[0], b_router[0])

    n_cond = -(-(1 + bs) // 8) * 8
    conds = jnp.zeros((n_cond, d), F32).at[0].set(c_ctx).at[1:1 + bs].set(c)
    mod = _adaln(conds, w_ada[0], b_ada[0]).reshape(n_cond, 6, d)
    mod = jnp.concatenate([mod, jnp.zeros((n_cond, 2, d), F32)], axis=1)
    mod_p, mod_s = mod[0:1], mod[1:1 + bs]

    zero_state = jnp.zeros((bp, GLA_HEADS, GLA_DK, GLA_DV), F32)
    x1p, h2p, lpp, tgp, metap, kc, vc, sf, sb = _mixer(x_prompt, mod_p, True, wp, None, None, None,
                                                       zero_state, zero_state)
    n_ctx = cache_k.shape[2]
    x1s, h2s, lps, tgs, metas, _, _, _, _ = _mixer(
        x_sample, mod_s, False, wp, _rope_tables(ss),
        cache_k[:, 0].reshape(bs, n_ctx, W_KV), cache_v[:, 0].reshape(bs, n_ctx, W_KV),
        state_gla_fwd[:, 0].astype(F32), state_gla_bwd[:, 0].astype(F32))

    n_p, n_s = bp * sp, bs * ss
    n_tok = n_p + n_s
    ntp = n_p // TOKEN_TILE
    lpos = jnp.concatenate([lpp, lps], axis=0)
    worst = n_tok * TOP_K + (n_tok // TOKEN_TILE) * N_EXPERTS * (SUBLANES - 1) + N_EXPERTS * (EXPERT_TILE - SUBLANES)
    n_rows = -(-worst // EXPERT_TILE) * EXPERT_TILE
    table, pad_table, tile_expert, n_valid = _route_tables(
        jnp.concatenate([metap, metas], axis=0), n_rows // EXPERT_TILE)
    xs = _dispatch(h2p.reshape(n_p, d), h2s.reshape(n_s, d), lpos, table, pad_table, n_rows)
    ys = _experts(xs, tile_expert, n_valid, w_gu[0], b_gu[0], w_down[0], b_down[0])
    y_prompt = _combine(ys, table[:ntp], lpp, tgp, x1p, mod_p, True, final_norm)
    y_sample = _combine(ys, table[ntp:], lps, tgs, x1s, mod_s, False, final_norm)

    new_cache_k = kc.reshape(bp, 1, sp, KV_HEADS, HEAD_DIM)
    new_cache_v = vc.reshape(bp, 1, sp, KV_HEADS, HEAD_DIM)
    return (y_prompt, y_sample, new_cache_k, new_cache_v, sf[:, None], sb[:, None])
```
